```python
import math
import jax
import jax.numpy as jnp
from jax import lax
import numpy as np

D_MODEL = 2048
BATCH = 4
SEQ = 2048
DEPTH = 2

N_HEADS = 16
HEAD_DIM = 128
ATT_WIDTH = N_HEADS * HEAD_DIM
ATT_SCALE = HEAD_DIM ** -0.5
DIL_PATTERNS = ((128, 1), (512, 4), (2048, 16))
BAND_BLOCK = 128
REL_BUCKETS = 32
REL_MAX_EXACT = REL_BUCKETS // 2
REL_MAX_DISTANCE = 2048
NSA_KV_GROUPS = 4
NSA_HEADS_PER_GROUP = N_HEADS // NSA_KV_GROUPS
NSA_BRANCHES = 3
CMP_BLOCK = 32
CMP_STRIDE = 16
CMP_HIDDEN = 256
SLC_BLOCK = 64
SLC_TOP_N = 16
SLC_QUERY_BLOCK = 32
WIN_SIZE = 512
N_A_LAYERS = DEPTH // 2
N_B_LAYERS = DEPTH - N_A_LAYERS
RMS_EPS = 1e-6
NEG_INF = -1e30
FORCE_SCORE = 1e9

kernel_name = 'yoco_dilated_nsa_hybrid'


def rms_norm(x, g):
    xf = x.astype(jnp.float32)
    y = xf * lax.rsqrt(jnp.mean(xf * xf, axis=-1, keepdims=True) + RMS_EPS)
    return (y * g.astype(jnp.float32)).astype(x.dtype)


def rel_bucket(dist):
    n = jnp.maximum(dist, 0)
    nf = jnp.maximum(n, 1).astype(jnp.float32)
    log_b = REL_MAX_EXACT + (jnp.log(nf / REL_MAX_EXACT) / math.log(REL_MAX_DISTANCE / REL_MAX_EXACT)
                             * (REL_BUCKETS - REL_MAX_EXACT)).astype(jnp.int32)
    return jnp.where(n < REL_MAX_EXACT, n, jnp.minimum(log_b, REL_BUCKETS - 1))


def rel_bias(table, dist):
    b = jnp.take(table, rel_bucket(dist), axis=0).astype(jnp.float32)
    return jnp.moveaxis(b, -1, 0)


def dilated_attention(q, k, v, table, window, dilation):
    B, H, S, Dh = q.shape
    L = S // dilation
    n_keys = window // dilation
    nb = -(-L // BAND_BLOCK)
    Lp = nb * BAND_BLOCK

    def to_sub(t):
        t = t.reshape(B, H, L, dilation, Dh).transpose(0, 1, 3, 2, 4)
        return jnp.pad(t, ((0, 0), (0, 0), (0, 0), (0, Lp - L), (0, 0)))

    def band(t):
        tp = jnp.pad(t, ((0, 0), (0, 0), (0, 0), (BAND_BLOCK, 0), (0, 0)))
        tp = tp.reshape(B, H, dilation, nb + 1, BAND_BLOCK, Dh)
        return jnp.concatenate([tp[:, :, :, :-1], tp[:, :, :, 1:]], axis=4)

    qb = to_sub(q).reshape(B, H, dilation, nb, BAND_BLOCK, Dh)
    kb = band(to_sub(k))
    vb = band(to_sub(v))
    qi = jnp.arange(BAND_BLOCK)[:, None]
    ki = jnp.arange(2 * BAND_BLOCK)[None, :]
    delta = qi + BAND_BLOCK - ki
    key_pos = jnp.arange(nb)[:, None, None] * BAND_BLOCK - BAND_BLOCK + ki[None]
    valid = (delta >= 0) & (delta <= n_keys) & (key_pos >= 0)
    bias = rel_bias(table, delta * dilation)
    s = jnp.einsum('bhrnqd,bhrnkd->bhrnqk', qb, kb).astype(jnp.float32) * ATT_SCALE + bias[None, :, None, None]
    s = jnp.where(valid, s, NEG_INF)
    m = jnp.max(s, axis=-1, keepdims=True)
    p = jnp.exp(s - m)
    den = jnp.sum(p, axis=-1, keepdims=True)
    o = jnp.einsum('bhrnqk,bhrnkd->bhrnqd', p.astype(vb.dtype), vb).astype(jnp.float32) / den
    lse = (m + jnp.log(den))[..., 0]
    o = o.reshape(B, H, dilation, Lp, Dh)[:, :, :, :L].transpose(0, 1, 3, 2, 4).reshape(B, H, S, Dh)
    lse = lse.reshape(B, H, dilation, Lp)[..., :L].transpose(0, 1, 3, 2).reshape(B, H, S)
    return o, lse


def dilated_mixer(h, w_in, w_out, table):
    B, S, _ = h.shape
    q, k, v, z = jnp.split(h @ w_in, 4, axis=-1)

    def heads(t):
        return t.reshape(B, S, N_HEADS, HEAD_DIM).transpose(0, 2, 1, 3)

    q, k, v = heads(q), heads(k), heads(v)
    outs, lses = [], []
    for window, dilation in DIL_PATTERNS:
        o, l = dilated_attention(q, k, v, table, window, dilation)
        outs.append(o)
        lses.append(l)
    alpha = jax.nn.softmax(jnp.stack(lses), axis=0)
    o = jnp.einsum('pbhs,pbhsd->bshd', alpha, jnp.stack(outs)).reshape(B, S, ATT_WIDTH).astype(h.dtype)
    return (o * jax.nn.silu(z)) @ w_out


def compress_blocks(t, pos, w1, w2):
    B, G, S, Dh = t.shape
    c = t.reshape(B, G, S // CMP_STRIDE, CMP_STRIDE, Dh)
    blocks = jnp.concatenate([c[:, :, :-1], c[:, :, 1:]], axis=3) + pos
    flat = blocks.reshape(B, G, blocks.shape[2], CMP_BLOCK * Dh)
    return jax.nn.gelu(flat @ w1) @ w2


def shared_kv(h, kv_norm, w_kv, cmp_pos_k, cmp_pos_v, cmp_w1_k, cmp_w2_k, cmp_w1_v, cmp_w2_v):
    B, S, _ = h.shape
    kv = (rms_norm(h, kv_norm) @ w_kv).reshape(B, S, 2 * NSA_BRANCHES, NSA_KV_GROUPS, HEAD_DIM)
    kv = kv.transpose(2, 0, 3, 1, 4)
    k_cmp = compress_blocks(kv[0], cmp_pos_k, cmp_w1_k, cmp_w2_k)
    v_cmp = compress_blocks(kv[1], cmp_pos_v, cmp_w1_v, cmp_w2_v)
    return k_cmp, v_cmp, kv[2], kv[3], kv[4], kv[5]


def nsa_mixer(h, w_in, w_out, table, k_cmp, v_cmp, k_slc, v_slc, k_win, v_win):
    B, S, _ = h.shape
    G, J, Dh = NSA_KV_GROUPS, NSA_HEADS_PER_GROUP, HEAD_DIM
    f32 = jnp.float32
    proj = h @ w_in
    q = proj[..., :ATT_WIDTH].reshape(B, S, G, J, Dh).transpose(0, 2, 3, 1, 4)
    z = proj[..., ATT_WIDTH:(1 + NSA_BRANCHES) * ATT_WIDTH].reshape(B, S, NSA_BRANCHES, N_HEADS, Dh)
    gate = jax.nn.sigmoid(proj[..., (1 + NSA_BRANCHES) * ATT_WIDTH:].astype(f32)).reshape(B, S, NSA_BRANCHES, N_HEADS)
    t = jnp.arange(S)

    nc = k_cmp.shape[2]
    cmp_end = jnp.arange(nc) * CMP_STRIDE + CMP_BLOCK - 1
    dist_c = t[:, None] - cmp_end[None, :]
    valid_c = dist_c >= 0
    bias_c = rel_bias(table, dist_c).reshape(G, J, S, nc)
    s_c = jnp.einsum('bgjsd,bgcd->bgjsc', q, k_cmp).astype(f32) * ATT_SCALE + bias_c
    s_c = jnp.where(valid_c, s_c, NEG_INF)
    p_c = jnp.where(valid_c, jnp.exp(s_c - jnp.max(s_c, axis=-1, keepdims=True)), 0.0)
    p_c = p_c / jnp.maximum(jnp.sum(p_c, axis=-1, keepdims=True), 1e-30)
    o_c = jnp.einsum('bgjsc,bgcd->bgjsd', p_c.astype(v_cmp.dtype), v_cmp)

    ns = S // SLC_BLOCK
    n_sel = min(SLC_TOP_N, ns)
    ci = np.arange(nc)[:, None] * CMP_STRIDE
    sj = np.arange(ns)[None, :] * SLC_BLOCK
    overlap = jnp.asarray(((ci < sj + SLC_BLOCK) & (ci + CMP_BLOCK > sj)).astype(np.float32))
    imp = jnp.einsum('bgjsc,cn->bgsn', p_c, overlap)
    cur = (t // SLC_BLOCK)[:, None]
    blk = jnp.arange(ns)[None, :]
    forced = (blk == 0) | (blk == cur) | (blk == cur - 1)
    imp = jnp.where(forced, FORCE_SCORE, jnp.where(blk > cur, -FORCE_SCORE, imp))
    _, sel = lax.top_k(imp, n_sel)

    kb = k_slc.reshape(B, G, ns, SLC_BLOCK, Dh)
    vb = v_slc.reshape(B, G, ns, SLC_BLOCK, Dh)
    nq = S // SLC_QUERY_BLOCK
    q_blocks = q.reshape(B, G, J, nq, SLC_QUERY_BLOCK, Dh).transpose(3, 0, 1, 2, 4, 5)
    sel_blocks = sel.reshape(B, G, nq, SLC_QUERY_BLOCK, n_sel).transpose(2, 0, 1, 3, 4)
    starts = jnp.arange(nq) * SLC_QUERY_BLOCK
    table_g = table.reshape(REL_BUCKETS, G, J)
    gather = jax.vmap(jax.vmap(lambda blocks, ix: blocks[ix]))
    group_bias = jax.vmap(lambda tb, bk: tb[bk], in_axes=(1, 1), out_axes=1)

    def selected_block(args):
        qb_, ix, start = args
        kg = gather(kb, ix)
        vg = gather(vb, ix)
        tq = start + jnp.arange(SLC_QUERY_BLOCK)
        dist = tq[:, None, None] - (ix[..., None] * SLC_BLOCK + jnp.arange(SLC_BLOCK))
        bias = jnp.moveaxis(group_bias(table_g, rel_bucket(dist)), -1, 2).astype(f32)
        s = jnp.einsum('bgjqd,bgqnkd->bgjqnk', qb_, kg).astype(f32) * ATT_SCALE + bias
        s = jnp.where((dist >= 0)[:, :, None], s, NEG_INF)
        p = jax.nn.softmax(s.reshape(*s.shape[:4], -1), axis=-1).reshape(s.shape)
        return jnp.einsum('bgjqnk,bgqnkd->bgjqd', p.astype(vg.dtype), vg)

    o_s = lax.map(selected_block, (q_blocks, sel_blocks, starts))
    o_s = o_s.transpose(1, 2, 3, 0, 4, 5).reshape(B, G, J, S, Dh)

    nb = S // BAND_BLOCK
    nw = WIN_SIZE // BAND_BLOCK
    kw_len = (nw + 1) * BAND_BLOCK

    def band(t_):
        tp = jnp.pad(t_, ((0, 0), (0, 0), (WIN_SIZE, 0), (0, 0))).reshape(B, G, nb + nw, BAND_BLOCK, Dh)
        return jnp.concatenate([tp[:, :, i:i + nb] for i in range(nw + 1)], axis=3)

    qw = q.reshape(B, G, J, nb, BAND_BLOCK, Dh)
    qpos = jnp.arange(nb)[:, None] * BAND_BLOCK + jnp.arange(BAND_BLOCK)[None, :]
    kpos = jnp.arange(nb)[:, None] * BAND_BLOCK - WIN_SIZE + jnp.arange(kw_len)[None, :]
    delta = qpos[:, :, None] - kpos[:, None, :]
    valid_w = (delta >= 0) & (delta < WIN_SIZE) & (kpos[:, None, :] >= 0)
    bias_w = rel_bias(table, delta).reshape(G, J, nb, BAND_BLOCK, kw_len)
    s_w = jnp.einsum('bgjnqd,bgnkd->bgjnqk', qw, band(k_win)).astype(f32) * ATT_SCALE + bias_w
    p_w = jax.nn.softmax(jnp.where(valid_w, s_w, NEG_INF), axis=-1)
    o_w = jnp.einsum('bgjnqk,bgnkd->bgjnqd', p_w.astype(v_win.dtype), band(v_win)).reshape(B, G, J, S, Dh)

    def to_tokens(o):
        return o.transpose(0, 3, 1, 2, 4).reshape(B, S, N_HEADS, Dh)

    o_all = jnp.stack([to_tokens(o_c), to_tokens(o_s), to_tokens(o_w)], axis=2)
    y = jnp.sum(gate[..., None].astype(h.dtype) * o_all * jax.nn.silu(z), axis=2)
    return y.reshape(B, S, ATT_WIDTH) @ w_out


def setup_inputs(seed: int = 0) -> dict:
    key = jax.random.key(seed)
    ks = jax.random.split(key, 16)
    f32 = jnp.float32

    def w(k, shape, fan_in):
        return jax.random.normal(k, shape, f32) * fan_in ** -0.5

    in_a = 4 * ATT_WIDTH
    in_b = (1 + NSA_BRANCHES) * ATT_WIDTH + NSA_BRANCHES * N_HEADS
    n_kv = 2 * NSA_BRANCHES * NSA_KV_GROUPS * HEAD_DIM
    return {
        'x': jax.random.normal(ks[0], (BATCH, SEQ, D_MODEL), f32),
        'norm_pre': 1.0 + 0.02 * jax.random.normal(ks[1], (DEPTH, D_MODEL), f32),
        'norm_post': 1.0 + 0.02 * jax.random.normal(ks[2], (DEPTH, D_MODEL), f32),
        'rel_table': 0.5 * jax.random.normal(ks[3], (REL_BUCKETS, N_HEADS), f32),
        'w_in_a': w(ks[4], (N_A_LAYERS, D_MODEL, in_a), D_MODEL),
        'w_out_a': w(ks[5], (N_A_LAYERS, ATT_WIDTH, D_MODEL), ATT_WIDTH),
        'kv_norm': 1.0 + 0.02 * jax.random.normal(ks[6], (D_MODEL,), f32),
        'w_kv': w(ks[7], (D_MODEL, n_kv), D_MODEL),
        'cmp_pos_k': 0.1 * jax.random.normal(ks[8], (CMP_BLOCK, HEAD_DIM), f32),
        'cmp_pos_v': 0.1 * jax.random.normal(ks[9], (CMP_BLOCK, HEAD_DIM), f32),
        'cmp_w1_k': w(ks[10], (CMP_BLOCK * HEAD_DIM, CMP_HIDDEN), CMP_BLOCK * HEAD_DIM),
        'cmp_w2_k': w(ks[11], (CMP_HIDDEN, HEAD_DIM), CMP_HIDDEN),
        'cmp_w1_v': w(ks[12], (CMP_BLOCK * HEAD_DIM, CMP_HIDDEN), CMP_BLOCK * HEAD_DIM),
        'cmp_w2_v': w(ks[13], (CMP_HIDDEN, HEAD_DIM), CMP_HIDDEN),
        'w_in_b': w(ks[14], (N_B_LAYERS, D_MODEL, in_b), D_MODEL),
        'w_out_b': w(ks[15], (N_B_LAYERS, ATT_WIDTH, D_MODEL), ATT_WIDTH),
    }


def reference(x, norm_pre, norm_post, rel_table, w_in_a, w_out_a, kv_norm, w_kv, cmp_pos_k, cmp_pos_v,
              cmp_w1_k, cmp_w2_k, cmp_w1_v, cmp_w2_v, w_in_b, w_out_b):
    h = x
    shared = None
    for layer in range(DEPTH):
        hn = rms_norm(h, norm_pre[layer])
        if layer < N_A_LAYERS:
            y = dilated_mixer(hn, w_in_a[layer], w_out_a[layer], rel_table)
        else:
            if layer == N_A_LAYERS:
                shared = shared_kv(h, kv_norm, w_kv, cmp_pos_k, cmp_pos_v, cmp_w1_k, cmp_w2_k, cmp_w1_v, cmp_w2_v)
            b = layer - N_A_LAYERS
            y = nsa_mixer(hn, w_in_b[b], w_out_b[b], rel_table, *shared)
        h = h + rms_norm(y, norm_post[layer])
    return h
```

```python
import functools
import math

import numpy as np
import jax
import jax.numpy as jnp
from jax import lax
from jax.experimental import pallas as pl
from jax.experimental.pallas import tpu as pltpu

F32 = jnp.float32
BF16 = jnp.bfloat16

D_MODEL = 2048
SEQ = 2048
N_HEADS = 16
HEAD_DIM = 128
ATT_WIDTH = N_HEADS * HEAD_DIM
ATT_SCALE = HEAD_DIM ** -0.5
DIL_PATTERNS = ((128, 1), (512, 4), (2048, 16))
REL_BUCKETS = 32
REL_MAX_EXACT = 16
REL_MAX_DISTANCE = 2048
NSA_KV_GROUPS = 4
NSA_HEADS_PER_GROUP = 4
NSA_BRANCHES = 3
CMP_BLOCK = 32
CMP_STRIDE = 16
CMP_HIDDEN = 256
SLC_BLOCK = 64
SLC_TOP_N = 16
WIN_SIZE = 512
RMS_EPS = 1e-6
NEG = -1e30
FORCE_SCORE = 1e9

TILE = 128
N_TILES = SEQ // TILE
N_CMP = SEQ // CMP_STRIDE
N_SLC = SEQ // SLC_BLOCK
WIN_TILES = WIN_SIZE // TILE + 1

TAB_A, TAB_S, TAB_C, TAB_W = 0, 16, 32, 48
TAB_ROWS = 56
VMEM_LIMIT = 48 * 1024 * 1024


def _np_bucket(dist):
    n = np.maximum(dist, 0)
    nf = np.maximum(n, 1).astype(np.float32)
    log_b = REL_MAX_EXACT + (
        np.log(nf / np.float32(REL_MAX_EXACT)) / np.float32(math.log(REL_MAX_DISTANCE / REL_MAX_EXACT))
        * np.float32(REL_BUCKETS - REL_MAX_EXACT)).astype(np.int32)
    return np.where(n < REL_MAX_EXACT, n, np.minimum(log_b, REL_BUCKETS - 1)).astype(np.int32)


@functools.lru_cache(maxsize=None)
def _static_maps():
    qi = np.arange(TILE)[:, None]
    ki = np.arange(TILE)[None, :]
    dist = TILE * np.arange(N_TILES)[:, None, None] + qi[None] - ki[None]
    bk_t = _np_bucket(dist)
    mult = np.zeros(dist.shape, np.int64)
    for window, dil in DIL_PATTERNS:
        mult += ((dist % dil == 0) & (dist <= window)).astype(np.int64)
    ok_a = (dist >= 0) & (mult > 0)
    base_a = np.where(ok_a, np.log(np.maximum(mult, 1)), NEG)
    base_s = np.where(dist >= 0, 0.0, NEG)
    base_w = np.full((8, TILE, TILE), NEG)
    base_w[:WIN_TILES] = np.where((dist >= 0) & (dist < WIN_SIZE), 0.0, NEG)[:WIN_TILES]
    t = TILE * np.arange(N_TILES)[:, None, None] + qi[None]
    dist_c = t - (CMP_STRIDE * ki[None] + CMP_BLOCK - 1)
    bk_c = _np_bucket(dist_c)
    base_c = np.where((dist_c >= 0) & (ki[None] < N_CMP - 1), 0.0, NEG)
    bk = np.concatenate([bk_t, bk_c]).astype(np.int32)
    base = np.concatenate([base_a, base_s, base_c, base_w]).astype(np.float32)
    ci = np.arange(N_CMP)[None, :] * CMP_STRIDE
    sj = np.arange(N_SLC)[:, None] * SLC_BLOCK
    ov_t = ((ci < sj + SLC_BLOCK) & (ci + CMP_BLOCK > sj) & (np.arange(N_CMP)[None, :] < N_CMP - 1))
    ov_t = ov_t.astype(np.float32)
    e = np.zeros((N_TILES, TILE, TILE), np.float32)
    for jt in range(N_TILES):
        for kk in range(TILE):
            e[jt, (jt * TILE + kk) // SLC_BLOCK, kk] = 1.0
    return bk, base, ov_t, e


def _bias_kernel(tab_ref, bk_ref, base_ref, out_ref):
    h = pl.program_id(0)
    tv = [tab_ref[b, h] for b in range(REL_BUCKETS)]

    def lookup(bk):
        val = jnp.full(bk.shape, tv[0], F32)
        for b in range(1, REL_BUCKETS):
            val = jnp.where(bk == b, tv[b], val)
        return val

    def body(d, carry):
        g = lookup(bk_ref[d])
        out_ref[TAB_A + d] = g + base_ref[TAB_A + d]
        out_ref[TAB_S + d] = g + base_ref[TAB_S + d]

        @pl.when(d < 8)
        def _():
            out_ref[TAB_W + d] = g + base_ref[TAB_W + d]

        gc = lookup(bk_ref[N_TILES + d])
        out_ref[TAB_C + d] = gc + base_ref[TAB_C + d]
        return carry

    lax.fori_loop(0, N_TILES, body, 0)


def _bias_tables(rel_table):
    bk, base, _, _ = _static_maps()
    return pl.pallas_call(
        _bias_kernel,
        grid=(N_HEADS,),
        in_specs=[
            pl.BlockSpec(memory_space=pltpu.SMEM),
            pl.BlockSpec((2 * N_TILES, TILE, TILE), lambda h: (0, 0, 0)),
            pl.BlockSpec((TAB_ROWS, TILE, TILE), lambda h: (0, 0, 0)),
        ],
        out_specs=pl.BlockSpec((None, TAB_ROWS, TILE, TILE), lambda h: (h, 0, 0, 0)),
        out_shape=jax.ShapeDtypeStruct((N_HEADS, TAB_ROWS, TILE, TILE), F32),
        compiler_params=pltpu.CompilerParams(dimension_semantics=("arbitrary",),
                                             vmem_limit_bytes=VMEM_LIMIT),
        name="bias_tables",
    )(rel_table, jnp.asarray(bk), jnp.asarray(base))


def _nm_kernel(*refs, n_chunks, gate):
    if gate:
        x_ref, g_ref, w_ref, wg_ref, o_ref, og_ref, xn_ref = refs
    else:
        x_ref, g_ref, w_ref, o_ref, xn_ref = refs

    @pl.when(pl.program_id(1) == 0)
    def _():
        x = x_ref[...]
        ms = jnp.mean(x * x, axis=-1, keepdims=True)
        xn_ref[...] = (x * lax.rsqrt(ms + RMS_EPS) * g_ref[...]).astype(BF16)
        if gate:
            og_ref[...] = jnp.dot(xn_ref[...], wg_ref[...], preferred_element_type=F32)

    res = jnp.dot(xn_ref[...], w_ref[...], preferred_element_type=F32)
    for c in range(n_chunks):
        o_ref[c] = res[:, c * HEAD_DIM:(c + 1) * HEAD_DIM].astype(BF16)


def _norm_matmul(x2d, gain, w_bf, wg_bf=None, *, tm=512, tn=1024):
    m, d = x2d.shape
    n = w_bf.shape[1]
    batch = m // SEQ
    spb = SEQ // tm
    gate = wg_bf is not None
    in_specs = [
        pl.BlockSpec((tm, d), lambda i, j: (i, 0)),
        pl.BlockSpec((1, d), lambda i, j: (0, 0)),
        pl.BlockSpec((d, tn), lambda i, j: (0, j)),
    ]
    out_shape = [jax.ShapeDtypeStruct((batch, n // HEAD_DIM, SEQ, HEAD_DIM), BF16)]
    out_specs = [pl.BlockSpec((None, tn // HEAD_DIM, tm, HEAD_DIM), lambda i, j: (i // spb, j, i % spb, 0))]
    args = [x2d, gain.reshape(1, d), w_bf]
    if gate:
        ng = wg_bf.shape[1]
        in_specs.append(pl.BlockSpec((d, ng), lambda i, j: (0, 0)))
        out_shape.append(jax.ShapeDtypeStruct((m, ng), F32))
        out_specs.append(pl.BlockSpec((tm, ng), lambda i, j: (i, 0)))
        args.append(wg_bf)
    res = pl.pallas_call(
        functools.partial(_nm_kernel, n_chunks=tn // HEAD_DIM, gate=gate),
        grid=(m // tm, n // tn),
        in_specs=in_specs,
        out_specs=out_specs,
        out_shape=out_shape,
        scratch_shapes=[pltpu.VMEM((tm, d), BF16)],
        compiler_params=pltpu.CompilerParams(dimension_semantics=("parallel", "arbitrary"),
                                             vmem_limit_bytes=VMEM_LIMIT),
        name="norm_matmul",
    )(*args)
    return res if gate else res[0]


def _op_kernel(a_ref, w_ref, r_ref, g_ref, o_ref):
    y = jnp.dot(a_ref[...], w_ref[...], preferred_element_type=F32)
    ms = jnp.mean(y * y, axis=-1, keepdims=True)
    o_ref[...] = r_ref[...] + y * lax.rsqrt(ms + RMS_EPS) * g_ref[...]


def _out_proj(a2d, w_bf, resid2d, gain, *, tm=256):
    m, k = a2d.shape
    n = w_bf.shape[1]
    return pl.pallas_call(
        _op_kernel,
        grid=(m // tm,),
        in_specs=[
            pl.BlockSpec((tm, k), lambda i: (i, 0)),
            pl.BlockSpec((k, n), lambda i: (0, 0)),
            pl.BlockSpec((tm, n), lambda i: (i, 0)),
            pl.BlockSpec((1, n), lambda i: (0, 0)),
        ],
        out_specs=pl.BlockSpec((tm, n), lambda i: (i, 0)),
        out_shape=jax.ShapeDtypeStruct((m, n), F32),
        compiler_params=pltpu.CompilerParams(dimension_semantics=("parallel",),
                                             vmem_limit_bytes=VMEM_LIMIT),
        name="out_proj",
    )(a2d, w_bf, resid2d, gain.reshape(1, n))


def _dot_nt(a, b):
    return lax.dot_general(a, b, (((1,), (1,)), ((), ())), preferred_element_type=F32)


def _scaled(q):
    return (q.astype(F32) * ATT_SCALE).astype(BF16)


def _flash(q, k_ref, v_ref, lo, hi, bias_fn):
    def body(jt, carry):
        m, l, acc = carry
        c0 = pl.multiple_of(jt * TILE, TILE)
        s = _dot_nt(q, k_ref[pl.ds(c0, TILE), :]) + bias_fn(jt)
        m_new = jnp.maximum(m, jnp.max(s, axis=-1, keepdims=True))
        alpha = jnp.exp(m - m_new)
        p = jnp.exp(s - m_new)
        l = alpha * l + jnp.sum(p, axis=-1, keepdims=True)
        acc = alpha * acc + jnp.dot(p.astype(BF16), v_ref[pl.ds(c0, TILE), :], preferred_element_type=F32)
        return m_new, l, acc

    init = (jnp.full((TILE, 1), NEG, F32), jnp.zeros((TILE, 1), F32), jnp.zeros((TILE, HEAD_DIM), F32))
    _, l, acc = lax.fori_loop(lo, hi, body, init)
    return acc / l


def _silu(z):
    return z * jax.nn.sigmoid(z)


def _attn_a_kernel(q_ref, k_ref, v_ref, z_ref, r_ref, o_ref):
    def qblock(n, carry):
        r0 = pl.multiple_of(n * TILE, TILE)
        q = _scaled(q_ref[pl.ds(r0, TILE), :])
        o = _flash(q, k_ref, v_ref, 0, n + 1, lambda jt: r_ref[n - jt])
        z = z_ref[pl.ds(r0, TILE), :].astype(F32)
        o_ref[pl.ds(r0, TILE), :] = (o * _silu(z)).astype(BF16)
        return carry

    lax.fori_loop(0, N_TILES, qblock, 0)


def _attn_a(qkvz, table):
    batch = qkvz.shape[0]

    def slot(s):
        return pl.BlockSpec((None, None, SEQ, HEAD_DIM), lambda h, b: (b, s * N_HEADS + h, 0, 0))

    return pl.pallas_call(
        _attn_a_kernel,
        grid=(N_HEADS, batch),
        in_specs=[slot(0), slot(1), slot(2), slot(3),
                  pl.BlockSpec((None, N_TILES, TILE, TILE), lambda h, b: (h, TAB_A // N_TILES, 0, 0))],
        out_specs=pl.BlockSpec((None, SEQ, HEAD_DIM), lambda h, b: (b, 0, h)),
        out_shape=jax.ShapeDtypeStruct((batch, SEQ, ATT_WIDTH), BF16),
        compiler_params=pltpu.CompilerParams(dimension_semantics=("parallel", "arbitrary"),
                                             vmem_limit_bytes=VMEM_LIMIT),
        name="dilated_attn",
    )(qkvz, qkvz, qkvz, qkvz, table)


def _cmp_kernel(c_ref, pos_ref, w1_ref, w2_ref, o_ref):
    half = CMP_STRIDE * HEAD_DIM
    c = c_ref[...].astype(F32)
    x_lo = (c + pos_ref[0]).astype(BF16)
    x_hi = (c + pos_ref[1]).astype(BF16)
    a = jnp.dot(x_lo, w1_ref[0:half, :], preferred_element_type=F32)
    bm = jnp.dot(x_hi, w1_ref[half:2 * half, :], preferred_element_type=F32)
    hid = jax.nn.gelu(a + pltpu.roll(bm, N_CMP - 1, 0))
    o_ref[...] = jnp.dot(hid.astype(BF16), w2_ref[...], preferred_element_type=F32).astype(BF16)


def _compress(kv, pos, w1_bf, w2_bf):
    batch = kv.shape[0]
    half = CMP_STRIDE * HEAD_DIM
    chunks = kv.reshape(batch, kv.shape[1], N_CMP, half)
    return pl.pallas_call(
        _cmp_kernel,
        grid=(2, batch, NSA_KV_GROUPS),
        in_specs=[
            pl.BlockSpec((None, None, N_CMP, half), lambda t, b, g: (b, t * NSA_KV_GROUPS + g, 0, 0)),
            pl.BlockSpec((None, 2, 1, half), lambda t, b, g: (t, 0, 0, 0)),
            pl.BlockSpec((None, 2 * half, CMP_HIDDEN), lambda t, b, g: (t, 0, 0)),
            pl.BlockSpec((None, CMP_HIDDEN, HEAD_DIM), lambda t, b, g: (t, 0, 0)),
        ],
        out_specs=pl.BlockSpec((None, None, None, N_CMP, HEAD_DIM), lambda t, b, g: (t, b, g, 0, 0)),
        out_shape=jax.ShapeDtypeStruct((2, batch, NSA_KV_GROUPS, N_CMP, HEAD_DIM), BF16),
        compiler_params=pltpu.CompilerParams(dimension_semantics=("arbitrary", "arbitrary", "arbitrary"),
                                             vmem_limit_bytes=VMEM_LIMIT),
        name="compress_kv",
    )(chunks, pos, w1_bf, w2_bf)


def _nsa_kernel(q_ref, zc_ref, zs_ref, zw_ref, gl_ref, kc_ref, vc_ref, ks_ref, vs_ref, kw_ref, vw_ref,
                rs_ref, rw_ref, rc_ref, ovt_ref, e_ref, o_ref, mexp_ref):
    n = pl.program_id(2)
    heads = NSA_HEADS_PER_GROUP
    qs = [_scaled(q_ref[j]) for j in range(heads)]

    kc = kc_ref[...]
    vc = vc_ref[...]
    psum = jnp.zeros((TILE, N_CMP), F32)
    o_cmp = []
    for j in range(heads):
        bias = rc_ref[j]
        s = _dot_nt(qs[j], kc) + bias
        valid = bias > 0.5 * NEG
        m = jnp.max(s, axis=-1, keepdims=True)
        p = jnp.where(valid, jnp.exp(s - m), 0.0)
        p = p / jnp.maximum(jnp.sum(p, axis=-1, keepdims=True), 1e-30)
        psum = psum + p
        o_cmp.append(jnp.dot(p.astype(BF16), vc, preferred_element_type=F32))

    imp = lax.dot_general(ovt_ref[...], psum, (((1,), (1,)), ((), ())),
                          precision=lax.Precision.HIGHEST, preferred_element_type=F32)
    blk = lax.broadcasted_iota(jnp.int32, (N_SLC, TILE), 0)
    tq = n * TILE + lax.broadcasted_iota(jnp.int32, (N_SLC, TILE), 1)
    cur = lax.shift_right_logical(tq, int(math.log2(SLC_BLOCK)))
    forced = (blk == 0) | (blk == cur) | (blk == cur - 1)
    score = jnp.where(forced, FORCE_SCORE, jnp.where(blk > cur, -FORCE_SCORE, imp))
    cnt = jnp.zeros((N_SLC, TILE), jnp.int32)
    for jj in range(N_SLC):
        sj = score[jj:jj + 1, :]
        beats = (sj > score) | ((sj == score) & (blk > jj))
        cnt = cnt + beats.astype(jnp.int32)
    mask_t = jnp.where(cnt < SLC_TOP_N, 0.0, NEG)
    mask_t = jnp.concatenate([mask_t, jnp.zeros((TILE - N_SLC, TILE), F32)], axis=0)
    mask = mask_t.T.astype(BF16)

    def expand(jt, carry):
        mexp_ref[jt] = jnp.dot(mask, e_ref[jt], preferred_element_type=F32)
        return carry

    lax.fori_loop(0, n + 1, expand, 0)

    gates = jax.nn.sigmoid(gl_ref[...])
    for j in range(heads):
        o_slc = _flash(qs[j], ks_ref, vs_ref, 0, n + 1,
                       lambda jt, j=j: rs_ref[j, n - jt] + mexp_ref[jt])
        o_win = _flash(qs[j], kw_ref, vw_ref, jnp.maximum(n - (WIN_TILES - 1), 0), n + 1,
                       lambda jt, j=j: rw_ref[j, n - jt])
        y = (gates[:, j:j + 1] * o_cmp[j] * _silu(zc_ref[j].astype(F32))
             + gates[:, heads + j:heads + j + 1] * o_slc * _silu(zs_ref[j].astype(F32))
             + gates[:, 2 * heads + j:2 * heads + j + 1] * o_win * _silu(zw_ref[j].astype(F32)))
        o_ref[:, j * HEAD_DIM:(j + 1) * HEAD_DIM] = y.astype(BF16)


def _nsa_attn(qz, gate_logits, kv, cmp_kv, table):
    batch = qz.shape[0]
    heads = NSA_HEADS_PER_GROUP
    groups = NSA_KV_GROUPS
    _, _, ov_t, e = _static_maps()

    def qz_spec(slot_group):
        return pl.BlockSpec((None, heads, TILE, HEAD_DIM),
                            lambda g, b, n: (b, slot_group * groups + g, n, 0))

    def kv_spec(branch):
        return pl.BlockSpec((None, None, SEQ, HEAD_DIM), lambda g, b, n: (b, branch * groups + g, 0, 0))

    def cmp_spec(t):
        return pl.BlockSpec((None, None, None, N_CMP, HEAD_DIM), lambda g, b, n: (t, b, g, 0, 0))

    return pl.pallas_call(
        _nsa_kernel,
        grid=(groups, batch, N_TILES),
        in_specs=[
            qz_spec(0), qz_spec(1), qz_spec(2), qz_spec(3),
            pl.BlockSpec((None, TILE, TILE), lambda g, b, n: (b, n, g)),
            cmp_spec(0), cmp_spec(1), kv_spec(2), kv_spec(3), kv_spec(4), kv_spec(5),
            pl.BlockSpec((heads, N_TILES, TILE, TILE), lambda g, b, n: (g, TAB_S // N_TILES, 0, 0)),
            pl.BlockSpec((heads, 8, TILE, TILE), lambda g, b, n: (g, TAB_W // 8, 0, 0)),
            pl.BlockSpec((heads, None, TILE, TILE), lambda g, b, n: (g, TAB_C + n, 0, 0)),
            pl.BlockSpec((N_SLC, N_CMP), lambda g, b, n: (0, 0)),
            pl.BlockSpec((N_TILES, TILE, TILE), lambda g, b, n: (0, 0, 0)),
        ],
        out_specs=pl.BlockSpec((None, TILE, heads * HEAD_DIM), lambda g, b, n: (b, n, g)),
        out_shape=jax.ShapeDtypeStruct((batch, SEQ, ATT_WIDTH), BF16),
        scratch_shapes=[pltpu.VMEM((N_TILES, TILE, TILE), F32)],
        compiler_params=pltpu.CompilerParams(dimension_semantics=("parallel", "parallel", "arbitrary"),
                                             vmem_limit_bytes=VMEM_LIMIT),
        name="nsa_attn",
    )(qz, qz, qz, qz, gate_logits, cmp_kv, cmp_kv, kv, kv, kv, kv, table, table, table,
      jnp.asarray(ov_t), jnp.asarray(e, dtype=BF16))


def kernel(x, norm_pre, norm_post, rel_table, w_in_a, w_out_a, kv_norm, w_kv, cmp_pos_k, cmp_pos_v,
           cmp_w1_k, cmp_w2_k, cmp_w1_v, cmp_w2_v, w_in_b, w_out_b):
    batch = x.shape[0]
    m = batch * SEQ
    heads = NSA_HEADS_PER_GROUP
    groups = NSA_KV_GROUPS
    x2d = x.reshape(m, D_MODEL)
    table = _bias_tables(rel_table)

    qkvz = _norm_matmul(x2d, norm_pre[0], w_in_a[0].astype(BF16))
    att_a = _attn_a(qkvz, table)
    h1 = _out_proj(att_a.reshape(m, ATT_WIDTH), w_out_a[0].astype(BF16), x2d, norm_post[0])

    kv = _norm_matmul(h1, kv_norm, w_kv.astype(BF16))
    half = CMP_STRIDE * HEAD_DIM
    pos = jnp.stack([cmp_pos_k, cmp_pos_v]).reshape(2, 2, 1, half)
    cmp_kv = _compress(kv, pos, jnp.stack([cmp_w1_k, cmp_w1_v]).astype(BF16),
                       jnp.stack([cmp_w2_k, cmp_w2_v]).astype(BF16))

    w_b = w_in_b[0]
    n_main = (1 + NSA_BRANCHES) * ATT_WIDTH
    wg = w_b[:, n_main:].reshape(D_MODEL, NSA_BRANCHES, groups, heads).transpose(0, 2, 1, 3)
    wg = jnp.pad(wg.reshape(D_MODEL, groups, NSA_BRANCHES * heads),
                 ((0, 0), (0, 0), (0, TILE - NSA_BRANCHES * heads))).reshape(D_MODEL, groups * TILE)
    qz, gate_logits = _norm_matmul(h1, norm_pre[1], w_b[:, :n_main].astype(BF16), wg.astype(BF16))
    att_b = _nsa_attn(qz, gate_logits.reshape(batch, SEQ, groups * TILE), kv, cmp_kv, table)
    out = _out_proj(att_b.reshape(m, ATT_WIDTH), w_out_b[0].astype(BF16), h1, norm_post[1])
    return out.reshape(batch, SEQ, D_MODEL)
```

```python
import functools
import math

import numpy as np
import jax
import jax.numpy as jnp
from jax import lax
from jax.experimental import pallas as pl
from jax.experimental.pallas import tpu as pltpu

F32 = jnp.float32
BF16 = jnp.bfloat16

D_MODEL = 2048
SEQ = 2048
N_HEADS = 16
HEAD_DIM = 128
ATT_WIDTH = N_HEADS * HEAD_DIM
ATT_SCALE = HEAD_DIM ** -0.5
DIL_PATTERNS = ((128, 1), (512, 4), (2048, 16))
REL_BUCKETS = 32
REL_MAX_EXACT = 16
REL_MAX_DISTANCE = 2048
NSA_KV_GROUPS = 4
NSA_HEADS_PER_GROUP = 4
NSA_BRANCHES = 3
CMP_BLOCK = 32
CMP_STRIDE = 16
CMP_HIDDEN = 256
SLC_BLOCK = 64
SLC_TOP_N = 16
WIN_SIZE = 512
RMS_EPS = 1e-6
NEG = -1e30
FORCE_SCORE = 1e9

TILE = 256
N_TILES = SEQ // TILE
N_CMP = SEQ // CMP_STRIDE
N_SLC = SEQ // SLC_BLOCK
WIN_TILES = -(-(WIN_SIZE - 1) // TILE) + 1
HEADS_PER_STEP = 4

TAB_A, TAB_S, TAB_W = 0, N_TILES, 2 * N_TILES
TAB_W_ROWS = 4
TAB_ROWS = 2 * N_TILES + TAB_W_ROWS
VMEM_LIMIT = 52 * 1024 * 1024


def _np_bucket(dist):
    n = np.maximum(dist, 0)
    nf = np.maximum(n, 1).astype(np.float32)
    log_b = REL_MAX_EXACT + (
        np.log(nf / np.float32(REL_MAX_EXACT)) / np.float32(math.log(REL_MAX_DISTANCE / REL_MAX_EXACT))
        * np.float32(REL_BUCKETS - REL_MAX_EXACT)).astype(np.int32)
    return np.where(n < REL_MAX_EXACT, n, np.minimum(log_b, REL_BUCKETS - 1)).astype(np.int32)


@functools.lru_cache(maxsize=None)
def _static_maps():
    ki = np.arange(TILE)[:, None]
    qi = np.arange(TILE)[None, :]
    dist = TILE * np.arange(N_TILES)[:, None, None] + qi[None] - ki[None]
    bk_t = _np_bucket(dist)
    mult = np.zeros(dist.shape, np.int64)
    for window, dil in DIL_PATTERNS:
        mult += ((dist % dil == 0) & (dist <= window)).astype(np.int64)
    base_a = np.where((dist >= 0) & (mult > 0), np.log(np.maximum(mult, 1)), NEG)
    base_s = np.where(dist >= 0, 0.0, NEG)
    base_w = np.full((TAB_W_ROWS, TILE, TILE), NEG)
    base_w[:WIN_TILES] = np.where((dist >= 0) & (dist < WIN_SIZE), 0.0, NEG)[:WIN_TILES]
    base_t = np.concatenate([base_a, base_s, base_w]).astype(np.float32)
    ci = np.arange(N_CMP)[:, None]
    t = TILE * np.arange(N_TILES)[:, None, None] + qi[None]
    dist_c = t - (CMP_STRIDE * ci[None] + CMP_BLOCK - 1)
    bk_c = _np_bucket(dist_c)
    base_c = np.where((dist_c >= 0) & (ci[None] < N_CMP - 1), 0.0, NEG).astype(np.float32)
    cs = np.arange(N_CMP)[None, :] * CMP_STRIDE
    sj = np.arange(N_SLC)[:, None] * SLC_BLOCK
    ov_t = ((cs < sj + SLC_BLOCK) & (cs + CMP_BLOCK > sj) & (np.arange(N_CMP)[None, :] < N_CMP - 1))
    onehot = np.zeros((SEQ, HEAD_DIM), np.float32)
    onehot[np.arange(SEQ), np.arange(SEQ) // SLC_BLOCK] = 1.0
    return bk_t.astype(np.int32), base_t, bk_c.astype(np.int32), base_c, ov_t.astype(np.float32), onehot


def _bias_kernel(tab_ref, bkt_ref, baset_ref, bkc_ref, basec_ref, out_ref, outc_ref, *, present_t, present_c):
    h = pl.program_id(0)
    tv = [tab_ref[b, h] for b in range(REL_BUCKETS)]

    def lookup(bk, present):
        val = jnp.full(bk.shape, tv[present[0]], F32)
        for b in present[1:]:
            val = jnp.where(bk == b, tv[b], val)
        return val

    for d in range(N_TILES):
        g = lookup(bkt_ref[d], present_t[d])
        out_ref[TAB_A + d] = g + baset_ref[TAB_A + d]
        out_ref[TAB_S + d] = g + baset_ref[TAB_S + d]
        if d < TAB_W_ROWS:
            out_ref[TAB_W + d] = g + baset_ref[TAB_W + d]
        outc_ref[d] = lookup(bkc_ref[d], present_c[d]) + basec_ref[d]


def _bias_tables(rel_table):
    bk_t, base_t, bk_c, base_c, _, _ = _static_maps()
    present_t = tuple(tuple(int(b) for b in np.unique(bk_t[d])) for d in range(N_TILES))
    present_c = tuple(tuple(int(b) for b in np.unique(bk_c[d])) for d in range(N_TILES))
    return pl.pallas_call(
        functools.partial(_bias_kernel, present_t=present_t, present_c=present_c),
        grid=(N_HEADS,),
        in_specs=[
            pl.BlockSpec(memory_space=pltpu.SMEM),
            pl.BlockSpec((N_TILES, TILE, TILE), lambda h: (0, 0, 0)),
            pl.BlockSpec((TAB_ROWS, TILE, TILE), lambda h: (0, 0, 0)),
            pl.BlockSpec((N_TILES, N_CMP, TILE), lambda h: (0, 0, 0)),
            pl.BlockSpec((N_TILES, N_CMP, TILE), lambda h: (0, 0, 0)),
        ],
        out_specs=[
            pl.BlockSpec((None, TAB_ROWS, TILE, TILE), lambda h: (h, 0, 0, 0)),
            pl.BlockSpec((None, N_TILES, N_CMP, TILE), lambda h: (h, 0, 0, 0)),
        ],
        out_shape=[
            jax.ShapeDtypeStruct((N_HEADS, TAB_ROWS, TILE, TILE), F32),
            jax.ShapeDtypeStruct((N_HEADS, N_TILES, N_CMP, TILE), F32),
        ],
        compiler_params=pltpu.CompilerParams(dimension_semantics=("arbitrary",),
                                             vmem_limit_bytes=VMEM_LIMIT),
        name="bias_tables",
    )(rel_table, jnp.asarray(bk_t), jnp.asarray(base_t), jnp.asarray(bk_c), jnp.asarray(base_c))


def _nm_kernel(*refs, n_chunks, gate):
    if gate:
        x_ref, g_ref, w_ref, wg_ref, o_ref, og_ref, xn_ref = refs
    else:
        x_ref, g_ref, w_ref, o_ref, xn_ref = refs

    @pl.when(pl.program_id(1) == 0)
    def _():
        x = x_ref[...]
        ms = jnp.mean(x * x, axis=-1, keepdims=True)
        xn_ref[...] = (x * lax.rsqrt(ms + RMS_EPS) * g_ref[...]).astype(BF16)
        if gate:
            og_ref[...] = jnp.dot(xn_ref[...], wg_ref[...], preferred_element_type=F32)

    res = jnp.dot(xn_ref[...], w_ref[...], preferred_element_type=F32)
    for c in range(n_chunks):
        o_ref[c] = res[:, c * HEAD_DIM:(c + 1) * HEAD_DIM].astype(BF16)


def _norm_matmul(x2d, gain, w_bf, wg_bf=None, *, tm=512, tn=1024):
    m, d = x2d.shape
    n = w_bf.shape[1]
    batch = m // SEQ
    spb = SEQ // tm
    gate = wg_bf is not None
    in_specs = [
        pl.BlockSpec((tm, d), lambda i, j: (i, 0)),
        pl.BlockSpec((1, d), lambda i, j: (0, 0)),
        pl.BlockSpec((d, tn), lambda i, j: (0, j)),
    ]
    out_shape = [jax.ShapeDtypeStruct((batch, n // HEAD_DIM, SEQ, HEAD_DIM), BF16)]
    out_specs = [pl.BlockSpec((None, tn // HEAD_DIM, tm, HEAD_DIM), lambda i, j: (i // spb, j, i % spb, 0))]
    args = [x2d, gain.reshape(1, d), w_bf]
    if gate:
        ng = wg_bf.shape[1]
        in_specs.append(pl.BlockSpec((d, ng), lambda i, j: (0, 0)))
        out_shape.append(jax.ShapeDtypeStruct((m, ng), F32))
        out_specs.append(pl.BlockSpec((tm, ng), lambda i, j: (i, 0)))
        args.append(wg_bf)
    res = pl.pallas_call(
        functools.partial(_nm_kernel, n_chunks=tn // HEAD_DIM, gate=gate),
        grid=(m // tm, n // tn),
        in_specs=in_specs,
        out_specs=out_specs,
        out_shape=out_shape,
        scratch_shapes=[pltpu.VMEM((tm, d), BF16)],
        compiler_params=pltpu.CompilerParams(dimension_semantics=("parallel", "arbitrary"),
                                             vmem_limit_bytes=VMEM_LIMIT),
        name="norm_matmul",
    )(*args)
    return res if gate else res[0]


def _op_kernel(a_ref, w_ref, r_ref, g_ref, o_ref):
    y = jnp.dot(a_ref[...], w_ref[...], preferred_element_type=F32)
    ms = jnp.mean(y * y, axis=-1, keepdims=True)
    o_ref[...] = r_ref[...] + y * lax.rsqrt(ms + RMS_EPS) * g_ref[...]


def _out_proj(a2d, w_bf, resid2d, gain, *, tm=256):
    m, k = a2d.shape
    n = w_bf.shape[1]
    return pl.pallas_call(
        _op_kernel,
        grid=(m // tm,),
        in_specs=[
            pl.BlockSpec((tm, k), lambda i: (i, 0)),
            pl.BlockSpec((k, n), lambda i: (0, 0)),
            pl.BlockSpec((tm, n), lambda i: (i, 0)),
            pl.BlockSpec((1, n), lambda i: (0, 0)),
        ],
        out_specs=pl.BlockSpec((tm, n), lambda i: (i, 0)),
        out_shape=jax.ShapeDtypeStruct((m, n), F32),
        compiler_params=pltpu.CompilerParams(dimension_semantics=("parallel",),
                                             vmem_limit_bytes=VMEM_LIMIT),
        name="out_proj",
    )(a2d, w_bf, resid2d, gain.reshape(1, n))


def _scaled_t(q):
    return (q.astype(F32) * ATT_SCALE).T.astype(BF16)


def _transpose_tiles(src_ref, dst_ref):
    for c in range(N_TILES):
        dst_ref[c] = src_ref[c * TILE:(c + 1) * TILE, :].astype(F32).T.astype(BF16)


def _silu(z):
    return z * jax.nn.sigmoid(z)


def _two_pass(lo, hi, heads, q_ref, k_fn, vt_fn, bias_fn, s_ref, acc_ref):
    def scores(j, ms):
        out = []
        for h in range(heads):
            s = jnp.dot(k_fn(h, j), q_ref[h], preferred_element_type=F32) + bias_fn(h, j)
            s_ref[h, j] = s
            out.append(jnp.maximum(ms[h], jnp.max(s, axis=0, keepdims=True)))
        return tuple(out)

    ms = lax.fori_loop(lo, hi, scores, tuple(jnp.full((1, TILE), NEG, F32) for _ in range(heads)))
    for h in range(heads):
        acc_ref[h] = jnp.zeros((HEAD_DIM, TILE), F32)

    def weighted(j, ls):
        out = []
        for h in range(heads):
            p = jnp.exp(s_ref[h, j] - ms[h])
            out.append(ls[h] + jnp.sum(p, axis=0, keepdims=True))
            acc_ref[h] += jnp.dot(vt_fn(h, j), p.astype(BF16), preferred_element_type=F32)
        return tuple(out)

    ls = lax.fori_loop(lo, hi, weighted, tuple(jnp.zeros((1, TILE), F32) for _ in range(heads)))
    return [acc_ref[h] / ls[h] for h in range(heads)]


def _attn_a_kernel(q_ref, k_ref, v_ref, z_ref, r_ref, o_ref, vt_ref, qt_ref, s_ref, acc_ref):
    n = pl.program_id(2)
    heads = HEADS_PER_STEP

    @pl.when(n == 0)
    def _():
        for h in range(heads):
            _transpose_tiles(v_ref.at[h], vt_ref.at[h])

    for h in range(heads):
        qt_ref[h] = _scaled_t(q_ref[h])

    def k_fn(h, j):
        return k_ref[h, pl.ds(pl.multiple_of(j * TILE, TILE), TILE), :]

    outs = _two_pass(0, n + 1, heads, qt_ref, k_fn, lambda h, j: vt_ref[h, j],
                     lambda h, j: r_ref[h, n - j], s_ref, acc_ref)
    for h in range(heads):
        y = outs[h].T * _silu(z_ref[h].astype(F32))
        o_ref[:, h * HEAD_DIM:(h + 1) * HEAD_DIM] = y.astype(BF16)


def _attn_a(qkvz, table):
    batch = qkvz.shape[0]
    heads = HEADS_PER_STEP
    hgroups = N_HEADS // heads

    def tile_slot(s):
        return pl.BlockSpec((None, heads, TILE, HEAD_DIM), lambda hg, b, n: (b, s * hgroups + hg, n, 0))

    def full_slot(s):
        return pl.BlockSpec((None, heads, SEQ, HEAD_DIM), lambda hg, b, n: (b, s * hgroups + hg, 0, 0))

    return pl.pallas_call(
        _attn_a_kernel,
        grid=(hgroups, batch, N_TILES),
        in_specs=[tile_slot(0), full_slot(1), full_slot(2), tile_slot(3),
                  pl.BlockSpec((heads, N_TILES, TILE, TILE), lambda hg, b, n: (hg, TAB_A // N_TILES, 0, 0))],
        out_specs=pl.BlockSpec((None, TILE, heads * HEAD_DIM), lambda hg, b, n: (b, n, hg)),
        out_shape=jax.ShapeDtypeStruct((batch, SEQ, ATT_WIDTH), BF16),
        scratch_shapes=[
            pltpu.VMEM((heads, N_TILES, HEAD_DIM, TILE), BF16),
            pltpu.VMEM((heads, HEAD_DIM, TILE), BF16),
            pltpu.VMEM((heads, N_TILES, TILE, TILE), F32),
            pltpu.VMEM((heads, HEAD_DIM, TILE), F32),
        ],
        compiler_params=pltpu.CompilerParams(dimension_semantics=("parallel", "parallel", "arbitrary"),
                                             vmem_limit_bytes=VMEM_LIMIT),
        name="dilated_attn",
    )(qkvz, qkvz, qkvz, qkvz, table)


def _cmp_kernel(c_ref, pos_ref, w1_ref, w2_ref, o_ref):
    half = CMP_STRIDE * HEAD_DIM
    c = c_ref[...].astype(F32)
    x_lo = (c + pos_ref[0]).astype(BF16)
    x_hi = (c + pos_ref[1]).astype(BF16)
    a = jnp.dot(x_lo, w1_ref[0:half, :], preferred_element_type=F32)
    bm = jnp.dot(x_hi, w1_ref[half:2 * half, :], preferred_element_type=F32)
    hid = jax.nn.gelu(a + pltpu.roll(bm, N_CMP - 1, 0))
    res = jnp.dot(hid.astype(BF16), w2_ref[...], preferred_element_type=F32)
    o_ref[...] = jnp.where(pl.program_id(0) == 1, res.T, res).astype(BF16)


def _compress(kv, pos, w1_bf, w2_bf):
    batch = kv.shape[0]
    half = CMP_STRIDE * HEAD_DIM
    chunks = kv.reshape(batch, kv.shape[1], N_CMP, half)
    return pl.pallas_call(
        _cmp_kernel,
        grid=(2, batch, NSA_KV_GROUPS),
        in_specs=[
            pl.BlockSpec((None, None, N_CMP, half), lambda t, b, g: (b, t * NSA_KV_GROUPS + g, 0, 0)),
            pl.BlockSpec((None, 2, 1, half), lambda t, b, g: (t, 0, 0, 0)),
            pl.BlockSpec((None, 2 * half, CMP_HIDDEN), lambda t, b, g: (t, 0, 0)),
            pl.BlockSpec((None, CMP_HIDDEN, HEAD_DIM), lambda t, b, g: (t, 0, 0)),
        ],
        out_specs=pl.BlockSpec((None, None, None, N_CMP, HEAD_DIM), lambda t, b, g: (t, b, g, 0, 0)),
        out_shape=jax.ShapeDtypeStruct((2, batch, NSA_KV_GROUPS, N_CMP, HEAD_DIM), BF16),
        compiler_params=pltpu.CompilerParams(dimension_semantics=("arbitrary", "arbitrary", "arbitrary"),
                                             vmem_limit_bytes=VMEM_LIMIT),
        name="compress_kv",
    )(chunks, pos, w1_bf, w2_bf)


def _nsa_kernel(q_ref, zc_ref, zs_ref, zw_ref, gl_ref, kc_ref, vct_ref, ks_ref, vs_ref, kw_ref, vw_ref,
                rs_ref, rw_ref, rc_ref, ovt_ref, onehot_ref, o_ref,
                kaug_ref, vst_ref, vwt_ref, qa_ref, qw_ref, s_ref, acc_ref):
    n = pl.program_id(2)
    heads = NSA_HEADS_PER_GROUP

    @pl.when(n == 0)
    def _():
        kaug_ref[:, 0:HEAD_DIM] = ks_ref[...]
        kaug_ref[:, HEAD_DIM:2 * HEAD_DIM] = onehot_ref[...]
        _transpose_tiles(vs_ref, vst_ref)
        _transpose_tiles(vw_ref, vwt_ref)

    kc = kc_ref[...]
    vct = vct_ref[...]
    psum = jnp.zeros((N_CMP, TILE), F32)
    o_cmp = []
    for h in range(heads):
        qt = _scaled_t(q_ref[h])
        qw_ref[h] = qt
        qa_ref[h, 0:HEAD_DIM, :] = qt
        bias = rc_ref[h]
        s = jnp.dot(kc, qt, preferred_element_type=F32) + bias
        m = jnp.max(s, axis=0, keepdims=True)
        p = jnp.where(bias > 0.5 * NEG, jnp.exp(s - m), 0.0)
        p = p / jnp.maximum(jnp.sum(p, axis=0, keepdims=True), 1e-30)
        psum = psum + p
        o_cmp.append(jnp.dot(vct, p.astype(BF16), preferred_element_type=F32))

    imp = jnp.dot(ovt_ref[...], psum, precision=lax.Precision.HIGHEST, preferred_element_type=F32)
    blk = lax.broadcasted_iota(jnp.int32, (N_SLC, TILE), 0)
    tq = n * TILE + lax.broadcasted_iota(jnp.int32, (N_SLC, TILE), 1)
    cur = lax.shift_right_logical(tq, int(math.log2(SLC_BLOCK)))
    forced = (blk == 0) | (blk == cur) | (blk == cur - 1)
    score = jnp.where(forced, FORCE_SCORE, jnp.where(blk > cur, -FORCE_SCORE, imp))
    cnt = jnp.zeros((N_SLC, TILE), jnp.int32)
    for jj in range(N_SLC):
        sj = score[jj:jj + 1, :]
        beats = (sj > score) | ((sj == score) & (blk > jj))
        cnt = cnt + beats.astype(jnp.int32)
    mask = jnp.where(cnt < SLC_TOP_N, 0.0, NEG)
    mask = jnp.concatenate([mask, jnp.zeros((HEAD_DIM - N_SLC, TILE), F32)], axis=0).astype(BF16)
    for h in range(heads):
        qa_ref[h, HEAD_DIM:2 * HEAD_DIM, :] = mask

    def ks_fn(h, j):
        return kaug_ref[pl.ds(pl.multiple_of(j * TILE, TILE), TILE), :]

    def kw_fn(h, j):
        return kw_ref[pl.ds(pl.multiple_of(j * TILE, TILE), TILE), :]

    o_slc = _two_pass(0, n + 1, heads, qa_ref, ks_fn, lambda h, j: vst_ref[j],
                      lambda h, j: rs_ref[h, n - j], s_ref, acc_ref)
    o_slc = [o.T for o in o_slc]
    o_win = _two_pass(jnp.maximum(n - (WIN_TILES - 1), 0), n + 1, heads, qw_ref, kw_fn,
                      lambda h, j: vwt_ref[j], lambda h, j: rw_ref[h, n - j], s_ref, acc_ref)

    gates = jax.nn.sigmoid(gl_ref[...])
    for h in range(heads):
        y = (gates[:, h:h + 1] * o_cmp[h].T * _silu(zc_ref[h].astype(F32))
             + gates[:, heads + h:heads + h + 1] * o_slc[h] * _silu(zs_ref[h].astype(F32))
             + gates[:, 2 * heads + h:2 * heads + h + 1] * o_win[h].T * _silu(zw_ref[h].astype(F32)))
        o_ref[:, h * HEAD_DIM:(h + 1) * HEAD_DIM] = y.astype(BF16)


def _nsa_attn(qz, gate_logits, kv, cmp_kv, table, table_c):
    batch = qz.shape[0]
    heads = NSA_HEADS_PER_GROUP
    groups = NSA_KV_GROUPS
    _, _, _, _, ov_t, onehot = _static_maps()

    def qz_spec(slot_group):
        return pl.BlockSpec((None, heads, TILE, HEAD_DIM),
                            lambda g, b, n: (b, slot_group * groups + g, n, 0))

    def kv_spec(branch):
        return pl.BlockSpec((None, None, SEQ, HEAD_DIM), lambda g, b, n: (b, branch * groups + g, 0, 0))

    def cmp_spec(t):
        return pl.BlockSpec((None, None, None, N_CMP, HEAD_DIM), lambda g, b, n: (t, b, g, 0, 0))

    return pl.pallas_call(
        _nsa_kernel,
        grid=(groups, batch, N_TILES),
        in_specs=[
            qz_spec(0), qz_spec(1), qz_spec(2), qz_spec(3),
            pl.BlockSpec((None, TILE, HEAD_DIM), lambda g, b, n: (b, n, g)),
            cmp_spec(0), cmp_spec(1), kv_spec(2), kv_spec(3), kv_spec(4), kv_spec(5),
            pl.BlockSpec((heads, N_TILES, TILE, TILE), lambda g, b, n: (g, TAB_S // N_TILES, 0, 0)),
            pl.BlockSpec((heads, TAB_W_ROWS, TILE, TILE), lambda g, b, n: (g, TAB_W // TAB_W_ROWS, 0, 0)),
            pl.BlockSpec((heads, None, N_CMP, TILE), lambda g, b, n: (g, n, 0, 0)),
            pl.BlockSpec((N_SLC, N_CMP), lambda g, b, n: (0, 0)),
            pl.BlockSpec((SEQ, HEAD_DIM), lambda g, b, n: (0, 0)),
        ],
        out_specs=pl.BlockSpec((None, TILE, heads * HEAD_DIM), lambda g, b, n: (b, n, g)),
        out_shape=jax.ShapeDtypeStruct((batch, SEQ, ATT_WIDTH), BF16),
        scratch_shapes=[
            pltpu.VMEM((SEQ, 2 * HEAD_DIM), BF16),
            pltpu.VMEM((N_TILES, HEAD_DIM, TILE), BF16),
            pltpu.VMEM((N_TILES, HEAD_DIM, TILE), BF16),
            pltpu.VMEM((heads, 2 * HEAD_DIM, TILE), BF16),
            pltpu.VMEM((heads, HEAD_DIM, TILE), BF16),
            pltpu.VMEM((heads, N_TILES, TILE, TILE), F32),
            pltpu.VMEM((heads, HEAD_DIM, TILE), F32),
        ],
        compiler_params=pltpu.CompilerParams(dimension_semantics=("parallel", "parallel", "arbitrary"),
                                             vmem_limit_bytes=VMEM_LIMIT),
        name="nsa_attn",
    )(qz, qz, qz, qz, gate_logits, cmp_kv, cmp_kv, kv, kv, kv, kv, table, table, table_c,
      jnp.asarray(ov_t), jnp.asarray(onehot, dtype=BF16))


def kernel(x, norm_pre, norm_post, rel_table, w_in_a, w_out_a, kv_norm, w_kv, cmp_pos_k, cmp_pos_v,
           cmp_w1_k, cmp_w2_k, cmp_w1_v, cmp_w2_v, w_in_b, w_out_b):
    batch = x.shape[0]
    m = batch * SEQ
    heads = NSA_HEADS_PER_GROUP
    groups = NSA_KV_GROUPS
    x2d = x.reshape(m, D_MODEL)
    table, table_c = _bias_tables(rel_table)

    qkvz = _norm_matmul(x2d, norm_pre[0], w_in_a[0].astype(BF16))
    att_a = _attn_a(qkvz, table)
    h1 = _out_proj(att_a.reshape(m, ATT_WIDTH), w_out_a[0].astype(BF16), x2d, norm_post[0])

    kv = _norm_matmul(h1, kv_norm, w_kv.astype(BF16))
    half = CMP_STRIDE * HEAD_DIM
    pos = jnp.stack([cmp_pos_k, cmp_pos_v]).reshape(2, 2, 1, half)
    cmp_kv = _compress(kv, pos, jnp.stack([cmp_w1_k, cmp_w1_v]).astype(BF16),
                       jnp.stack([cmp_w2_k, cmp_w2_v]).astype(BF16))

    w_b = w_in_b[0]
    n_main = (1 + NSA_BRANCHES) * ATT_WIDTH
    wg = w_b[:, n_main:].reshape(D_MODEL, NSA_BRANCHES, groups, heads).transpose(0, 2, 1, 3)
    wg = jnp.pad(wg.reshape(D_MODEL, groups, NSA_BRANCHES * heads),
                 ((0, 0), (0, 0), (0, HEAD_DIM - NSA_BRANCHES * heads))).reshape(D_MODEL, groups * HEAD_DIM)
    qz, gate_logits = _norm_matmul(h1, norm_pre[1], w_b[:, :n_main].astype(BF16), wg.astype(BF16))
    att_b = _nsa_attn(qz, gate_logits.reshape(batch, SEQ, groups * HEAD_DIM), kv, cmp_kv, table, table_c)
    out = _out_proj(att_b.reshape(m, ATT_WIDTH), w_out_b[0].astype(BF16), h1, norm_post[1])
    return out.reshape(batch, SEQ, D_MODEL)
```

```python
import functools
import math

import numpy as np
import jax
import jax.numpy as jnp
from jax import lax
from jax.experimental import pallas as pl
from jax.experimental.pallas import tpu as pltpu

F32 = jnp.float32
BF16 = jnp.bfloat16

D_MODEL = 2048
SEQ = 2048
N_HEADS = 16
HEAD_DIM = 128
ATT_WIDTH = N_HEADS * HEAD_DIM
ATT_SCALE = HEAD_DIM ** -0.5
DIL_PATTERNS = ((128, 1), (512, 4), (2048, 16))
REL_BUCKETS = 32
REL_MAX_EXACT = 16
REL_MAX_DISTANCE = 2048
NSA_KV_GROUPS = 4
NSA_HEADS_PER_GROUP = 4
NSA_BRANCHES = 3
CMP_BLOCK = 32
CMP_STRIDE = 16
CMP_HIDDEN = 256
SLC_BLOCK = 64
SLC_TOP_N = 16
WIN_SIZE = 512
RMS_EPS = 1e-6
NEG = -1e30
FORCE_SCORE = 1e9
LOG2E = math.log2(math.e)

TILE = 256
N_TILES = SEQ // TILE
N_PAIRS = N_TILES // 2
CAUSAL_STEPS = N_TILES + 1
CAUSAL_UNROLL = 3
N_CMP = SEQ // CMP_STRIDE
CMP_CHUNK = CMP_STRIDE * HEAD_DIM
N_SLC = SEQ // SLC_BLOCK
WIN_TILES = -(-(WIN_SIZE - 1) // TILE) + 1
HEADS_PER_STEP = 4
VT_ROWS = HEAD_DIM + 16

TAB_A, TAB_S, TAB_W = 0, N_TILES, 2 * N_TILES
TAB_W_ROWS = 4
TAB_ROWS = 2 * N_TILES + TAB_W_ROWS
VMEM_LIMIT = 52 * 1024 * 1024


def _np_bucket(dist):
    n = np.maximum(dist, 0)
    nf = np.maximum(n, 1).astype(np.float32)
    log_b = REL_MAX_EXACT + (
        np.log(nf / np.float32(REL_MAX_EXACT)) / np.float32(math.log(REL_MAX_DISTANCE / REL_MAX_EXACT))
        * np.float32(REL_BUCKETS - REL_MAX_EXACT)).astype(np.int32)
    return np.where(n < REL_MAX_EXACT, n, np.minimum(log_b, REL_BUCKETS - 1)).astype(np.int32)


@functools.lru_cache(maxsize=None)
def _static_maps():
    ki = np.arange(TILE)[:, None]
    qi = np.arange(TILE)[None, :]
    dist = TILE * np.arange(N_TILES)[:, None, None] + qi[None] - ki[None]
    bk_t = _np_bucket(dist)
    mult = np.zeros(dist.shape, np.int64)
    for window, dil in DIL_PATTERNS:
        mult += ((dist % dil == 0) & (dist <= window)).astype(np.int64)
    base_a = np.where((dist >= 0) & (mult > 0), np.log2(np.maximum(mult, 1)), NEG)
    base_s = np.where(dist >= 0, 0.0, NEG)
    base_w = np.full((TAB_W_ROWS, TILE, TILE), NEG)
    base_w[:WIN_TILES] = np.where((dist >= 0) & (dist < WIN_SIZE), 0.0, NEG)[:WIN_TILES]
    base_t = np.concatenate([base_a, base_s, base_w]).astype(np.float32)
    ci = np.arange(N_CMP)[:, None]
    t = TILE * np.arange(N_TILES)[:, None, None] + qi[None]
    dist_c = t - (CMP_STRIDE * ci[None] + CMP_BLOCK - 1)
    bk_c = _np_bucket(dist_c)
    base_c = np.where((dist_c >= 0) & (ci[None] < N_CMP - 1), 0.0, NEG).astype(np.float32)
    cs = np.arange(N_CMP)[None, :] * CMP_STRIDE
    sj = np.arange(N_SLC)[:, None] * SLC_BLOCK
    ov_t = ((cs < sj + SLC_BLOCK) & (cs + CMP_BLOCK > sj) & (np.arange(N_CMP)[None, :] < N_CMP - 1))
    onehot = np.zeros((SEQ, HEAD_DIM), np.float32)
    onehot[np.arange(SEQ), np.arange(SEQ) // SLC_BLOCK] = 1.0
    return bk_t.astype(np.int32), base_t, bk_c.astype(np.int32), base_c, ov_t.astype(np.float32), onehot


def _bias_kernel(tab_ref, bkt_ref, baset_ref, bkc_ref, basec_ref, out_ref, outc_ref, *, present_t, present_c):
    h = pl.program_id(0)
    tv = [tab_ref[b, h] * LOG2E for b in range(REL_BUCKETS)]

    def lookup(bk, present):
        val = jnp.full(bk.shape, tv[present[0]], F32)
        for b in present[1:]:
            val = jnp.where(bk == b, tv[b], val)
        return val

    for d in range(N_TILES):
        g = lookup(bkt_ref[d], present_t[d])
        out_ref[TAB_A + d] = g + baset_ref[TAB_A + d]
        out_ref[TAB_S + d] = g + baset_ref[TAB_S + d]
        if d < TAB_W_ROWS:
            out_ref[TAB_W + d] = g + baset_ref[TAB_W + d]
        outc_ref[d] = lookup(bkc_ref[d], present_c[d]) + basec_ref[d]


def _bias_tables(rel_table):
    bk_t, base_t, bk_c, base_c, _, _ = _static_maps()
    present_t = tuple(tuple(int(b) for b in np.unique(bk_t[d])) for d in range(N_TILES))
    present_c = tuple(tuple(int(b) for b in np.unique(bk_c[d])) for d in range(N_TILES))
    return pl.pallas_call(
        functools.partial(_bias_kernel, present_t=present_t, present_c=present_c),
        grid=(N_HEADS,),
        in_specs=[
            pl.BlockSpec(memory_space=pltpu.SMEM),
            pl.BlockSpec((N_TILES, TILE, TILE), lambda h: (0, 0, 0)),
            pl.BlockSpec((TAB_ROWS, TILE, TILE), lambda h: (0, 0, 0)),
            pl.BlockSpec((N_TILES, N_CMP, TILE), lambda h: (0, 0, 0)),
            pl.BlockSpec((N_TILES, N_CMP, TILE), lambda h: (0, 0, 0)),
        ],
        out_specs=[
            pl.BlockSpec((None, TAB_ROWS, TILE, TILE), lambda h: (h, 0, 0, 0)),
            pl.BlockSpec((None, N_TILES, N_CMP, TILE), lambda h: (h, 0, 0, 0)),
        ],
        out_shape=[
            jax.ShapeDtypeStruct((N_HEADS, TAB_ROWS, TILE, TILE), F32),
            jax.ShapeDtypeStruct((N_HEADS, N_TILES, N_CMP, TILE), F32),
        ],
        compiler_params=pltpu.CompilerParams(dimension_semantics=("arbitrary",),
                                             vmem_limit_bytes=VMEM_LIMIT),
        name="bias_tables",
    )(rel_table, jnp.asarray(bk_t), jnp.asarray(base_t), jnp.asarray(bk_c), jnp.asarray(base_c))


def _silu(z):
    return z * jax.nn.sigmoid(z)


def _nm_kernel(*refs, n_chunks, extra, silu_from):
    if extra == "gate":
        x_ref, g_ref, w_ref, wg_ref, o_ref, og_ref, xn_ref = refs
    elif extra == "chunks":
        x_ref, g_ref, w_ref, o_ref, oc_ref, xn_ref, res_ref = refs
    else:
        x_ref, g_ref, w_ref, o_ref, xn_ref = refs
    j = pl.program_id(1)

    @pl.when(j == 0)
    def _():
        x = x_ref[...]
        ms = jnp.mean(x * x, axis=-1, keepdims=True)
        xn_ref[...] = (x * lax.rsqrt(ms + RMS_EPS) * g_ref[...]).astype(BF16)
        if extra == "gate":
            og_ref[...] = jnp.dot(xn_ref[...], wg_ref[...], preferred_element_type=F32)

    res = jnp.dot(xn_ref[...], w_ref[...], preferred_element_type=F32)

    def store(act):
        for c in range(n_chunks):
            o_ref[c] = act(res[:, c * HEAD_DIM:(c + 1) * HEAD_DIM]).astype(BF16)

    @pl.when(j < silu_from)
    def _():
        store(lambda t: t)

    @pl.when(j >= silu_from)
    def _():
        store(_silu)

    if extra == "chunks":
        @pl.when(j == 0)
        def _():
            rows = res_ref.shape[1] // CMP_STRIDE
            for c in range(n_chunks):
                res_ref[c] = res[:, c * HEAD_DIM:(c + 1) * HEAD_DIM]
                for i in range(CMP_STRIDE):
                    oc_ref[c, :, i * HEAD_DIM:(i + 1) * HEAD_DIM] = res_ref[
                        c, pl.ds(i, rows, stride=CMP_STRIDE), :].astype(BF16)


def _norm_matmul(x2d, gain, w_bf, wg_bf=None, *, n_cols, silu_from_col, chunks=False, tm=512, tn=1024):
    m, d = x2d.shape
    assert silu_from_col % tn == 0 and n_cols % tn == 0
    assert not (chunks and wg_bf is not None)
    batch = m // SEQ
    spb = SEQ // tm
    extra = "gate" if wg_bf is not None else ("chunks" if chunks else None)
    in_specs = [
        pl.BlockSpec((tm, d), lambda i, j: (i, 0)),
        pl.BlockSpec((1, d), lambda i, j: (0, 0)),
        pl.BlockSpec((d, tn), lambda i, j: (0, j)),
    ]
    out_shape = [jax.ShapeDtypeStruct((batch, n_cols // HEAD_DIM, SEQ, HEAD_DIM), BF16)]
    out_specs = [pl.BlockSpec((None, tn // HEAD_DIM, tm, HEAD_DIM), lambda i, j: (i // spb, j, i % spb, 0))]
    args = [x2d, gain.reshape(1, d), w_bf]
    scratch = [pltpu.VMEM((tm, d), BF16)]
    if extra == "gate":
        ng = wg_bf.shape[1]
        in_specs.append(pl.BlockSpec((d, ng), lambda i, j: (0, 0)))
        out_shape.append(jax.ShapeDtypeStruct((m, ng), F32))
        out_specs.append(pl.BlockSpec((tm, ng), lambda i, j: (i, 0)))
        args.append(wg_bf)
    elif extra == "chunks":
        out_shape.append(jax.ShapeDtypeStruct((batch, tn // HEAD_DIM, N_CMP, CMP_CHUNK), BF16))
        out_specs.append(pl.BlockSpec((None, tn // HEAD_DIM, tm // CMP_STRIDE, CMP_CHUNK),
                                      lambda i, j: (i // spb, 0, i % spb, 0)))
        scratch.append(pltpu.VMEM((tn // HEAD_DIM, tm, HEAD_DIM), F32))
    res = pl.pallas_call(
        functools.partial(_nm_kernel, n_chunks=tn // HEAD_DIM, extra=extra, silu_from=silu_from_col // tn),
        grid=(m // tm, n_cols // tn),
        in_specs=in_specs,
        out_specs=out_specs,
        out_shape=out_shape,
        scratch_shapes=scratch,
        compiler_params=pltpu.CompilerParams(dimension_semantics=("parallel", "arbitrary"),
                                             vmem_limit_bytes=VMEM_LIMIT),
        name="norm_matmul",
    )(*args)
    return res if extra else res[0]


def _op_kernel(a_ref, w_ref, r_ref, g_ref, o_ref):
    y = jnp.dot(a_ref[...], w_ref[...], preferred_element_type=F32)
    ms = jnp.mean(y * y, axis=-1, keepdims=True)
    o_ref[...] = r_ref[...] + y * lax.rsqrt(ms + RMS_EPS) * g_ref[...]


def _out_proj(a2d, w_bf, resid2d, gain, *, tm=256):
    m, k = a2d.shape
    n = w_bf.shape[1]
    return pl.pallas_call(
        _op_kernel,
        grid=(m // tm,),
        in_specs=[
            pl.BlockSpec((tm, k), lambda i: (i, 0)),
            pl.BlockSpec((k, n), lambda i: (0, 0)),
            pl.BlockSpec((tm, n), lambda i: (i, 0)),
            pl.BlockSpec((1, n), lambda i: (0, 0)),
        ],
        out_specs=pl.BlockSpec((tm, n), lambda i: (i, 0)),
        out_shape=jax.ShapeDtypeStruct((m, n), F32),
        compiler_params=pltpu.CompilerParams(dimension_semantics=("parallel",),
                                             vmem_limit_bytes=VMEM_LIMIT),
        name="out_proj",
    )(a2d, w_bf, resid2d, gain.reshape(1, n))


def _scaled_t(q):
    return (q.astype(F32) * (ATT_SCALE * LOG2E)).T.astype(BF16)


def _transpose_tiles(src_ref, dst_ref):
    for c in range(N_TILES):
        dst_ref[c, 0:HEAD_DIM, :] = src_ref[c * TILE:(c + 1) * TILE, :].astype(F32).T.astype(BF16)
        dst_ref[c, HEAD_DIM:VT_ROWS, :] = jnp.ones((VT_ROWS - HEAD_DIM, TILE), BF16)


def _pair_tiles(n):
    return (n, N_TILES - 1 - n)


def _tile_rows(idx):
    return pl.ds(pl.multiple_of(idx * TILE, TILE), TILE)


def _normalized(acc):
    return acc[0:HEAD_DIM, :] / acc[HEAD_DIM:HEAD_DIM + 1, :]


def _pair_causal(n, heads, q_ref, k_fn, vt_fn, bias_fn, s_ref, acc_ref):
    def step(t):
        first = t <= n
        slot = jnp.where(first, 0, 1)
        j = jnp.where(first, t, t - n - 1)
        delta = jnp.where(first, n - t, N_TILES - t)
        return first, slot, j, delta

    def scores(t, ms):
        first, slot, j, delta = step(t)
        m0, m1 = list(ms[0]), list(ms[1])
        for h in range(heads):
            s = jnp.dot(k_fn(h, j), q_ref[slot, h], preferred_element_type=F32) + bias_fn(h, delta)
            s_ref[h, t] = s
            cm = jnp.max(s, axis=0, keepdims=True)
            m0[h] = jnp.where(first, jnp.maximum(m0[h], cm), m0[h])
            m1[h] = jnp.where(first, m1[h], jnp.maximum(m1[h], cm))
        return tuple(m0), tuple(m1)

    neg = (jnp.full((1, TILE), NEG, F32),) * heads
    m0, m1 = lax.fori_loop(0, CAUSAL_STEPS, scores, (neg, neg), unroll=CAUSAL_UNROLL)

    for slot in range(2):
        for h in range(heads):
            acc_ref[slot, h] = jnp.zeros((VT_ROWS, TILE), F32)

    def weighted(t, carry):
        first, slot, j, _ = step(t)
        for h in range(heads):
            p = jnp.exp2(s_ref[h, t] - jnp.where(first, m0[h], m1[h]))
            acc_ref[slot, h] += jnp.dot(vt_fn(h, j), p.astype(BF16), preferred_element_type=F32)
        return carry

    lax.fori_loop(0, CAUSAL_STEPS, weighted, 0, unroll=CAUSAL_UNROLL)
    return [[_normalized(acc_ref[slot, h]) for h in range(heads)] for slot in range(2)]


def _attn_a_kernel(qa_ref, qb_ref, k_ref, v_ref, za_ref, zb_ref, r_ref, o_ref, vt_ref, qt_ref, s_ref, acc_ref):
    n = pl.program_id(2)
    heads = HEADS_PER_STEP

    @pl.when(n == 0)
    def _():
        for h in range(heads):
            _transpose_tiles(v_ref.at[h], vt_ref.at[h])

    for slot, q_ref in enumerate((qa_ref, qb_ref)):
        for h in range(heads):
            qt_ref[slot, h] = _scaled_t(q_ref[h])

    outs = _pair_causal(n, heads, qt_ref, lambda h, j: k_ref[h, _tile_rows(j), :], lambda h, j: vt_ref[h, j],
                        lambda h, delta: r_ref[h, delta], s_ref, acc_ref)
    for slot, (z_ref, idx) in enumerate(zip((za_ref, zb_ref), _pair_tiles(n))):
        for h in range(heads):
            y = outs[slot][h].T * z_ref[h].astype(F32)
            o_ref[_tile_rows(idx), h * HEAD_DIM:(h + 1) * HEAD_DIM] = y.astype(BF16)


def _resident(block_shape, index_map):
    return pl.BlockSpec(block_shape, index_map, pipeline_mode=pl.Buffered(1))


def _attn_a(qkvz, table):
    batch = qkvz.shape[0]
    heads = HEADS_PER_STEP
    hgroups = N_HEADS // heads

    def tile_slot(s, second):
        def index(hg, b, n):
            return (b, s * hgroups + hg, N_TILES - 1 - n if second else n, 0)
        return pl.BlockSpec((None, heads, TILE, HEAD_DIM), index)

    def full_slot(s):
        return pl.BlockSpec((None, heads, SEQ, HEAD_DIM), lambda hg, b, n: (b, s * hgroups + hg, 0, 0))

    return pl.pallas_call(
        _attn_a_kernel,
        grid=(hgroups, batch, N_PAIRS),
        in_specs=[tile_slot(0, False), tile_slot(0, True), full_slot(1), full_slot(2),
                  tile_slot(3, False), tile_slot(3, True),
                  _resident((heads, N_TILES, TILE, TILE), lambda hg, b, n: (hg, TAB_A // N_TILES, 0, 0))],
        out_specs=pl.BlockSpec((None, SEQ, heads * HEAD_DIM), lambda hg, b, n: (b, 0, hg)),
        out_shape=jax.ShapeDtypeStruct((batch, SEQ, ATT_WIDTH), BF16),
        scratch_shapes=[
            pltpu.VMEM((heads, N_TILES, VT_ROWS, TILE), BF16),
            pltpu.VMEM((2, heads, HEAD_DIM, TILE), BF16),
            pltpu.VMEM((heads, CAUSAL_STEPS, TILE, TILE), F32),
            pltpu.VMEM((2, heads, VT_ROWS, TILE), F32),
        ],
        compiler_params=pltpu.CompilerParams(dimension_semantics=("parallel", "parallel", "arbitrary"),
                                             vmem_limit_bytes=VMEM_LIMIT),
        name="dilated_attn",
    )(qkvz, qkvz, qkvz, qkvz, qkvz, qkvz, table)


def _cmp_kernel(c_ref, pos_ref, w1_ref, w2_ref, o_ref):
    c = c_ref[...].astype(F32)
    x_lo = (c + pos_ref[0]).astype(BF16)
    x_hi = (c + pos_ref[1]).astype(BF16)
    a = jnp.dot(x_lo, w1_ref[0:CMP_CHUNK, :], preferred_element_type=F32)
    bm = jnp.dot(x_hi, w1_ref[CMP_CHUNK:2 * CMP_CHUNK, :], preferred_element_type=F32)
    hid = jax.nn.gelu(a + pltpu.roll(bm, N_CMP - 1, 0))
    res = jnp.dot(hid.astype(BF16), w2_ref[...], preferred_element_type=F32)
    o_ref[...] = jnp.where(pl.program_id(0) == 1, res.T, res).astype(BF16)


def _compress(chunks, pos, w1_bf, w2_bf):
    batch = chunks.shape[0]
    return pl.pallas_call(
        _cmp_kernel,
        grid=(2, batch, NSA_KV_GROUPS),
        in_specs=[
            pl.BlockSpec((None, None, N_CMP, CMP_CHUNK), lambda t, b, g: (b, t * NSA_KV_GROUPS + g, 0, 0)),
            pl.BlockSpec((None, 2, 1, CMP_CHUNK), lambda t, b, g: (t, 0, 0, 0)),
            pl.BlockSpec((None, 2 * CMP_CHUNK, CMP_HIDDEN), lambda t, b, g: (t, 0, 0)),
            pl.BlockSpec((None, CMP_HIDDEN, HEAD_DIM), lambda t, b, g: (t, 0, 0)),
        ],
        out_specs=pl.BlockSpec((None, None, None, N_CMP, HEAD_DIM), lambda t, b, g: (t, b, g, 0, 0)),
        out_shape=jax.ShapeDtypeStruct((2, batch, NSA_KV_GROUPS, N_CMP, HEAD_DIM), BF16),
        compiler_params=pltpu.CompilerParams(dimension_semantics=("arbitrary", "arbitrary", "arbitrary"),
                                             vmem_limit_bytes=VMEM_LIMIT),
        name="compress_kv",
    )(chunks, pos, w1_bf, w2_bf)


def _nsa_kernel(qa_ref, qb_ref, zca_ref, zcb_ref, zsa_ref, zsb_ref, zwa_ref, zwb_ref, gla_ref, glb_ref,
                kc_ref, vct_ref, ks_ref, vs_ref, kw_ref, vw_ref, rs_ref, rw_ref, rca_ref, rcb_ref,
                ovt_ref, onehot_ref, o_ref,
                kaug_ref, vst_ref, vwt_ref, qa_sc, qw_sc, s_ref, acc_ref):
    n = pl.program_id(2)
    heads = NSA_HEADS_PER_GROUP
    tiles = _pair_tiles(n)

    @pl.when(n == 0)
    def _():
        kaug_ref[:, 0:HEAD_DIM] = ks_ref[...]
        kaug_ref[:, HEAD_DIM:2 * HEAD_DIM] = onehot_ref[...]
        _transpose_tiles(vs_ref, vst_ref)
        _transpose_tiles(vw_ref, vwt_ref)

    kc = kc_ref[...]
    vct = vct_ref[...]
    o_cmp = []
    for slot, (q_ref, rc_ref) in enumerate(((qa_ref, rca_ref), (qb_ref, rcb_ref))):
        psum = jnp.zeros((N_CMP, TILE), F32)
        outs = []
        for h in range(heads):
            qt = _scaled_t(q_ref[h])
            qw_sc[slot, h] = qt
            qa_sc[slot, h, 0:HEAD_DIM, :] = qt
            bias = rc_ref[h]
            s = jnp.dot(kc, qt, preferred_element_type=F32) + bias
            m = jnp.max(s, axis=0, keepdims=True)
            p = jnp.where(bias > 0.5 * NEG, jnp.exp2(s - m), 0.0)
            p = p / jnp.maximum(jnp.sum(p, axis=0, keepdims=True), 1e-30)
            psum = psum + p
            outs.append(jnp.dot(vct, p.astype(BF16), preferred_element_type=F32))
        o_cmp.append(outs)

        imp = jnp.dot(ovt_ref[...], psum, precision=lax.Precision.HIGHEST, preferred_element_type=F32)
        blk = lax.broadcasted_iota(jnp.int32, (N_SLC, TILE), 0)
        tq = tiles[slot] * TILE + lax.broadcasted_iota(jnp.int32, (N_SLC, TILE), 1)
        cur = lax.shift_right_logical(tq, int(math.log2(SLC_BLOCK)))
        forced = (blk == 0) | (blk == cur) | (blk == cur - 1)
        score = jnp.where(forced, FORCE_SCORE, jnp.where(blk > cur, -FORCE_SCORE, imp))
        cnt = jnp.zeros((N_SLC, TILE), jnp.int32)
        for jj in range(N_SLC):
            sj = score[jj:jj + 1, :]
            beats = (sj > score) | ((sj == score) & (blk > jj))
            cnt = cnt + beats.astype(jnp.int32)
        mask = jnp.where(cnt < SLC_TOP_N, 0.0, NEG)
        mask = jnp.concatenate([mask, jnp.zeros((HEAD_DIM - N_SLC, TILE), F32)], axis=0).astype(BF16)
        for h in range(heads):
            qa_sc[slot, h, HEAD_DIM:2 * HEAD_DIM, :] = mask

    o_slc = _pair_causal(n, heads, qa_sc, lambda h, j: kaug_ref[_tile_rows(j), :], lambda h, j: vst_ref[j],
                         lambda h, delta: rs_ref[h, delta], s_ref, acc_ref)

    o_win = []
    for slot in range(2):
        js = [tiles[slot] - d for d in range(WIN_TILES)]
        ms = [jnp.full((1, TILE), NEG, F32)] * heads
        for d in range(WIN_TILES):
            bias_idx = jnp.where(js[d] >= 0, d, TAB_W_ROWS - 1)
            rows = _tile_rows(jnp.maximum(js[d], 0))
            for h in range(heads):
                s = jnp.dot(kw_ref[rows, :], qw_sc[slot, h], preferred_element_type=F32) + rw_ref[h, bias_idx]
                s_ref[h, slot * WIN_TILES + d] = s
                ms[h] = jnp.maximum(ms[h], jnp.max(s, axis=0, keepdims=True))
        outs = []
        for h in range(heads):
            acc = None
            for d in range(WIN_TILES):
                p = jnp.exp2(s_ref[h, slot * WIN_TILES + d] - ms[h])
                part = jnp.dot(vwt_ref[jnp.maximum(js[d], 0)], p.astype(BF16), preferred_element_type=F32)
                acc = part if acc is None else acc + part
            outs.append(_normalized(acc))
        o_win.append(outs)

    for slot, (zc_ref, zs_ref, zw_ref, gl_ref) in enumerate(
            ((zca_ref, zsa_ref, zwa_ref, gla_ref), (zcb_ref, zsb_ref, zwb_ref, glb_ref))):
        gates = jax.nn.sigmoid(gl_ref[...])
        for h in range(heads):
            y = (gates[:, h:h + 1] * o_cmp[slot][h].T * zc_ref[h].astype(F32)
                 + gates[:, heads + h:heads + h + 1] * o_slc[slot][h].T * zs_ref[h].astype(F32)
                 + gates[:, 2 * heads + h:2 * heads + h + 1] * o_win[slot][h].T * zw_ref[h].astype(F32))
            o_ref[_tile_rows(tiles[slot]), h * HEAD_DIM:(h + 1) * HEAD_DIM] = y.astype(BF16)


def _nsa_attn(qz, gate_logits, kv, cmp_kv, table, table_c):
    batch = qz.shape[0]
    heads = NSA_HEADS_PER_GROUP
    groups = NSA_KV_GROUPS
    _, _, _, _, ov_t, onehot = _static_maps()

    def tile_of(n, second):
        return N_TILES - 1 - n if second else n

    def qz_specs(slot_group):
        return [pl.BlockSpec((None, heads, TILE, HEAD_DIM),
                             lambda g, b, n, second=second: (b, slot_group * groups + g, tile_of(n, second), 0))
                for second in (False, True)]

    def kv_spec(branch):
        return pl.BlockSpec((None, None, SEQ, HEAD_DIM), lambda g, b, n: (b, branch * groups + g, 0, 0))

    def cmp_spec(t):
        return pl.BlockSpec((None, None, None, N_CMP, HEAD_DIM), lambda g, b, n: (t, b, g, 0, 0))

    gl_specs = [pl.BlockSpec((None, TILE, HEAD_DIM), lambda g, b, n, second=second: (b, tile_of(n, second), g))
                for second in (False, True)]
    rc_specs = [pl.BlockSpec((heads, None, N_CMP, TILE), lambda g, b, n, second=second: (g, tile_of(n, second), 0, 0))
                for second in (False, True)]

    return pl.pallas_call(
        _nsa_kernel,
        grid=(groups, batch, N_PAIRS),
        in_specs=[
            *qz_specs(0), *qz_specs(1), *qz_specs(2), *qz_specs(3), *gl_specs,
            cmp_spec(0), cmp_spec(1), kv_spec(2), kv_spec(3), kv_spec(4), kv_spec(5),
            _resident((heads, N_TILES, TILE, TILE), lambda g, b, n: (g, TAB_S // N_TILES, 0, 0)),
            _resident((heads, TAB_W_ROWS, TILE, TILE), lambda g, b, n: (g, TAB_W // TAB_W_ROWS, 0, 0)),
            *rc_specs,
            _resident((N_SLC, N_CMP), lambda g, b, n: (0, 0)),
            _resident((SEQ, HEAD_DIM), lambda g, b, n: (0, 0)),
        ],
        out_specs=pl.BlockSpec((None, SEQ, heads * HEAD_DIM), lambda g, b, n: (b, 0, g)),
        out_shape=jax.ShapeDtypeStruct((batch, SEQ, ATT_WIDTH), BF16),
        scratch_shapes=[
            pltpu.VMEM((SEQ, 2 * HEAD_DIM), BF16),
            pltpu.VMEM((N_TILES, VT_ROWS, TILE), BF16),
            pltpu.VMEM((N_TILES, VT_ROWS, TILE), BF16),
            pltpu.VMEM((2, heads, 2 * HEAD_DIM, TILE), BF16),
            pltpu.VMEM((2, heads, HEAD_DIM, TILE), BF16),
            pltpu.VMEM((heads, CAUSAL_STEPS, TILE, TILE), F32),
            pltpu.VMEM((2, heads, VT_ROWS, TILE), F32),
        ],
        compiler_params=pltpu.CompilerParams(dimension_semantics=("parallel", "parallel", "arbitrary"),
                                             vmem_limit_bytes=VMEM_LIMIT),
        name="nsa_attn",
    )(qz, qz, qz, qz, qz, qz, qz, qz, gate_logits, gate_logits, cmp_kv, cmp_kv, kv, kv, kv, kv,
      table, table, table_c, table_c, jnp.asarray(ov_t), jnp.asarray(onehot, dtype=BF16))


def kernel(x, norm_pre, norm_post, rel_table, w_in_a, w_out_a, kv_norm, w_kv, cmp_pos_k, cmp_pos_v,
           cmp_w1_k, cmp_w2_k, cmp_w1_v, cmp_w2_v, w_in_b, w_out_b):
    batch = x.shape[0]
    m = batch * SEQ
    heads = NSA_HEADS_PER_GROUP
    groups = NSA_KV_GROUPS
    x2d = x.reshape(m, D_MODEL)
    table, table_c = _bias_tables(rel_table)

    qkvz = _norm_matmul(x2d, norm_pre[0], w_in_a[0].astype(BF16), n_cols=4 * ATT_WIDTH,
                        silu_from_col=3 * ATT_WIDTH)
    att_a = _attn_a(qkvz, table)
    h1 = _out_proj(att_a.reshape(m, ATT_WIDTH), w_out_a[0].astype(BF16), x2d, norm_post[0])

    n_kv = w_kv.shape[1]
    kv, cmp_chunks = _norm_matmul(h1, kv_norm, w_kv.astype(BF16), n_cols=n_kv, silu_from_col=n_kv, chunks=True)
    pos = jnp.stack([cmp_pos_k, cmp_pos_v]).reshape(2, 2, 1, CMP_CHUNK)
    cmp_kv = _compress(cmp_chunks, pos, jnp.stack([cmp_w1_k, cmp_w1_v]).astype(BF16),
                       jnp.stack([cmp_w2_k, cmp_w2_v]).astype(BF16))

    w_b = w_in_b[0]
    n_main = (1 + NSA_BRANCHES) * ATT_WIDTH
    wg = w_b[:, n_main:].reshape(D_MODEL, NSA_BRANCHES, groups, heads).transpose(0, 2, 1, 3)
    wg = jnp.pad(wg.reshape(D_MODEL, groups, NSA_BRANCHES * heads),
                 ((0, 0), (0, 0), (0, HEAD_DIM - NSA_BRANCHES * heads))).reshape(D_MODEL, groups * HEAD_DIM)
    qz, gate_logits = _norm_matmul(h1, norm_pre[1], w_b.astype(BF16), wg.astype(BF16), n_cols=n_main,
                                   silu_from_col=ATT_WIDTH)
    att_b = _nsa_attn(qz, gate_logits.reshape(batch, SEQ, groups * HEAD_DIM), kv, cmp_kv, table, table_c)
    out = _out_proj(att_b.reshape(m, ATT_WIDTH), w_out_b[0].astype(BF16), h1, norm_post[1])
    return out.reshape(batch, SEQ, D_MODEL)
```

```python
import functools
import math

import numpy as np
import jax
import jax.numpy as jnp
from jax import lax
from jax.experimental import pallas as pl
from jax.experimental.pallas import tpu as pltpu

F32 = jnp.float32
BF16 = jnp.bfloat16

D_MODEL = 2048
SEQ = 2048
N_HEADS = 16
HEAD_DIM = 128
ATT_WIDTH = N_HEADS * HEAD_DIM
ATT_SCALE = HEAD_DIM ** -0.5
DIL_PATTERNS = ((128, 1), (512, 4), (2048, 16))
REL_BUCKETS = 32
REL_MAX_EXACT = 16
REL_MAX_DISTANCE = 2048
NSA_KV_GROUPS = 4
NSA_HEADS_PER_GROUP = 4
NSA_BRANCHES = 3
CMP_BLOCK = 32
CMP_STRIDE = 16
CMP_HIDDEN = 256
SLC_BLOCK = 64
SLC_TOP_N = 16
WIN_SIZE = 512
RMS_EPS = 1e-6
NEG = -1e30
FORCE_SCORE = 1e9
LOG2E = math.log2(math.e)

TILE = 256
N_TILES = SEQ // TILE
N_PAIRS = N_TILES // 2
CAUSAL_STEPS = N_TILES + 1
STEP_UNROLL = 3
N_CMP = SEQ // CMP_STRIDE
CMP_CHUNK = CMP_STRIDE * HEAD_DIM
N_SLC = SEQ // SLC_BLOCK
WIN_TILES = -(-(WIN_SIZE - 1) // TILE) + 1
HEADS_PER_STEP = 4
VT_ROWS = HEAD_DIM + 16
MXU_COLS = 256

TAB_A, TAB_S, TAB_W = 0, N_TILES, 2 * N_TILES
TAB_W_ROWS = 4
TAB_ROWS = 2 * N_TILES + TAB_W_ROWS
VMEM_LIMIT = 52 * 1024 * 1024


def _np_bucket(dist):
    n = np.maximum(dist, 0)
    nf = np.maximum(n, 1).astype(np.float32)
    log_b = REL_MAX_EXACT + (
        np.log(nf / np.float32(REL_MAX_EXACT)) / np.float32(math.log(REL_MAX_DISTANCE / REL_MAX_EXACT))
        * np.float32(REL_BUCKETS - REL_MAX_EXACT)).astype(np.int32)
    return np.where(n < REL_MAX_EXACT, n, np.minimum(log_b, REL_BUCKETS - 1)).astype(np.int32)


@functools.lru_cache(maxsize=None)
def _static_maps():
    ki = np.arange(TILE)[:, None]
    qi = np.arange(TILE)[None, :]
    dist = TILE * np.arange(N_TILES)[:, None, None] + qi[None] - ki[None]
    bk_t = _np_bucket(dist)
    mult = np.zeros(dist.shape, np.int64)
    for window, dil in DIL_PATTERNS:
        mult += ((dist % dil == 0) & (dist <= window)).astype(np.int64)
    base_a = np.where((dist >= 0) & (mult > 0), np.log2(np.maximum(mult, 1)), NEG)
    base_s = np.where(dist >= 0, 0.0, NEG)
    base_w = np.full((TAB_W_ROWS, TILE, TILE), NEG)
    base_w[:WIN_TILES] = np.where((dist >= 0) & (dist < WIN_SIZE), 0.0, NEG)[:WIN_TILES]
    base_t = np.concatenate([base_a, base_s, base_w]).astype(np.float32)
    ci = np.arange(N_CMP)[:, None]
    t = TILE * np.arange(N_TILES)[:, None, None] + qi[None]
    dist_c = t - (CMP_STRIDE * ci[None] + CMP_BLOCK - 1)
    bk_c = _np_bucket(dist_c)
    base_c = np.where((dist_c >= 0) & (ci[None] < N_CMP - 1), 0.0, NEG).astype(np.float32)
    cs = np.arange(N_CMP)[None, :] * CMP_STRIDE
    sj = np.arange(N_SLC)[:, None] * SLC_BLOCK
    ov_t = ((cs < sj + SLC_BLOCK) & (cs + CMP_BLOCK > sj) & (np.arange(N_CMP)[None, :] < N_CMP - 1))
    onehot = np.zeros((SEQ, HEAD_DIM), np.float32)
    onehot[np.arange(SEQ), np.arange(SEQ) // SLC_BLOCK] = 1.0
    return bk_t.astype(np.int32), base_t, bk_c.astype(np.int32), base_c, ov_t.astype(np.float32), onehot


def _bias_kernel(tab_ref, bkt_ref, baset_ref, bkc_ref, basec_ref, out_ref, outc_ref, *, present_t, present_c):
    h = pl.program_id(0)
    tv = [tab_ref[b, h] * LOG2E for b in range(REL_BUCKETS)]

    def lookup(bk, present):
        val = jnp.full(bk.shape, tv[present[0]], F32)
        for b in present[1:]:
            val = jnp.where(bk == b, tv[b], val)
        return val

    for d in range(N_TILES):
        g = lookup(bkt_ref[d], present_t[d])
        out_ref[TAB_A + d] = g + baset_ref[TAB_A + d]
        out_ref[TAB_S + d] = g + baset_ref[TAB_S + d]
        if d < TAB_W_ROWS:
            out_ref[TAB_W + d] = g + baset_ref[TAB_W + d]
        outc_ref[d] = lookup(bkc_ref[d], present_c[d]) + basec_ref[d]


def _bias_tables(rel_table):
    bk_t, base_t, bk_c, base_c, _, _ = _static_maps()
    present_t = tuple(tuple(int(b) for b in np.unique(bk_t[d])) for d in range(N_TILES))
    present_c = tuple(tuple(int(b) for b in np.unique(bk_c[d])) for d in range(N_TILES))
    return pl.pallas_call(
        functools.partial(_bias_kernel, present_t=present_t, present_c=present_c),
        grid=(N_HEADS,),
        in_specs=[
            pl.BlockSpec(memory_space=pltpu.SMEM),
            pl.BlockSpec((N_TILES, TILE, TILE), lambda h: (0, 0, 0)),
            pl.BlockSpec((TAB_ROWS, TILE, TILE), lambda h: (0, 0, 0)),
            pl.BlockSpec((N_TILES, N_CMP, TILE), lambda h: (0, 0, 0)),
            pl.BlockSpec((N_TILES, N_CMP, TILE), lambda h: (0, 0, 0)),
        ],
        out_specs=[
            pl.BlockSpec((None, TAB_ROWS, TILE, TILE), lambda h: (h, 0, 0, 0)),
            pl.BlockSpec((None, N_TILES, N_CMP, TILE), lambda h: (h, 0, 0, 0)),
        ],
        out_shape=[
            jax.ShapeDtypeStruct((N_HEADS, TAB_ROWS, TILE, TILE), F32),
            jax.ShapeDtypeStruct((N_HEADS, N_TILES, N_CMP, TILE), F32),
        ],
        compiler_params=pltpu.CompilerParams(dimension_semantics=("arbitrary",),
                                             vmem_limit_bytes=VMEM_LIMIT),
        name="bias_tables",
    )(rel_table, jnp.asarray(bk_t), jnp.asarray(base_t), jnp.asarray(bk_c), jnp.asarray(base_c))


def _silu(z):
    return z * jax.nn.sigmoid(z)


def _nm_kernel(*refs, n_chunks, extra, silu_from):
    if extra == "gate":
        x_ref, g_ref, w_ref, wg_ref, o_ref, og_ref, xn_ref = refs
    elif extra == "chunks":
        x_ref, g_ref, w_ref, o_ref, oc_ref, xn_ref, res_ref = refs
    else:
        x_ref, g_ref, w_ref, o_ref, xn_ref = refs
    j = pl.program_id(1)

    @pl.when(j == 0)
    def _():
        x = x_ref[...]
        ms = jnp.mean(x * x, axis=-1, keepdims=True)
        xn_ref[...] = (x * lax.rsqrt(ms + RMS_EPS) * g_ref[...]).astype(BF16)
        if extra == "gate":
            og_ref[...] = jnp.dot(xn_ref[...], wg_ref[...], preferred_element_type=F32)

    def product(act):
        for cc in range(n_chunks // 2):
            res = jnp.dot(xn_ref[...], w_ref[:, cc * MXU_COLS:(cc + 1) * MXU_COLS], preferred_element_type=F32)
            for u in range(2):
                c = 2 * cc + u
                piece = res[:, u * HEAD_DIM:(u + 1) * HEAD_DIM]
                o_ref[c] = act(piece).astype(BF16)
                if extra == "chunks":
                    @pl.when(j == 0)
                    def _():
                        rows = res_ref.shape[1] // CMP_STRIDE
                        res_ref[c] = piece
                        for i in range(CMP_STRIDE):
                            oc_ref[c, :, i * HEAD_DIM:(i + 1) * HEAD_DIM] = res_ref[
                                c, pl.ds(i, rows, stride=CMP_STRIDE), :].astype(BF16)

    @pl.when(j < silu_from)
    def _():
        product(lambda t: t)

    @pl.when(j >= silu_from)
    def _():
        product(_silu)


def _norm_matmul(x2d, gain, w_bf, wg_bf=None, *, n_cols, silu_from_col, chunks=False, tm=1024, tn=1024):
    m, d = x2d.shape
    assert silu_from_col % tn == 0 and n_cols % tn == 0
    assert not (chunks and wg_bf is not None)
    batch = m // SEQ
    spb = SEQ // tm
    extra = "gate" if wg_bf is not None else ("chunks" if chunks else None)
    in_specs = [
        pl.BlockSpec((tm, d), lambda i, j: (i, 0)),
        pl.BlockSpec((1, d), lambda i, j: (0, 0)),
        pl.BlockSpec((d, tn), lambda i, j: (0, j)),
    ]
    out_shape = [jax.ShapeDtypeStruct((batch, n_cols // HEAD_DIM, SEQ, HEAD_DIM), BF16)]
    out_specs = [pl.BlockSpec((None, tn // HEAD_DIM, tm, HEAD_DIM), lambda i, j: (i // spb, j, i % spb, 0))]
    args = [x2d, gain.reshape(1, d), w_bf]
    scratch = [pltpu.VMEM((tm, d), BF16)]
    if extra == "gate":
        ng = wg_bf.shape[1]
        in_specs.append(pl.BlockSpec((d, ng), lambda i, j: (0, 0)))
        out_shape.append(jax.ShapeDtypeStruct((m, ng), F32))
        out_specs.append(pl.BlockSpec((tm, ng), lambda i, j: (i, 0)))
        args.append(wg_bf)
    elif extra == "chunks":
        out_shape.append(jax.ShapeDtypeStruct((batch, tn // HEAD_DIM, N_CMP, CMP_CHUNK), BF16))
        out_specs.append(pl.BlockSpec((None, tn // HEAD_DIM, tm // CMP_STRIDE, CMP_CHUNK),
                                      lambda i, j: (i // spb, 0, i % spb, 0)))
        scratch.append(pltpu.VMEM((tn // HEAD_DIM, tm, HEAD_DIM), F32))
    res = pl.pallas_call(
        functools.partial(_nm_kernel, n_chunks=tn // HEAD_DIM, extra=extra, silu_from=silu_from_col // tn),
        grid=(m // tm, n_cols // tn),
        in_specs=in_specs,
        out_specs=out_specs,
        out_shape=out_shape,
        scratch_shapes=scratch,
        compiler_params=pltpu.CompilerParams(dimension_semantics=("parallel", "arbitrary"),
                                             vmem_limit_bytes=VMEM_LIMIT),
        name="norm_matmul",
    )(*args)
    return res if extra else res[0]


def _op_kernel(a_ref, w_ref, r_ref, g_ref, o_ref, *, parts):
    rows = a_ref.shape[0] // parts
    for r in range(parts):
        sl = slice(r * rows, (r + 1) * rows)
        y = jnp.dot(a_ref[sl, :], w_ref[...], preferred_element_type=F32)
        ms = jnp.mean(y * y, axis=-1, keepdims=True)
        o_ref[sl, :] = r_ref[sl, :] + y * lax.rsqrt(ms + RMS_EPS) * g_ref[...]


def _out_proj(a2d, w_bf, resid2d, gain, *, tm=512, parts=2):
    m, k = a2d.shape
    n = w_bf.shape[1]
    return pl.pallas_call(
        functools.partial(_op_kernel, parts=parts),
        grid=(m // tm,),
        in_specs=[
            pl.BlockSpec((tm, k), lambda i: (i, 0)),
            pl.BlockSpec((k, n), lambda i: (0, 0), pipeline_mode=pl.Buffered(1)),
            pl.BlockSpec((tm, n), lambda i: (i, 0)),
            pl.BlockSpec((1, n), lambda i: (0, 0)),
        ],
        out_specs=pl.BlockSpec((tm, n), lambda i: (i, 0)),
        out_shape=jax.ShapeDtypeStruct((m, n), F32),
        compiler_params=pltpu.CompilerParams(dimension_semantics=("parallel",),
                                             vmem_limit_bytes=VMEM_LIMIT),
        name="out_proj",
    )(a2d, w_bf, resid2d, gain.reshape(1, n))


def _scaled_t(q):
    return (q.astype(F32) * (ATT_SCALE * LOG2E)).T.astype(BF16)


def _transpose_tiles(src_ref, dst_ref):
    for c in range(N_TILES):
        dst_ref[c, 0:HEAD_DIM, :] = src_ref[c * TILE:(c + 1) * TILE, :].astype(F32).T.astype(BF16)
        dst_ref[c, HEAD_DIM:VT_ROWS, :] = jnp.ones((VT_ROWS - HEAD_DIM, TILE), BF16)


def _pair_tiles(n):
    return (n, N_TILES - 1 - n)


def _tile_rows(idx):
    return pl.ds(pl.multiple_of(idx * TILE, TILE), TILE)


def _normalized(acc):
    return acc[0:HEAD_DIM, :] / acc[HEAD_DIM:HEAD_DIM + 1, :]


def _pair_attention(steps, step_fn, heads, q_ref, k_fn, vt_fn, bias_fn, s_ref, acc_ref, unrolled=False):
    def scores(t, ms):
        slot, j, b = step_fn(t)
        first = slot == 0
        m0, m1 = list(ms[0]), list(ms[1])
        for h in range(heads):
            s = jnp.dot(k_fn(h, j), q_ref[slot, h], preferred_element_type=F32) + bias_fn(h, b)
            s_ref[h, t] = s
            cm = jnp.max(s, axis=0, keepdims=True)
            m0[h] = jnp.where(first, jnp.maximum(m0[h], cm), m0[h])
            m1[h] = jnp.where(first, m1[h], jnp.maximum(m1[h], cm))
        return tuple(m0), tuple(m1)

    def loop(body, init):
        if unrolled:
            for t in range(steps):
                init = body(t, init)
            return init
        return lax.fori_loop(0, steps, body, init, unroll=STEP_UNROLL)

    neg = (jnp.full((1, TILE), NEG, F32),) * heads
    m0, m1 = loop(scores, (neg, neg))

    for slot in range(2):
        for h in range(heads):
            acc_ref[slot, h] = jnp.zeros((VT_ROWS, TILE), F32)

    def weighted(t, carry):
        slot, j, _ = step_fn(t)
        first = slot == 0
        for h in range(heads):
            p = jnp.exp2(s_ref[h, t] - jnp.where(first, m0[h], m1[h]))
            acc_ref[slot, h] += jnp.dot(vt_fn(h, j), p.astype(BF16), preferred_element_type=F32)
        return carry

    loop(weighted, 0)


def _causal_step(n):
    def step(t):
        first = t <= n
        return (jnp.where(first, 0, 1), jnp.where(first, t, t - n - 1), jnp.where(first, n - t, N_TILES - t))
    return step


def _window_step(n):
    def step(t):
        slot, d = divmod(t, WIN_TILES)
        j = _pair_tiles(n)[slot] - d
        return slot, jnp.maximum(j, 0), jnp.where(j >= 0, d, TAB_W_ROWS - 1)
    return step


def _attn_a_kernel(qa_ref, qb_ref, k_ref, v_ref, za_ref, zb_ref, r_ref, o_ref,
                   vt_ref, qt_ref, s_ref, acc_ref):
    n = pl.program_id(2)
    heads = HEADS_PER_STEP

    @pl.when(n == 0)
    def _():
        for h in range(heads):
            _transpose_tiles(v_ref.at[h], vt_ref.at[h])

    for slot, q_ref in enumerate((qa_ref, qb_ref)):
        for h in range(heads):
            qt_ref[slot, h] = _scaled_t(q_ref[h])

    _pair_attention(CAUSAL_STEPS, _causal_step(n), heads, qt_ref,
                    lambda h, j: k_ref[h, _tile_rows(j), :], lambda h, j: vt_ref[h, j],
                    lambda h, delta: r_ref[h, delta], s_ref, acc_ref)
    for slot, (z_ref, idx) in enumerate(zip((za_ref, zb_ref), _pair_tiles(n))):
        for h in range(heads):
            y = _normalized(acc_ref[slot, h]).T * z_ref[h].astype(F32)
            o_ref[_tile_rows(idx), h * HEAD_DIM:(h + 1) * HEAD_DIM] = y.astype(BF16)


def _resident(block_shape, index_map):
    return pl.BlockSpec(block_shape, index_map, pipeline_mode=pl.Buffered(1))


def _attn_a(qkvz, table):
    batch = qkvz.shape[0]
    heads = HEADS_PER_STEP
    hgroups = N_HEADS // heads

    def tile_slot(s, second):
        def index(hg, b, n):
            return (b, s * hgroups + hg, N_TILES - 1 - n if second else n, 0)
        return pl.BlockSpec((None, heads, TILE, HEAD_DIM), index)

    def full_slot(s):
        return pl.BlockSpec((None, heads, SEQ, HEAD_DIM), lambda hg, b, n: (b, s * hgroups + hg, 0, 0))

    return pl.pallas_call(
        _attn_a_kernel,
        grid=(hgroups, batch, N_PAIRS),
        in_specs=[tile_slot(0, False), tile_slot(0, True), full_slot(1), full_slot(2),
                  tile_slot(3, False), tile_slot(3, True),
                  _resident((heads, N_TILES, TILE, TILE), lambda hg, b, n: (hg, TAB_A // N_TILES, 0, 0))],
        out_specs=pl.BlockSpec((None, SEQ, heads * HEAD_DIM), lambda hg, b, n: (b, 0, hg)),
        out_shape=jax.ShapeDtypeStruct((batch, SEQ, ATT_WIDTH), BF16),
        scratch_shapes=[
            pltpu.VMEM((heads, N_TILES, VT_ROWS, TILE), BF16),
            pltpu.VMEM((2, heads, HEAD_DIM, TILE), BF16),
            pltpu.VMEM((heads, CAUSAL_STEPS, TILE, TILE), F32),
            pltpu.VMEM((2, heads, VT_ROWS, TILE), F32),
        ],
        compiler_params=pltpu.CompilerParams(dimension_semantics=("parallel", "parallel", "arbitrary"),
                                             vmem_limit_bytes=VMEM_LIMIT),
        name="dilated_attn",
    )(qkvz, qkvz, qkvz, qkvz, qkvz, qkvz, table)


def _cmp_kernel(c_ref, pos_ref, w1_ref, w2_ref, o_ref):
    c = c_ref[...].astype(F32)
    x_lo = (c + pos_ref[0]).astype(BF16)
    x_hi = (c + pos_ref[1]).astype(BF16)
    a = jnp.dot(x_lo, w1_ref[0:CMP_CHUNK, :], preferred_element_type=F32)
    bm = jnp.dot(x_hi, w1_ref[CMP_CHUNK:2 * CMP_CHUNK, :], preferred_element_type=F32)
    hid = jax.nn.gelu(a + pltpu.roll(bm, N_CMP - 1, 0))
    res = jnp.dot(hid.astype(BF16), w2_ref[...], preferred_element_type=F32)
    o_ref[...] = jnp.where(pl.program_id(0) == 1, res.T, res).astype(BF16)


def _compress(chunks, pos, w1_bf, w2_bf):
    batch = chunks.shape[0]
    return pl.pallas_call(
        _cmp_kernel,
        grid=(2, batch, NSA_KV_GROUPS),
        in_specs=[
            pl.BlockSpec((None, None, N_CMP, CMP_CHUNK), lambda t, b, g: (b, t * NSA_KV_GROUPS + g, 0, 0)),
            pl.BlockSpec((None, 2, 1, CMP_CHUNK), lambda t, b, g: (t, 0, 0, 0)),
            pl.BlockSpec((None, 2 * CMP_CHUNK, CMP_HIDDEN), lambda t, b, g: (t, 0, 0)),
            pl.BlockSpec((None, CMP_HIDDEN, HEAD_DIM), lambda t, b, g: (t, 0, 0)),
        ],
        out_specs=pl.BlockSpec((None, None, None, N_CMP, HEAD_DIM), lambda t, b, g: (t, b, g, 0, 0)),
        out_shape=jax.ShapeDtypeStruct((2, batch, NSA_KV_GROUPS, N_CMP, HEAD_DIM), BF16),
        compiler_params=pltpu.CompilerParams(dimension_semantics=("arbitrary", "arbitrary", "arbitrary"),
                                             vmem_limit_bytes=VMEM_LIMIT),
        name="compress_kv",
    )(chunks, pos, w1_bf, w2_bf)


def _nsa_kernel(qa_ref, qb_ref, zca_ref, zcb_ref, zsa_ref, zsb_ref, zwa_ref, zwb_ref, gla_ref, glb_ref,
                kc_ref, vct_ref, ks_ref, vs_ref, kw_ref, vw_ref, rs_ref, rw_ref, rca_ref, rcb_ref,
                ovt_ref, onehot_ref, o_ref,
                kaug_ref, vst_ref, vwt_ref, qa_sc, qw_sc, ycmp_ref, s_ref, accs_ref, accw_ref):
    n = pl.program_id(2)
    heads = NSA_HEADS_PER_GROUP
    tiles = _pair_tiles(n)

    @pl.when(n == 0)
    def _():
        kaug_ref[:, 0:HEAD_DIM] = ks_ref[...]
        kaug_ref[:, HEAD_DIM:2 * HEAD_DIM] = onehot_ref[...]
        _transpose_tiles(vs_ref, vst_ref)
        _transpose_tiles(vw_ref, vwt_ref)

    kc = kc_ref[...]
    vct = vct_ref[...]
    for slot, (q_ref, rc_ref, zc_ref, gl_ref) in enumerate(
            ((qa_ref, rca_ref, zca_ref, gla_ref), (qb_ref, rcb_ref, zcb_ref, glb_ref))):
        psum = jnp.zeros((N_CMP, TILE), F32)
        gates = jax.nn.sigmoid(gl_ref[...])
        for h in range(heads):
            qt = _scaled_t(q_ref[h])
            qw_sc[slot, h] = qt
            qa_sc[slot, h, 0:HEAD_DIM, :] = qt
            bias = rc_ref[h]
            s = jnp.dot(kc, qt, preferred_element_type=F32) + bias
            m = jnp.max(s, axis=0, keepdims=True)
            p = jnp.where(bias > 0.5 * NEG, jnp.exp2(s - m), 0.0)
            p = p / jnp.maximum(jnp.sum(p, axis=0, keepdims=True), 1e-30)
            psum = psum + p
            o_cmp = jnp.dot(vct, p.astype(BF16), preferred_element_type=F32)
            ycmp_ref[slot, h] = gates[:, h:h + 1] * o_cmp.T * zc_ref[h].astype(F32)

        imp = jnp.dot(ovt_ref[...], psum, precision=lax.Precision.HIGHEST, preferred_element_type=F32)
        blk = lax.broadcasted_iota(jnp.int32, (N_SLC, TILE), 0)
        tq = tiles[slot] * TILE + lax.broadcasted_iota(jnp.int32, (N_SLC, TILE), 1)
        cur = lax.shift_right_logical(tq, int(math.log2(SLC_BLOCK)))
        forced = (blk == 0) | (blk == cur) | (blk == cur - 1)
        score = jnp.where(forced, FORCE_SCORE, jnp.where(blk > cur, -FORCE_SCORE, imp))
        cnt = jnp.zeros((N_SLC, TILE), jnp.int32)
        for jj in range(N_SLC):
            sj = score[jj:jj + 1, :]
            beats = (sj > score) | ((sj == score) & (blk > jj))
            cnt = cnt + beats.astype(jnp.int32)
        mask = jnp.where(cnt < SLC_TOP_N, 0.0, NEG)
        mask = jnp.concatenate([mask, jnp.zeros((HEAD_DIM - N_SLC, TILE), F32)], axis=0).astype(BF16)
        for h in range(heads):
            qa_sc[slot, h, HEAD_DIM:2 * HEAD_DIM, :] = mask

    _pair_attention(CAUSAL_STEPS, _causal_step(n), heads, qa_sc,
                    lambda h, j: kaug_ref[_tile_rows(j), :], lambda h, j: vst_ref[j],
                    lambda h, delta: rs_ref[h, delta], s_ref, accs_ref)
    _pair_attention(2 * WIN_TILES, _window_step(n), heads, qw_sc,
                    lambda h, j: kw_ref[_tile_rows(j), :], lambda h, j: vwt_ref[j],
                    lambda h, b: rw_ref[h, b], s_ref, accw_ref, unrolled=True)

    for slot, (zs_ref, zw_ref, gl_ref) in enumerate(((zsa_ref, zwa_ref, gla_ref), (zsb_ref, zwb_ref, glb_ref))):
        gates = jax.nn.sigmoid(gl_ref[...])
        for h in range(heads):
            y = (ycmp_ref[slot, h]
                 + gates[:, heads + h:heads + h + 1] * _normalized(accs_ref[slot, h]).T * zs_ref[h].astype(F32)
                 + gates[:, 2 * heads + h:2 * heads + h + 1] * _normalized(accw_ref[slot, h]).T
                 * zw_ref[h].astype(F32))
            o_ref[_tile_rows(tiles[slot]), h * HEAD_DIM:(h + 1) * HEAD_DIM] = y.astype(BF16)


def _nsa_attn(qz, gate_logits, kv, cmp_kv, table, table_c):
    batch = qz.shape[0]
    heads = NSA_HEADS_PER_GROUP
    groups = NSA_KV_GROUPS
    _, _, _, _, ov_t, onehot = _static_maps()

    def tile_of(n, second):
        return N_TILES - 1 - n if second else n

    def qz_specs(slot_group):
        return [pl.BlockSpec((None, heads, TILE, HEAD_DIM),
                             lambda g, b, n, second=second: (b, slot_group * groups + g, tile_of(n, second), 0))
                for second in (False, True)]

    def kv_spec(branch):
        return pl.BlockSpec((None, None, SEQ, HEAD_DIM), lambda g, b, n: (b, branch * groups + g, 0, 0))

    def cmp_spec(t):
        return pl.BlockSpec((None, None, None, N_CMP, HEAD_DIM), lambda g, b, n: (t, b, g, 0, 0))

    gl_specs = [pl.BlockSpec((None, TILE, HEAD_DIM), lambda g, b, n, second=second: (b, tile_of(n, second), g))
                for second in (False, True)]
    rc_specs = [pl.BlockSpec((heads, None, N_CMP, TILE), lambda g, b, n, second=second: (g, tile_of(n, second), 0, 0))
                for second in (False, True)]

    return pl.pallas_call(
        _nsa_kernel,
        grid=(groups, batch, N_PAIRS),
        in_specs=[
            *qz_specs(0), *qz_specs(1), *qz_specs(2), *qz_specs(3), *gl_specs,
            cmp_spec(0), cmp_spec(1), kv_spec(2), kv_spec(3), kv_spec(4), kv_spec(5),
            _resident((heads, N_TILES, TILE, TILE), lambda g, b, n: (g, TAB_S // N_TILES, 0, 0)),
            _resident((heads, TAB_W_ROWS, TILE, TILE), lambda g, b, n: (g, TAB_W // TAB_W_ROWS, 0, 0)),
            *rc_specs,
            _resident((N_SLC, N_CMP), lambda g, b, n: (0, 0)),
            _resident((SEQ, HEAD_DIM), lambda g, b, n: (0, 0)),
        ],
        out_specs=pl.BlockSpec((None, SEQ, heads * HEAD_DIM), lambda g, b, n: (b, 0, g)),
        out_shape=jax.ShapeDtypeStruct((batch, SEQ, ATT_WIDTH), BF16),
        scratch_shapes=[
            pltpu.VMEM((SEQ, 2 * HEAD_DIM), BF16),
            pltpu.VMEM((N_TILES, VT_ROWS, TILE), BF16),
            pltpu.VMEM((N_TILES, VT_ROWS, TILE), BF16),
            pltpu.VMEM((2, heads, 2 * HEAD_DIM, TILE), BF16),
            pltpu.VMEM((2, heads, HEAD_DIM, TILE), BF16),
            pltpu.VMEM((2, heads, TILE, HEAD_DIM), F32),
            pltpu.VMEM((heads, CAUSAL_STEPS, TILE, TILE), F32),
            pltpu.VMEM((2, heads, VT_ROWS, TILE), F32),
            pltpu.VMEM((2, heads, VT_ROWS, TILE), F32),
        ],
        compiler_params=pltpu.CompilerParams(dimension_semantics=("parallel", "parallel", "arbitrary"),
                                             vmem_limit_bytes=VMEM_LIMIT),
        name="nsa_attn",
    )(qz, qz, qz, qz, qz, qz, qz, qz, gate_logits, gate_logits, cmp_kv, cmp_kv, kv, kv, kv, kv,
      table, table, table_c, table_c, jnp.asarray(ov_t), jnp.asarray(onehot, dtype=BF16))


def kernel(x, norm_pre, norm_post, rel_table, w_in_a, w_out_a, kv_norm, w_kv, cmp_pos_k, cmp_pos_v,
           cmp_w1_k, cmp_w2_k, cmp_w1_v, cmp_w2_v, w_in_b, w_out_b):
    batch = x.shape[0]
    m = batch * SEQ
    heads = NSA_HEADS_PER_GROUP
    groups = NSA_KV_GROUPS
    x2d = x.reshape(m, D_MODEL)
    table, table_c = _bias_tables(rel_table)

    qkvz = _norm_matmul(x2d, norm_pre[0], w_in_a[0].astype(BF16), n_cols=4 * ATT_WIDTH,
                        silu_from_col=3 * ATT_WIDTH)
    att_a = _attn_a(qkvz, table)
    h1 = _out_proj(att_a.reshape(m, ATT_WIDTH), w_out_a[0].astype(BF16), x2d, norm_post[0])

    n_kv = w_kv.shape[1]
    kv, cmp_chunks = _norm_matmul(h1, kv_norm, w_kv.astype(BF16), n_cols=n_kv, silu_from_col=n_kv, chunks=True)
    pos = jnp.stack([cmp_pos_k, cmp_pos_v]).reshape(2, 2, 1, CMP_CHUNK)
    cmp_kv = _compress(cmp_chunks, pos, jnp.stack([cmp_w1_k, cmp_w1_v]).astype(BF16),
                       jnp.stack([cmp_w2_k, cmp_w2_v]).astype(BF16))

    w_b = w_in_b[0]
    n_main = (1 + NSA_BRANCHES) * ATT_WIDTH
    wg = w_b[:, n_main:].reshape(D_MODEL, NSA_BRANCHES, groups, heads).transpose(0, 2, 1, 3)
    wg = jnp.pad(wg.reshape(D_MODEL, groups, NSA_BRANCHES * heads),
                 ((0, 0), (0, 0), (0, HEAD_DIM - NSA_BRANCHES * heads))).reshape(D_MODEL, groups * HEAD_DIM)
    qz, gate_logits = _norm_matmul(h1, norm_pre[1], w_b.astype(BF16), wg.astype(BF16), n_cols=n_main,
                                   silu_from_col=ATT_WIDTH)
    att_b = _nsa_attn(qz, gate_logits.reshape(batch, SEQ, groups * HEAD_DIM), kv, cmp_kv, table, table_c)
    out = _out_proj(att_b.reshape(m, ATT_WIDTH), w_out_b[0].astype(BF16), h1, norm_post[1])
    return out.reshape(batch, SEQ, D_MODEL)
```

```python
import functools
import math

import numpy as np
import jax
import jax.numpy as jnp
from jax import lax
from jax.experimental import pallas as pl
from jax.experimental.pallas import tpu as pltpu

F32 = jnp.float32
BF16 = jnp.bfloat16

D_MODEL = 2048
SEQ = 2048
N_HEADS = 16
HEAD_DIM = 128
ATT_WIDTH = N_HEADS * HEAD_DIM
ATT_SCALE = HEAD_DIM ** -0.5
DIL_PATTERNS = ((128, 1), (512, 4), (2048, 16))
REL_BUCKETS = 32
REL_MAX_EXACT = 16
REL_MAX_DISTANCE = 2048
NSA_KV_GROUPS = 4
NSA_HEADS_PER_GROUP = 4
NSA_BRANCHES = 3
CMP_BLOCK = 32
CMP_STRIDE = 16
CMP_HIDDEN = 256
SLC_BLOCK = 64
SLC_TOP_N = 16
WIN_SIZE = 512
RMS_EPS = 1e-6
NEG = -1e30
FORCE_SCORE = 1e9
LOG2E = math.log2(math.e)

TILE = 256
N_TILES = SEQ // TILE
N_PAIRS = N_TILES // 2
CAUSAL_STEPS = N_TILES + 1
STEP_UNROLL = 3
N_CMP = SEQ // CMP_STRIDE
CMP_CHUNK = CMP_STRIDE * HEAD_DIM
N_SLC = SEQ // SLC_BLOCK
WIN_TILES = -(-(WIN_SIZE - 1) // TILE) + 1
HEADS_PER_STEP = 4
VT_ROWS = HEAD_DIM + 16
MXU_COLS = 256

TAB_A, TAB_S, TAB_W = 0, N_TILES, 2 * N_TILES
TAB_W_ROWS = 4
TAB_ROWS = 2 * N_TILES + TAB_W_ROWS
VMEM_LIMIT = 52 * 1024 * 1024


def _np_bucket(dist):
    n = np.maximum(dist, 0)
    nf = np.maximum(n, 1).astype(np.float32)
    log_b = REL_MAX_EXACT + (
        np.log(nf / np.float32(REL_MAX_EXACT)) / np.float32(math.log(REL_MAX_DISTANCE / REL_MAX_EXACT))
        * np.float32(REL_BUCKETS - REL_MAX_EXACT)).astype(np.int32)
    return np.where(n < REL_MAX_EXACT, n, np.minimum(log_b, REL_BUCKETS - 1)).astype(np.int32)


@functools.lru_cache(maxsize=None)
def _static_maps():
    ki = np.arange(TILE)[:, None]
    qi = np.arange(TILE)[None, :]
    dist = TILE * np.arange(N_TILES)[:, None, None] + qi[None] - ki[None]
    bk_t = _np_bucket(dist)
    mult = np.zeros(dist.shape, np.int64)
    for window, dil in DIL_PATTERNS:
        mult += ((dist % dil == 0) & (dist <= window)).astype(np.int64)
    base_a = np.where((dist >= 0) & (mult > 0), np.log2(np.maximum(mult, 1)), NEG)
    base_s = np.where(dist >= 0, 0.0, NEG)
    base_w = np.full((TAB_W_ROWS, TILE, TILE), NEG)
    base_w[:WIN_TILES] = np.where((dist >= 0) & (dist < WIN_SIZE), 0.0, NEG)[:WIN_TILES]
    base_t = np.concatenate([base_a, base_s, base_w]).astype(np.float32)
    ci = np.arange(N_CMP)[:, None]
    t = TILE * np.arange(N_TILES)[:, None, None] + qi[None]
    dist_c = t - (CMP_STRIDE * ci[None] + CMP_BLOCK - 1)
    bk_c = _np_bucket(dist_c)
    base_c = np.where((dist_c >= 0) & (ci[None] < N_CMP - 1), 0.0, NEG).astype(np.float32)
    cs = np.arange(N_CMP)[None, :] * CMP_STRIDE
    sj = np.arange(N_SLC)[:, None] * SLC_BLOCK
    ov_t = ((cs < sj + SLC_BLOCK) & (cs + CMP_BLOCK > sj) & (np.arange(N_CMP)[None, :] < N_CMP - 1))
    onehot = np.zeros((SEQ, HEAD_DIM), np.float32)
    onehot[np.arange(SEQ), np.arange(SEQ) // SLC_BLOCK] = 1.0
    return bk_t.astype(np.int32), base_t, bk_c.astype(np.int32), base_c, ov_t.astype(np.float32), onehot


def _bias_kernel(tab_ref, bkt_ref, baset_ref, bkc_ref, basec_ref, out_ref, outc_ref, *, present_t, present_c):
    h = pl.program_id(0)
    tv = [tab_ref[b, h] * LOG2E for b in range(REL_BUCKETS)]

    def lookup(bk, present):
        val = jnp.full(bk.shape, tv[present[0]], F32)
        for b in present[1:]:
            val = jnp.where(bk == b, tv[b], val)
        return val

    for d in range(N_TILES):
        g = lookup(bkt_ref[d], present_t[d])
        out_ref[TAB_A + d] = g + baset_ref[TAB_A + d]
        out_ref[TAB_S + d] = g + baset_ref[TAB_S + d]
        if d < TAB_W_ROWS:
            out_ref[TAB_W + d] = g + baset_ref[TAB_W + d]
        outc_ref[d] = lookup(bkc_ref[d], present_c[d]) + basec_ref[d]


def _bias_tables(rel_table):
    bk_t, base_t, bk_c, base_c, _, _ = _static_maps()
    present_t = tuple(tuple(int(b) for b in np.unique(bk_t[d])) for d in range(N_TILES))
    present_c = tuple(tuple(int(b) for b in np.unique(bk_c[d])) for d in range(N_TILES))
    return pl.pallas_call(
        functools.partial(_bias_kernel, present_t=present_t, present_c=present_c),
        grid=(N_HEADS,),
        in_specs=[
            pl.BlockSpec(memory_space=pltpu.SMEM),
            pl.BlockSpec((N_TILES, TILE, TILE), lambda h: (0, 0, 0)),
            pl.BlockSpec((TAB_ROWS, TILE, TILE), lambda h: (0, 0, 0)),
            pl.BlockSpec((N_TILES, N_CMP, TILE), lambda h: (0, 0, 0)),
            pl.BlockSpec((N_TILES, N_CMP, TILE), lambda h: (0, 0, 0)),
        ],
        out_specs=[
            pl.BlockSpec((None, TAB_ROWS, TILE, TILE), lambda h: (h, 0, 0, 0)),
            pl.BlockSpec((None, N_TILES, N_CMP, TILE), lambda h: (h, 0, 0, 0)),
        ],
        out_shape=[
            jax.ShapeDtypeStruct((N_HEADS, TAB_ROWS, TILE, TILE), F32),
            jax.ShapeDtypeStruct((N_HEADS, N_TILES, N_CMP, TILE), F32),
        ],
        compiler_params=pltpu.CompilerParams(dimension_semantics=("arbitrary",),
                                             vmem_limit_bytes=VMEM_LIMIT),
        name="bias_tables",
    )(rel_table, jnp.asarray(bk_t), jnp.asarray(base_t), jnp.asarray(bk_c), jnp.asarray(base_c))


def _silu(z):
    half = 0.5 * z
    return half + half * jnp.tanh(half)


def _nm_kernel(*refs, n_chunks, extra, silu_from):
    if extra == "gate":
        x_ref, g_ref, w_ref, wg_ref, o_ref, og_ref, xn_ref = refs
    elif extra == "chunks":
        x_ref, g_ref, w_ref, o_ref, oc_ref, xn_ref, res_ref = refs
    else:
        x_ref, g_ref, w_ref, o_ref, xn_ref = refs
    j = pl.program_id(1)

    @pl.when(j == 0)
    def _():
        x = x_ref[...]
        ms = jnp.mean(x * x, axis=-1, keepdims=True)
        xn_ref[...] = (x * lax.rsqrt(ms + RMS_EPS) * g_ref[...]).astype(BF16)
        if extra == "gate":
            og_ref[...] = jnp.dot(xn_ref[...], wg_ref[...], preferred_element_type=F32)

    def product(act, stash=False):
        for cc in range(n_chunks // 2):
            res = jnp.dot(xn_ref[...], w_ref[:, cc * MXU_COLS:(cc + 1) * MXU_COLS], preferred_element_type=F32)
            for u in range(2):
                c = 2 * cc + u
                piece = res[:, u * HEAD_DIM:(u + 1) * HEAD_DIM]
                o_ref[c] = act(piece).astype(BF16)
                if stash:
                    res_ref[c] = piece

    first_plain = 0
    if extra == "chunks":
        first_plain = 1

        @pl.when(j == 0)
        def _():
            product(lambda t: t, stash=True)
            rows = res_ref.shape[1] // CMP_STRIDE
            for c in range(n_chunks):
                for i in range(CMP_STRIDE):
                    oc_ref[c, :, i * HEAD_DIM:(i + 1) * HEAD_DIM] = res_ref[
                        c, pl.ds(i, rows, stride=CMP_STRIDE), :].astype(BF16)

    @pl.when((j >= first_plain) & (j < silu_from))
    def _():
        product(lambda t: t)

    @pl.when(j >= silu_from)
    def _():
        product(_silu)


def _norm_matmul(x2d, gain, w_bf, wg_bf=None, *, n_cols, silu_from_col, chunks=False, tm=1024, tn=1024):
    m, d = x2d.shape
    assert silu_from_col % tn == 0 and n_cols % tn == 0
    assert not (chunks and wg_bf is not None)
    batch = m // SEQ
    spb = SEQ // tm
    extra = "gate" if wg_bf is not None else ("chunks" if chunks else None)
    in_specs = [
        pl.BlockSpec((tm, d), lambda i, j: (i, 0)),
        pl.BlockSpec((1, d), lambda i, j: (0, 0)),
        pl.BlockSpec((d, tn), lambda i, j: (0, j)),
    ]
    out_shape = [jax.ShapeDtypeStruct((batch, n_cols // HEAD_DIM, SEQ, HEAD_DIM), BF16)]
    out_specs = [pl.BlockSpec((None, tn // HEAD_DIM, tm, HEAD_DIM), lambda i, j: (i // spb, j, i % spb, 0))]
    args = [x2d, gain.reshape(1, d), w_bf]
    scratch = [pltpu.VMEM((tm, d), BF16)]
    if extra == "gate":
        ng = wg_bf.shape[1]
        in_specs.append(pl.BlockSpec((d, ng), lambda i, j: (0, 0)))
        out_shape.append(jax.ShapeDtypeStruct((m, ng), F32))
        out_specs.append(pl.BlockSpec((tm, ng), lambda i, j: (i, 0)))
        args.append(wg_bf)
    elif extra == "chunks":
        out_shape.append(jax.ShapeDtypeStruct((batch, tn // HEAD_DIM, N_CMP, CMP_CHUNK), BF16))
        out_specs.append(pl.BlockSpec((None, tn // HEAD_DIM, tm // CMP_STRIDE, CMP_CHUNK),
                                      lambda i, j: (i // spb, 0, i % spb, 0)))
        scratch.append(pltpu.VMEM((tn // HEAD_DIM, tm, HEAD_DIM), F32))
    res = pl.pallas_call(
        functools.partial(_nm_kernel, n_chunks=tn // HEAD_DIM, extra=extra, silu_from=silu_from_col // tn),
        grid=(m // tm, n_cols // tn),
        in_specs=in_specs,
        out_specs=out_specs,
        out_shape=out_shape,
        scratch_shapes=scratch,
        compiler_params=pltpu.CompilerParams(dimension_semantics=("parallel", "arbitrary"),
                                             vmem_limit_bytes=VMEM_LIMIT),
        name="norm_matmul",
    )(*args)
    return res if extra else res[0]


def _op_kernel(a_ref, w_ref, r_ref, g_ref, o_ref, *, parts):
    rows = a_ref.shape[0] // parts
    for r in range(parts):
        sl = slice(r * rows, (r + 1) * rows)
        y = jnp.dot(a_ref[sl, :], w_ref[...], preferred_element_type=F32)
        ms = jnp.mean(y * y, axis=-1, keepdims=True)
        o_ref[sl, :] = r_ref[sl, :] + y * lax.rsqrt(ms + RMS_EPS) * g_ref[...]


def _out_proj(a2d, w_bf, resid2d, gain, *, tm=512, parts=2):
    m, k = a2d.shape
    n = w_bf.shape[1]
    return pl.pallas_call(
        functools.partial(_op_kernel, parts=parts),
        grid=(m // tm,),
        in_specs=[
            pl.BlockSpec((tm, k), lambda i: (i, 0)),
            pl.BlockSpec((k, n), lambda i: (0, 0), pipeline_mode=pl.Buffered(1)),
            pl.BlockSpec((tm, n), lambda i: (i, 0)),
            pl.BlockSpec((1, n), lambda i: (0, 0)),
        ],
        out_specs=pl.BlockSpec((tm, n), lambda i: (i, 0)),
        out_shape=jax.ShapeDtypeStruct((m, n), F32),
        compiler_params=pltpu.CompilerParams(dimension_semantics=("parallel",),
                                             vmem_limit_bytes=VMEM_LIMIT),
        name="out_proj",
    )(a2d, w_bf, resid2d, gain.reshape(1, n))


def _scaled_t(q):
    return (q.astype(F32) * (ATT_SCALE * LOG2E)).T.astype(BF16)


def _transpose_tiles(src_ref, dst_ref):
    for c in range(N_TILES):
        dst_ref[c, 0:HEAD_DIM, :] = src_ref[c * TILE:(c + 1) * TILE, :].astype(F32).T.astype(BF16)
        dst_ref[c, HEAD_DIM:VT_ROWS, :] = jnp.ones((VT_ROWS - HEAD_DIM, TILE), BF16)


def _pair_tiles(n):
    return (n, N_TILES - 1 - n)


def _tile_rows(idx):
    return pl.ds(pl.multiple_of(idx * TILE, TILE), TILE)


def _normalized(acc):
    return acc[0:HEAD_DIM, :] / acc[HEAD_DIM:HEAD_DIM + 1, :]


def _pair_attention(steps, step_fn, heads, q_ref, k_fn, vt_fn, bias_fn, s_ref, acc_ref, unrolled=False):
    def scores(t, ms):
        slot, j, b = step_fn(t)
        first = slot == 0
        m0, m1 = list(ms[0]), list(ms[1])
        for h in range(heads):
            s = jnp.dot(k_fn(h, j), q_ref[slot, h], preferred_element_type=F32) + bias_fn(h, b)
            s_ref[h, t] = s
            cm = jnp.max(s, axis=0, keepdims=True)
            m0[h] = jnp.where(first, jnp.maximum(m0[h], cm), m0[h])
            m1[h] = jnp.where(first, m1[h], jnp.maximum(m1[h], cm))
        return tuple(m0), tuple(m1)

    def loop(body, init):
        if unrolled:
            for t in range(steps):
                init = body(t, init)
            return init
        return lax.fori_loop(0, steps, body, init, unroll=STEP_UNROLL)

    neg = (jnp.full((1, TILE), NEG, F32),) * heads
    m0, m1 = loop(scores, (neg, neg))

    for slot in range(2):
        for h in range(heads):
            acc_ref[slot, h] = jnp.zeros((VT_ROWS, TILE), F32)

    def weighted(t, carry):
        slot, j, _ = step_fn(t)
        first = slot == 0
        for h in range(heads):
            p = jnp.exp2((s_ref[h, t] - jnp.where(first, m0[h], m1[h])).astype(BF16))
            acc_ref[slot, h] += jnp.dot(vt_fn(h, j), p, preferred_element_type=F32)
        return carry

    loop(weighted, 0)


def _causal_step(n):
    def step(t):
        first = t <= n
        return (jnp.where(first, 0, 1), jnp.where(first, t, t - n - 1), jnp.where(first, n - t, N_TILES - t))
    return step


def _window_step(n):
    def step(t):
        slot, d = divmod(t, WIN_TILES)
        j = _pair_tiles(n)[slot] - d
        return slot, jnp.maximum(j, 0), jnp.where(j >= 0, d, TAB_W_ROWS - 1)
    return step


def _attn_a_kernel(qa_ref, qb_ref, k_ref, v_ref, za_ref, zb_ref, r_ref, o_ref,
                   vt_ref, qt_ref, s_ref, acc_ref):
    n = pl.program_id(2)
    heads = HEADS_PER_STEP

    @pl.when(n == 0)
    def _():
        for h in range(heads):
            _transpose_tiles(v_ref.at[h], vt_ref.at[h])

    for slot, q_ref in enumerate((qa_ref, qb_ref)):
        for h in range(heads):
            qt_ref[slot, h] = _scaled_t(q_ref[h])

    _pair_attention(CAUSAL_STEPS, _causal_step(n), heads, qt_ref,
                    lambda h, j: k_ref[h, _tile_rows(j), :], lambda h, j: vt_ref[h, j],
                    lambda h, delta: r_ref[h, delta], s_ref, acc_ref)
    for slot, (z_ref, idx) in enumerate(zip((za_ref, zb_ref), _pair_tiles(n))):
        for h in range(heads):
            y = _normalized(acc_ref[slot, h]).T * z_ref[h].astype(F32)
            o_ref[_tile_rows(idx), h * HEAD_DIM:(h + 1) * HEAD_DIM] = y.astype(BF16)


def _resident(block_shape, index_map):
    return pl.BlockSpec(block_shape, index_map, pipeline_mode=pl.Buffered(1))


def _attn_a(qkvz, table):
    batch = qkvz.shape[0]
    heads = HEADS_PER_STEP
    hgroups = N_HEADS // heads

    def tile_slot(s, second):
        def index(hg, b, n):
            return (b, s * hgroups + hg, N_TILES - 1 - n if second else n, 0)
        return pl.BlockSpec((None, heads, TILE, HEAD_DIM), index)

    def full_slot(s):
        return pl.BlockSpec((None, heads, SEQ, HEAD_DIM), lambda hg, b, n: (b, s * hgroups + hg, 0, 0))

    return pl.pallas_call(
        _attn_a_kernel,
        grid=(hgroups, batch, N_PAIRS),
        in_specs=[tile_slot(0, False), tile_slot(0, True), full_slot(1), full_slot(2),
                  tile_slot(3, False), tile_slot(3, True),
                  _resident((heads, N_TILES, TILE, TILE), lambda hg, b, n: (hg, TAB_A // N_TILES, 0, 0))],
        out_specs=pl.BlockSpec((None, SEQ, heads * HEAD_DIM), lambda hg, b, n: (b, 0, hg)),
        out_shape=jax.ShapeDtypeStruct((batch, SEQ, ATT_WIDTH), BF16),
        scratch_shapes=[
            pltpu.VMEM((heads, N_TILES, VT_ROWS, TILE), BF16),
            pltpu.VMEM((2, heads, HEAD_DIM, TILE), BF16),
            pltpu.VMEM((heads, CAUSAL_STEPS, TILE, TILE), F32),
            pltpu.VMEM((2, heads, VT_ROWS, TILE), F32),
        ],
        compiler_params=pltpu.CompilerParams(dimension_semantics=("parallel", "parallel", "arbitrary"),
                                             vmem_limit_bytes=VMEM_LIMIT),
        name="dilated_attn",
    )(qkvz, qkvz, qkvz, qkvz, qkvz, qkvz, table)


def _cmp_kernel(c_ref, pos_ref, w1_ref, w2_ref, o_ref):
    c = c_ref[...].astype(F32)
    x_lo = (c + pos_ref[0]).astype(BF16)
    x_hi = (c + pos_ref[1]).astype(BF16)
    a = jnp.dot(x_lo, w1_ref[0:CMP_CHUNK, :], preferred_element_type=F32)
    bm = jnp.dot(x_hi, w1_ref[CMP_CHUNK:2 * CMP_CHUNK, :], preferred_element_type=F32)
    hid = jax.nn.gelu(a + pltpu.roll(bm, N_CMP - 1, 0))
    res = jnp.dot(hid.astype(BF16), w2_ref[...], preferred_element_type=F32)
    o_ref[...] = jnp.where(pl.program_id(0) == 1, res.T, res).astype(BF16)


def _compress(chunks, pos, w1_bf, w2_bf):
    batch = chunks.shape[0]
    return pl.pallas_call(
        _cmp_kernel,
        grid=(2, batch, NSA_KV_GROUPS),
        in_specs=[
            pl.BlockSpec((None, None, N_CMP, CMP_CHUNK), lambda t, b, g: (b, t * NSA_KV_GROUPS + g, 0, 0)),
            pl.BlockSpec((None, 2, 1, CMP_CHUNK), lambda t, b, g: (t, 0, 0, 0)),
            pl.BlockSpec((None, 2 * CMP_CHUNK, CMP_HIDDEN), lambda t, b, g: (t, 0, 0)),
            pl.BlockSpec((None, CMP_HIDDEN, HEAD_DIM), lambda t, b, g: (t, 0, 0)),
        ],
        out_specs=pl.BlockSpec((None, None, None, N_CMP, HEAD_DIM), lambda t, b, g: (t, b, g, 0, 0)),
        out_shape=jax.ShapeDtypeStruct((2, batch, NSA_KV_GROUPS, N_CMP, HEAD_DIM), BF16),
        compiler_params=pltpu.CompilerParams(dimension_semantics=("arbitrary", "arbitrary", "arbitrary"),
                                             vmem_limit_bytes=VMEM_LIMIT),
        name="compress_kv",
    )(chunks, pos, w1_bf, w2_bf)


def _nsa_kernel(qa_ref, qb_ref, zca_ref, zcb_ref, zsa_ref, zsb_ref, zwa_ref, zwb_ref, gla_ref, glb_ref,
                kc_ref, vct_ref, ks_ref, vs_ref, kw_ref, vw_ref, rs_ref, rw_ref, rca_ref, rcb_ref,
                ovt_ref, onehot_ref, o_ref,
                kaug_ref, vst_ref, vwt_ref, qa_sc, qw_sc, ycmp_ref, s_ref, accs_ref, accw_ref):
    n = pl.program_id(2)
    heads = NSA_HEADS_PER_GROUP
    tiles = _pair_tiles(n)

    @pl.when(n == 0)
    def _():
        kaug_ref[:, 0:HEAD_DIM] = ks_ref[...]
        kaug_ref[:, HEAD_DIM:2 * HEAD_DIM] = onehot_ref[...]
        _transpose_tiles(vs_ref, vst_ref)
        _transpose_tiles(vw_ref, vwt_ref)

    kc = kc_ref[...]
    vct = vct_ref[...]
    for slot, (q_ref, rc_ref, zc_ref, gl_ref) in enumerate(
            ((qa_ref, rca_ref, zca_ref, gla_ref), (qb_ref, rcb_ref, zcb_ref, glb_ref))):
        psum = jnp.zeros((N_CMP, TILE), F32)
        gates = jax.nn.sigmoid(gl_ref[...])
        for h in range(heads):
            qt = _scaled_t(q_ref[h])
            qw_sc[slot, h] = qt
            qa_sc[slot, h, 0:HEAD_DIM, :] = qt
            bias = rc_ref[h]
            s = jnp.dot(kc, qt, preferred_element_type=F32) + bias
            m = jnp.max(s, axis=0, keepdims=True)
            p = jnp.where(bias > 0.5 * NEG, jnp.exp2(s - m), 0.0)
            p = p / jnp.maximum(jnp.sum(p, axis=0, keepdims=True), 1e-30)
            psum = psum + p
            o_cmp = jnp.dot(vct, p.astype(BF16), preferred_element_type=F32)
            ycmp_ref[slot, h] = gates[:, h:h + 1] * o_cmp.T * zc_ref[h].astype(F32)

        imp = jnp.dot(ovt_ref[...], psum, precision=lax.Precision.HIGHEST, preferred_element_type=F32)
        blk = lax.broadcasted_iota(jnp.int32, (N_SLC, TILE), 0)
        tq = tiles[slot] * TILE + lax.broadcasted_iota(jnp.int32, (N_SLC, TILE), 1)
        cur = lax.shift_right_logical(tq, int(math.log2(SLC_BLOCK)))
        forced = (blk == 0) | (blk == cur) | (blk == cur - 1)
        score = jnp.where(forced, FORCE_SCORE, jnp.where(blk > cur, -FORCE_SCORE, imp))
        cnt = jnp.zeros((N_SLC, TILE), jnp.int32)
        for jj in range(N_SLC):
            sj = score[jj:jj + 1, :]
            beats = (sj > score) | ((sj == score) & (blk > jj))
            cnt = cnt + beats.astype(jnp.int32)
        mask = jnp.where(cnt < SLC_TOP_N, 0.0, NEG)
        mask = jnp.concatenate([mask, jnp.zeros((HEAD_DIM - N_SLC, TILE), F32)], axis=0).astype(BF16)
        for h in range(heads):
            qa_sc[slot, h, HEAD_DIM:2 * HEAD_DIM, :] = mask

    _pair_attention(CAUSAL_STEPS, _causal_step(n), heads, qa_sc,
                    lambda h, j: kaug_ref[_tile_rows(j), :], lambda h, j: vst_ref[j],
                    lambda h, delta: rs_ref[h, delta], s_ref, accs_ref)
    _pair_attention(2 * WIN_TILES, _window_step(n), heads, qw_sc,
                    lambda h, j: kw_ref[_tile_rows(j), :], lambda h, j: vwt_ref[j],
                    lambda h, b: rw_ref[h, b], s_ref, accw_ref, unrolled=True)

    for slot, (zs_ref, zw_ref, gl_ref) in enumerate(((zsa_ref, zwa_ref, gla_ref), (zsb_ref, zwb_ref, glb_ref))):
        gates = jax.nn.sigmoid(gl_ref[...])
        for h in range(heads):
            y = (ycmp_ref[slot, h]
                 + gates[:, heads + h:heads + h + 1] * _normalized(accs_ref[slot, h]).T * zs_ref[h].astype(F32)
                 + gates[:, 2 * heads + h:2 * heads + h + 1] * _normalized(accw_ref[slot, h]).T
                 * zw_ref[h].astype(F32))
            o_ref[_tile_rows(tiles[slot]), h * HEAD_DIM:(h + 1) * HEAD_DIM] = y.astype(BF16)


def _nsa_attn(qz, gate_logits, kv, cmp_kv, table, table_c):
    batch = qz.shape[0]
    heads = NSA_HEADS_PER_GROUP
    groups = NSA_KV_GROUPS
    _, _, _, _, ov_t, onehot = _static_maps()

    def tile_of(n, second):
        return N_TILES - 1 - n if second else n

    def qz_specs(slot_group):
        return [pl.BlockSpec((None, heads, TILE, HEAD_DIM),
                             lambda g, b, n, second=second: (b, slot_group * groups + g, tile_of(n, second), 0))
                for second in (False, True)]

    def kv_spec(branch):
        return pl.BlockSpec((None, None, SEQ, HEAD_DIM), lambda g, b, n: (b, branch * groups + g, 0, 0))

    def cmp_spec(t):
        return pl.BlockSpec((None, None, None, N_CMP, HEAD_DIM), lambda g, b, n: (t, b, g, 0, 0))

    gl_specs = [pl.BlockSpec((None, TILE, HEAD_DIM), lambda g, b, n, second=second: (b, tile_of(n, second), g))
                for second in (False, True)]
    rc_specs = [pl.BlockSpec((heads, None, N_CMP, TILE), lambda g, b, n, second=second: (g, tile_of(n, second), 0, 0))
                for second in (False, True)]

    return pl.pallas_call(
        _nsa_kernel,
        grid=(groups, batch, N_PAIRS),
        in_specs=[
            *qz_specs(0), *qz_specs(1), *qz_specs(2), *qz_specs(3), *gl_specs,
            cmp_spec(0), cmp_spec(1), kv_spec(2), kv_spec(3), kv_spec(4), kv_spec(5),
            _resident((heads, N_TILES, TILE, TILE), lambda g, b, n: (g, TAB_S // N_TILES, 0, 0)),
            _resident((heads, TAB_W_ROWS, TILE, TILE), lambda g, b, n: (g, TAB_W // TAB_W_ROWS, 0, 0)),
            *rc_specs,
            _resident((N_SLC, N_CMP), lambda g, b, n: (0, 0)),
            _resident((SEQ, HEAD_DIM), lambda g, b, n: (0, 0)),
        ],
        out_specs=pl.BlockSpec((None, SEQ, heads * HEAD_DIM), lambda g, b, n: (b, 0, g)),
        out_shape=jax.ShapeDtypeStruct((batch, SEQ, ATT_WIDTH), BF16),
        scratch_shapes=[
            pltpu.VMEM((SEQ, 2 * HEAD_DIM), BF16),
            pltpu.VMEM((N_TILES, VT_ROWS, TILE), BF16),
            pltpu.VMEM((N_TILES, VT_ROWS, TILE), BF16),
            pltpu.VMEM((2, heads, 2 * HEAD_DIM, TILE), BF16),
            pltpu.VMEM((2, heads, HEAD_DIM, TILE), BF16),
            pltpu.VMEM((2, heads, TILE, HEAD_DIM), F32),
            pltpu.VMEM((heads, CAUSAL_STEPS, TILE, TILE), F32),
            pltpu.VMEM((2, heads, VT_ROWS, TILE), F32),
            pltpu.VMEM((2, heads, VT_ROWS, TILE), F32),
        ],
        compiler_params=pltpu.CompilerParams(dimension_semantics=("parallel", "parallel", "arbitrary"),
                                             vmem_limit_bytes=VMEM_LIMIT),
        name="nsa_attn",
    )(qz, qz, qz, qz, qz, qz, qz, qz, gate_logits, gate_logits, cmp_kv, cmp_kv, kv, kv, kv, kv,
      table, table, table_c, table_c, jnp.asarray(ov_t), jnp.asarray(onehot, dtype=BF16))


def kernel(x, norm_pre, norm_post, rel_table, w_in_a, w_out_a, kv_norm, w_kv, cmp_pos_k, cmp_pos_v,
           cmp_w1_k, cmp_w2_k, cmp_w1_v, cmp_w2_v, w_in_b, w_out_b):
    batch = x.shape[0]
    m = batch * SEQ
    heads = NSA_HEADS_PER_GROUP
    groups = NSA_KV_GROUPS
    x2d = x.reshape(m, D_MODEL)
    table, table_c = _bias_tables(rel_table)

    qkvz = _norm_matmul(x2d, norm_pre[0], w_in_a[0].astype(BF16), n_cols=4 * ATT_WIDTH,
                        silu_from_col=3 * ATT_WIDTH)
    att_a = _attn_a(qkvz, table)
    h1 = _out_proj(att_a.reshape(m, ATT_WIDTH), w_out_a[0].astype(BF16), x2d, norm_post[0])

    n_kv = w_kv.shape[1]
    kv, cmp_chunks = _norm_matmul(h1, kv_norm, w_kv.astype(BF16), n_cols=n_kv, silu_from_col=n_kv, chunks=True)
    pos = jnp.stack([cmp_pos_k, cmp_pos_v]).reshape(2, 2, 1, CMP_CHUNK)
    cmp_kv = _compress(cmp_chunks, pos, jnp.stack([cmp_w1_k, cmp_w1_v]).astype(BF16),
                       jnp.stack([cmp_w2_k, cmp_w2_v]).astype(BF16))

    w_b = w_in_b[0]
    n_main = (1 + NSA_BRANCHES) * ATT_WIDTH
    wg = w_b[:, n_main:].reshape(D_MODEL, NSA_BRANCHES, groups, heads).transpose(0, 2, 1, 3)
    wg = jnp.pad(wg.reshape(D_MODEL, groups, NSA_BRANCHES * heads),
                 ((0, 0), (0, 0), (0, HEAD_DIM - NSA_BRANCHES * heads))).reshape(D_MODEL, groups * HEAD_DIM)
    qz, gate_logits = _norm_matmul(h1, norm_pre[1], w_b.astype(BF16), wg.astype(BF16), n_cols=n_main,
                                   silu_from_col=ATT_WIDTH)
    att_b = _nsa_attn(qz, gate_logits.reshape(batch, SEQ, groups * HEAD_DIM), kv, cmp_kv, table, table_c)
    out = _out_proj(att_b.reshape(m, ATT_WIDTH), w_out_b[0].astype(BF16), h1, norm_post[1])
    return out.reshape(batch, SEQ, D_MODEL)
```

```python
import functools
import math

import numpy as np
import jax
import jax.numpy as jnp
from jax import lax
from jax.experimental import pallas as pl
from jax.experimental.pallas import tpu as pltpu

F32 = jnp.float32
BF16 = jnp.bfloat16

D_MODEL = 2048
SEQ = 2048
N_HEADS = 16
HEAD_DIM = 128
ATT_WIDTH = N_HEADS * HEAD_DIM
ATT_SCALE = HEAD_DIM ** -0.5
DIL_PATTERNS = ((128, 1), (512, 4), (2048, 16))
REL_BUCKETS = 32
REL_MAX_EXACT = 16
REL_MAX_DISTANCE = 2048
NSA_KV_GROUPS = 4
NSA_HEADS_PER_GROUP = 4
NSA_BRANCHES = 3
CMP_BLOCK = 32
CMP_STRIDE = 16
CMP_HIDDEN = 256
SLC_BLOCK = 64
SLC_TOP_N = 16
WIN_SIZE = 512
RMS_EPS = 1e-6
NEG = -1e30
FORCE_SCORE = 1e9
LOG2E = math.log2(math.e)

TILE = 256
N_TILES = SEQ // TILE
N_PAIRS = N_TILES // 2
CAUSAL_STEPS = N_TILES + 1
STEP_UNROLL = 3
N_CMP = SEQ // CMP_STRIDE
CMP_CHUNK = CMP_STRIDE * HEAD_DIM
N_SLC = SEQ // SLC_BLOCK
WIN_TILES = -(-(WIN_SIZE - 1) // TILE) + 1
HEADS_PER_STEP = 4
VT_ROWS = HEAD_DIM + 16
MXU_COLS = 256

TAB_A, TAB_S, TAB_W = 0, N_TILES, 2 * N_TILES
TAB_W_ROWS = 4
TAB_ROWS = 2 * N_TILES + TAB_W_ROWS
VMEM_LIMIT = 52 * 1024 * 1024


def _np_bucket(dist):
    n = np.maximum(dist, 0)
    nf = np.maximum(n, 1).astype(np.float32)
    log_b = REL_MAX_EXACT + (
        np.log(nf / np.float32(REL_MAX_EXACT)) / np.float32(math.log(REL_MAX_DISTANCE / REL_MAX_EXACT))
        * np.float32(REL_BUCKETS - REL_MAX_EXACT)).astype(np.int32)
    return np.where(n < REL_MAX_EXACT, n, np.minimum(log_b, REL_BUCKETS - 1)).astype(np.int32)


@functools.lru_cache(maxsize=None)
def _static_maps():
    ki = np.arange(TILE)[:, None]
    qi = np.arange(TILE)[None, :]
    dist = TILE * np.arange(N_TILES)[:, None, None] + qi[None] - ki[None]
    bk_t = _np_bucket(dist)
    mult = np.zeros(dist.shape, np.int64)
    for window, dil in DIL_PATTERNS:
        mult += ((dist % dil == 0) & (dist <= window)).astype(np.int64)
    base_a = np.where((dist >= 0) & (mult > 0), np.log2(np.maximum(mult, 1)), NEG)
    base_s = np.where(dist >= 0, 0.0, NEG)
    base_w = np.full((TAB_W_ROWS, TILE, TILE), NEG)
    base_w[:WIN_TILES] = np.where((dist >= 0) & (dist < WIN_SIZE), 0.0, NEG)[:WIN_TILES]
    base_t = np.concatenate([base_a, base_s, base_w]).astype(np.float32)
    ci = np.arange(N_CMP)[:, None]
    t = TILE * np.arange(N_TILES)[:, None, None] + qi[None]
    dist_c = t - (CMP_STRIDE * ci[None] + CMP_BLOCK - 1)
    bk_c = _np_bucket(dist_c)
    base_c = np.where((dist_c >= 0) & (ci[None] < N_CMP - 1), 0.0, NEG).astype(np.float32)
    cs = np.arange(N_CMP)[None, :] * CMP_STRIDE
    sj = np.arange(N_SLC)[:, None] * SLC_BLOCK
    ov_t = ((cs < sj + SLC_BLOCK) & (cs + CMP_BLOCK > sj) & (np.arange(N_CMP)[None, :] < N_CMP - 1))
    onehot = np.zeros((SEQ, HEAD_DIM), np.float32)
    onehot[np.arange(SEQ), np.arange(SEQ) // SLC_BLOCK] = 1.0
    return bk_t.astype(np.int32), base_t, bk_c.astype(np.int32), base_c, ov_t.astype(np.float32), onehot


def _bias_kernel(tab_ref, bkt_ref, baset_ref, bkc_ref, basec_ref, out_ref, outc_ref, *, present_t, present_c):
    h = pl.program_id(0)
    tv = [tab_ref[b, h] * LOG2E for b in range(REL_BUCKETS)]

    def lookup(bk, present):
        val = jnp.full(bk.shape, tv[present[0]], F32)
        for b in present[1:]:
            val = jnp.where(bk == b, tv[b], val)
        return val

    for d in range(N_TILES):
        g = lookup(bkt_ref[d], present_t[d])
        out_ref[TAB_A + d] = g + baset_ref[TAB_A + d]
        out_ref[TAB_S + d] = g + baset_ref[TAB_S + d]
        if d < TAB_W_ROWS:
            out_ref[TAB_W + d] = g + baset_ref[TAB_W + d]
        outc_ref[d] = lookup(bkc_ref[d], present_c[d]) + basec_ref[d]


def _bias_tables(rel_table):
    bk_t, base_t, bk_c, base_c, _, _ = _static_maps()
    present_t = tuple(tuple(int(b) for b in np.unique(bk_t[d])) for d in range(N_TILES))
    present_c = tuple(tuple(int(b) for b in np.unique(bk_c[d])) for d in range(N_TILES))
    return pl.pallas_call(
        functools.partial(_bias_kernel, present_t=present_t, present_c=present_c),
        grid=(N_HEADS,),
        in_specs=[
            pl.BlockSpec(memory_space=pltpu.SMEM),
            pl.BlockSpec((N_TILES, TILE, TILE), lambda h: (0, 0, 0)),
            pl.BlockSpec((TAB_ROWS, TILE, TILE), lambda h: (0, 0, 0)),
            pl.BlockSpec((N_TILES, N_CMP, TILE), lambda h: (0, 0, 0)),
            pl.BlockSpec((N_TILES, N_CMP, TILE), lambda h: (0, 0, 0)),
        ],
        out_specs=[
            pl.BlockSpec((None, TAB_ROWS, TILE, TILE), lambda h: (h, 0, 0, 0)),
            pl.BlockSpec((None, N_TILES, N_CMP, TILE), lambda h: (h, 0, 0, 0)),
        ],
        out_shape=[
            jax.ShapeDtypeStruct((N_HEADS, TAB_ROWS, TILE, TILE), F32),
            jax.ShapeDtypeStruct((N_HEADS, N_TILES, N_CMP, TILE), F32),
        ],
        compiler_params=pltpu.CompilerParams(dimension_semantics=("arbitrary",),
                                             vmem_limit_bytes=VMEM_LIMIT),
        name="bias_tables",
    )(rel_table, jnp.asarray(bk_t), jnp.asarray(base_t), jnp.asarray(bk_c), jnp.asarray(base_c))


def _silu(z):
    half = 0.5 * z
    return half + half * jnp.tanh(half)


TRANSPOSED_KINDS = ("t", "tq", "ts")


def _nm_kernel(*refs, tile_kinds, gate):
    refs = list(refs)
    x_ref, g_ref, w_ref = refs[:3]
    del refs[:3]
    wg_ref = refs.pop(0) if gate else None
    kinds_used = {k for kinds in tile_kinds for k in kinds}
    on_ref = refs.pop(0) if "n" in kinds_used else None
    ot_ref = refs.pop(0) if kinds_used & set(TRANSPOSED_KINDS) else None
    og_ref = refs.pop(0) if gate else None
    oc_ref = refs.pop(0) if "c" in kinds_used else None
    xn_ref = refs.pop(0)
    res_ref = refs.pop(0) if "c" in kinds_used else None
    j = pl.program_id(1)

    @pl.when(j == 0)
    def _():
        x = x_ref[...]
        ms = jnp.mean(x * x, axis=-1, keepdims=True)
        xn_ref[...] = (x * lax.rsqrt(ms + RMS_EPS) * g_ref[...]).astype(BF16)
        if gate:
            og_ref[...] = jnp.dot(xn_ref[...], wg_ref[...], preferred_element_type=F32)

    def product(kinds):
        pos = {"n": 0, "t": 0}
        for cc in range(len(kinds) // 2):
            res = jnp.dot(xn_ref[...], w_ref[:, cc * MXU_COLS:(cc + 1) * MXU_COLS], preferred_element_type=F32)
            for u in range(2):
                c = 2 * cc + u
                piece = res[:, u * HEAD_DIM:(u + 1) * HEAD_DIM]
                kind = kinds[c]
                if kind == "n":
                    on_ref[pos["n"]] = piece.astype(BF16)
                    pos["n"] += 1
                elif kind == "c":
                    res_ref[c] = piece
                else:
                    if kind == "tq":
                        piece = piece * (ATT_SCALE * LOG2E)
                    elif kind == "ts":
                        piece = _silu(piece)
                    ot_ref[pos["t"]] = piece.T.astype(BF16)
                    pos["t"] += 1
        if "c" in kinds:
            rows = res_ref.shape[1] // CMP_STRIDE
            for c in range(len(kinds)):
                for i in range(CMP_STRIDE):
                    oc_ref[c, :, i * HEAD_DIM:(i + 1) * HEAD_DIM] = res_ref[
                        c, pl.ds(i, rows, stride=CMP_STRIDE), :].astype(BF16)

    for kinds in sorted(set(tile_kinds)):
        tiles = [t for t, k in enumerate(tile_kinds) if k == kinds]
        cond = functools.reduce(lambda a, b: a | b, [j == t for t in tiles])
        pl.when(cond)(functools.partial(product, kinds))


def _norm_matmul(x2d, gain, w_bf, wg_bf=None, *, tile_kinds, tm=1024, tn=1024):
    m, d = x2d.shape
    slots = tn // HEAD_DIM
    batch = m // SEQ
    spb = SEQ // tm
    gate = wg_bf is not None
    assert all(len(k) == slots for k in tile_kinds)

    def family(match):
        counts = [sum(1 for k in kinds if match(k)) for kinds in tile_kinds]
        tiles = [t for t, c in enumerate(counts) if c]
        per_tile = counts[tiles[0]] if tiles else 0
        assert all(counts[t] == per_tile for t in tiles)

        def block(j):
            return jnp.minimum(sum(jnp.where(j > t, 1, 0) for t in tiles), len(tiles) - 1)
        return tiles, per_tile, block

    n_tiles, n_per, n_block = family(lambda k: k == "n")
    t_tiles, t_per, t_block = family(lambda k: k in TRANSPOSED_KINDS)
    c_tiles, c_per, _ = family(lambda k: k == "c")
    assert c_tiles in ([], [0]) and c_per in (0, slots)

    in_specs = [
        pl.BlockSpec((tm, d), lambda i, j: (i, 0)),
        pl.BlockSpec((1, d), lambda i, j: (0, 0)),
        pl.BlockSpec((d, tn), lambda i, j: (0, j)),
    ]
    args = [x2d, gain.reshape(1, d), w_bf]
    out_shape, out_specs = [], []
    scratch = [pltpu.VMEM((tm, d), BF16)]
    if gate:
        in_specs.append(pl.BlockSpec((d, wg_bf.shape[1]), lambda i, j: (0, 0)))
        args.append(wg_bf)
    if n_tiles:
        out_shape.append(jax.ShapeDtypeStruct((batch, n_per * len(n_tiles), SEQ, HEAD_DIM), BF16))
        out_specs.append(pl.BlockSpec((None, n_per, tm, HEAD_DIM),
                                      lambda i, j: (i // spb, n_block(j), i % spb, 0)))
    if t_tiles:
        out_shape.append(jax.ShapeDtypeStruct((batch, t_per * len(t_tiles), HEAD_DIM, SEQ), BF16))
        out_specs.append(pl.BlockSpec((None, t_per, HEAD_DIM, tm),
                                      lambda i, j: (i // spb, t_block(j), 0, i % spb)))
    if gate:
        out_shape.append(jax.ShapeDtypeStruct((m, wg_bf.shape[1]), F32))
        out_specs.append(pl.BlockSpec((tm, wg_bf.shape[1]), lambda i, j: (i, 0)))
    if c_tiles:
        out_shape.append(jax.ShapeDtypeStruct((batch, slots, N_CMP, CMP_CHUNK), BF16))
        out_specs.append(pl.BlockSpec((None, slots, tm // CMP_STRIDE, CMP_CHUNK),
                                      lambda i, j: (i // spb, 0, i % spb, 0)))
        scratch.append(pltpu.VMEM((slots, tm, HEAD_DIM), F32))
    return pl.pallas_call(
        functools.partial(_nm_kernel, tile_kinds=tuple(tuple(k) for k in tile_kinds), gate=gate),
        grid=(m // tm, len(tile_kinds)),
        in_specs=in_specs,
        out_specs=out_specs,
        out_shape=out_shape,
        scratch_shapes=scratch,
        compiler_params=pltpu.CompilerParams(dimension_semantics=("parallel", "arbitrary"),
                                             vmem_limit_bytes=VMEM_LIMIT),
        name="norm_matmul",
    )(*args)


def _op_kernel(a_ref, w_ref, r_ref, g_ref, o_ref, wbf_ref, *, parts):
    @pl.when(pl.program_id(0) == 0)
    def _():
        step = 256
        for r in range(w_ref.shape[0] // step):
            wbf_ref[r * step:(r + 1) * step, :] = w_ref[r * step:(r + 1) * step, :].astype(BF16)

    rows = a_ref.shape[0] // parts
    for r in range(parts):
        sl = slice(r * rows, (r + 1) * rows)
        y = jnp.dot(a_ref[sl, :], wbf_ref[...], preferred_element_type=F32)
        ms = jnp.mean(y * y, axis=-1, keepdims=True)
        o_ref[sl, :] = r_ref[sl, :] + y * lax.rsqrt(ms + RMS_EPS) * g_ref[...]


def _out_proj(a2d, w, resid2d, gain, *, tm=512, parts=2):
    m, k = a2d.shape
    n = w.shape[1]
    return pl.pallas_call(
        functools.partial(_op_kernel, parts=parts),
        grid=(m // tm,),
        in_specs=[
            pl.BlockSpec((tm, k), lambda i: (i, 0)),
            pl.BlockSpec((k, n), lambda i: (0, 0), pipeline_mode=pl.Buffered(1)),
            pl.BlockSpec((tm, n), lambda i: (i, 0)),
            pl.BlockSpec((1, n), lambda i: (0, 0)),
        ],
        out_specs=pl.BlockSpec((tm, n), lambda i: (i, 0)),
        out_shape=jax.ShapeDtypeStruct((m, n), F32),
        scratch_shapes=[pltpu.VMEM((k, n), BF16)],
        compiler_params=pltpu.CompilerParams(dimension_semantics=("arbitrary",),
                                             vmem_limit_bytes=VMEM_LIMIT),
        name="out_proj",
    )(a2d, w, resid2d, gain.reshape(1, n))


def _value_tiles(src_ref, dst_ref):
    for c in range(N_TILES):
        dst_ref[c, 0:HEAD_DIM, :] = src_ref[:, c * TILE:(c + 1) * TILE]
        dst_ref[c, HEAD_DIM:VT_ROWS, :] = jnp.ones((VT_ROWS - HEAD_DIM, TILE), BF16)


def _pair_tiles(n):
    return (n, N_TILES - 1 - n)


def _tile_rows(idx):
    return pl.ds(pl.multiple_of(idx * TILE, TILE), TILE)


def _normalized(acc):
    return acc[0:HEAD_DIM, :] / acc[HEAD_DIM:HEAD_DIM + 1, :]


def _pair_attention(steps, step_fn, heads, q_ref, k_fn, vt_fn, bias_fn, s_ref, acc_ref, unrolled=False):
    def scores(t, ms):
        slot, j, b = step_fn(t)
        first = slot == 0
        m0, m1 = list(ms[0]), list(ms[1])
        for h in range(heads):
            s = jnp.dot(k_fn(h, j), q_ref[slot, h], preferred_element_type=F32) + bias_fn(h, b)
            s_ref[h, t] = s
            cm = jnp.max(s, axis=0, keepdims=True)
            m0[h] = jnp.where(first, jnp.maximum(m0[h], cm), m0[h])
            m1[h] = jnp.where(first, m1[h], jnp.maximum(m1[h], cm))
        return tuple(m0), tuple(m1)

    def loop(body, init):
        if unrolled:
            for t in range(steps):
                init = body(t, init)
            return init
        return lax.fori_loop(0, steps, body, init, unroll=STEP_UNROLL)

    neg = (jnp.full((1, TILE), NEG, F32),) * heads
    m0, m1 = loop(scores, (neg, neg))

    for slot in range(2):
        for h in range(heads):
            acc_ref[slot, h] = jnp.zeros((VT_ROWS, TILE), F32)

    def weighted(t, carry):
        slot, j, _ = step_fn(t)
        first = slot == 0
        for h in range(heads):
            p = jnp.exp2((s_ref[h, t] - jnp.where(first, m0[h], m1[h])).astype(BF16))
            acc_ref[slot, h] += jnp.dot(vt_fn(h, j), p, preferred_element_type=F32)
        return carry

    loop(weighted, 0)


def _causal_step(n):
    def step(t):
        first = t <= n
        return (jnp.where(first, 0, 1), jnp.where(first, t, t - n - 1), jnp.where(first, n - t, N_TILES - t))
    return step


def _window_step(n):
    def step(t):
        slot, d = divmod(t, WIN_TILES)
        j = _pair_tiles(n)[slot] - d
        return slot, jnp.maximum(j, 0), jnp.where(j >= 0, d, TAB_W_ROWS - 1)
    return step


def _attn_a_kernel(qa_ref, qb_ref, k_ref, v_ref, za_ref, zb_ref, r_ref, o_ref,
                   vt_ref, qt_ref, s_ref, acc_ref):
    n = pl.program_id(2)
    heads = HEADS_PER_STEP

    @pl.when(n == 0)
    def _():
        for h in range(heads):
            _value_tiles(v_ref.at[h], vt_ref.at[h])

    for slot, q_ref in enumerate((qa_ref, qb_ref)):
        for h in range(heads):
            qt_ref[slot, h] = q_ref[h]

    _pair_attention(CAUSAL_STEPS, _causal_step(n), heads, qt_ref,
                    lambda h, j: k_ref[h, _tile_rows(j), :], lambda h, j: vt_ref[h, j],
                    lambda h, delta: r_ref[h, delta], s_ref, acc_ref)
    for slot, (z_ref, idx) in enumerate(zip((za_ref, zb_ref), _pair_tiles(n))):
        for h in range(heads):
            y = (_normalized(acc_ref[slot, h]) * z_ref[h].astype(F32)).T
            o_ref[_tile_rows(idx), h * HEAD_DIM:(h + 1) * HEAD_DIM] = y.astype(BF16)


def _resident(block_shape, index_map):
    return pl.BlockSpec(block_shape, index_map, pipeline_mode=pl.Buffered(1))


def _attn_a(k_nat, qvz_t, table):
    batch = k_nat.shape[0]
    heads = HEADS_PER_STEP
    hgroups = N_HEADS // heads

    def tile_slot(s, second):
        def index(hg, b, n):
            return (b, s * hgroups + hg, 0, N_TILES - 1 - n if second else n)
        return pl.BlockSpec((None, heads, HEAD_DIM, TILE), index)

    return pl.pallas_call(
        _attn_a_kernel,
        grid=(hgroups, batch, N_PAIRS),
        in_specs=[tile_slot(0, False), tile_slot(0, True),
                  pl.BlockSpec((None, heads, SEQ, HEAD_DIM), lambda hg, b, n: (b, hg, 0, 0)),
                  pl.BlockSpec((None, heads, HEAD_DIM, SEQ), lambda hg, b, n: (b, hgroups + hg, 0, 0)),
                  tile_slot(2, False), tile_slot(2, True),
                  _resident((heads, N_TILES, TILE, TILE), lambda hg, b, n: (hg, TAB_A // N_TILES, 0, 0))],
        out_specs=pl.BlockSpec((None, SEQ, heads * HEAD_DIM), lambda hg, b, n: (b, 0, hg)),
        out_shape=jax.ShapeDtypeStruct((batch, SEQ, ATT_WIDTH), BF16),
        scratch_shapes=[
            pltpu.VMEM((heads, N_TILES, VT_ROWS, TILE), BF16),
            pltpu.VMEM((2, heads, HEAD_DIM, TILE), BF16),
            pltpu.VMEM((heads, CAUSAL_STEPS, TILE, TILE), F32),
            pltpu.VMEM((2, heads, VT_ROWS, TILE), F32),
        ],
        compiler_params=pltpu.CompilerParams(dimension_semantics=("parallel", "parallel", "arbitrary"),
                                             vmem_limit_bytes=VMEM_LIMIT),
        name="dilated_attn",
    )(qvz_t, qvz_t, k_nat, qvz_t, qvz_t, qvz_t, table)


def _cmp_kernel(c_ref, pos_ref, w1_ref, w2_ref, o_ref):
    c = c_ref[...].astype(F32)
    x_lo = (c + pos_ref[0]).astype(BF16)
    x_hi = (c + pos_ref[1]).astype(BF16)
    a = jnp.dot(x_lo, w1_ref[0:CMP_CHUNK, :], preferred_element_type=F32)
    bm = jnp.dot(x_hi, w1_ref[CMP_CHUNK:2 * CMP_CHUNK, :], preferred_element_type=F32)
    hid = jax.nn.gelu(a + pltpu.roll(bm, N_CMP - 1, 0))
    res = jnp.dot(hid.astype(BF16), w2_ref[...], preferred_element_type=F32)
    o_ref[...] = jnp.where(pl.program_id(0) == 1, res.T, res).astype(BF16)


def _compress(chunks, pos, w1_bf, w2_bf):
    batch = chunks.shape[0]
    return pl.pallas_call(
        _cmp_kernel,
        grid=(2, batch, NSA_KV_GROUPS),
        in_specs=[
            pl.BlockSpec((None, None, N_CMP, CMP_CHUNK), lambda t, b, g: (b, t * NSA_KV_GROUPS + g, 0, 0)),
            pl.BlockSpec((None, 2, 1, CMP_CHUNK), lambda t, b, g: (t, 0, 0, 0)),
            pl.BlockSpec((None, 2 * CMP_CHUNK, CMP_HIDDEN), lambda t, b, g: (t, 0, 0)),
            pl.BlockSpec((None, CMP_HIDDEN, HEAD_DIM), lambda t, b, g: (t, 0, 0)),
        ],
        out_specs=pl.BlockSpec((None, None, None, N_CMP, HEAD_DIM), lambda t, b, g: (t, b, g, 0, 0)),
        out_shape=jax.ShapeDtypeStruct((2, batch, NSA_KV_GROUPS, N_CMP, HEAD_DIM), BF16),
        compiler_params=pltpu.CompilerParams(dimension_semantics=("arbitrary", "arbitrary", "arbitrary"),
                                             vmem_limit_bytes=VMEM_LIMIT),
        name="compress_kv",
    )(chunks, pos, w1_bf, w2_bf)


def _nsa_kernel(qa_ref, qb_ref, zca_ref, zcb_ref, zsa_ref, zsb_ref, zwa_ref, zwb_ref, gla_ref, glb_ref,
                kc_ref, vct_ref, ks_ref, vs_ref, kw_ref, vw_ref, rs_ref, rw_ref, rca_ref, rcb_ref,
                ovt_ref, onehot_ref, o_ref,
                kaug_ref, vst_ref, vwt_ref, qa_sc, qw_sc, gates_ref, ycmp_ref, s_ref, accs_ref, accw_ref):
    n = pl.program_id(2)
    heads = NSA_HEADS_PER_GROUP
    tiles = _pair_tiles(n)

    @pl.when(n == 0)
    def _():
        kaug_ref[:, 0:HEAD_DIM] = ks_ref[...]
        kaug_ref[:, HEAD_DIM:2 * HEAD_DIM] = onehot_ref[...]
        _value_tiles(vs_ref, vst_ref)
        _value_tiles(vw_ref, vwt_ref)

    kc = kc_ref[...]
    vct = vct_ref[...]
    for slot, (q_ref, rc_ref, zc_ref, gl_ref) in enumerate(
            ((qa_ref, rca_ref, zca_ref, gla_ref), (qb_ref, rcb_ref, zcb_ref, glb_ref))):
        psum = jnp.zeros((N_CMP, TILE), F32)
        gates_ref[slot] = jax.nn.sigmoid(gl_ref[...].T)
        for h in range(heads):
            qt = q_ref[h]
            qw_sc[slot, h] = qt
            qa_sc[slot, h, 0:HEAD_DIM, :] = qt
            bias = rc_ref[h]
            s = jnp.dot(kc, qt, preferred_element_type=F32) + bias
            m = jnp.max(s, axis=0, keepdims=True)
            p = jnp.where(bias > 0.5 * NEG, jnp.exp2(s - m), 0.0)
            p = p / jnp.maximum(jnp.sum(p, axis=0, keepdims=True), 1e-30)
            psum = psum + p
            o_cmp = jnp.dot(vct, p.astype(BF16), preferred_element_type=F32)
            ycmp_ref[slot, h] = gates_ref[slot, h:h + 1, :] * o_cmp * zc_ref[h].astype(F32)

        imp = jnp.dot(ovt_ref[...], psum, precision=lax.Precision.HIGHEST, preferred_element_type=F32)
        blk = lax.broadcasted_iota(jnp.int32, (N_SLC, TILE), 0)
        tq = tiles[slot] * TILE + lax.broadcasted_iota(jnp.int32, (N_SLC, TILE), 1)
        cur = lax.shift_right_logical(tq, int(math.log2(SLC_BLOCK)))
        forced = (blk == 0) | (blk == cur) | (blk == cur - 1)
        score = jnp.where(forced, FORCE_SCORE, jnp.where(blk > cur, -FORCE_SCORE, imp))
        cnt = jnp.zeros((N_SLC, TILE), jnp.int32)
        for jj in range(N_SLC):
            sj = score[jj:jj + 1, :]
            beats = (sj > score) | ((sj == score) & (blk > jj))
            cnt = cnt + beats.astype(jnp.int32)
        mask = jnp.where(cnt < SLC_TOP_N, 0.0, NEG)
        mask = jnp.concatenate([mask, jnp.zeros((HEAD_DIM - N_SLC, TILE), F32)], axis=0).astype(BF16)
        for h in range(heads):
            qa_sc[slot, h, HEAD_DIM:2 * HEAD_DIM, :] = mask

    _pair_attention(CAUSAL_STEPS, _causal_step(n), heads, qa_sc,
                    lambda h, j: kaug_ref[_tile_rows(j), :], lambda h, j: vst_ref[j],
                    lambda h, delta: rs_ref[h, delta], s_ref, accs_ref)
    _pair_attention(2 * WIN_TILES, _window_step(n), heads, qw_sc,
                    lambda h, j: kw_ref[_tile_rows(j), :], lambda h, j: vwt_ref[j],
                    lambda h, b: rw_ref[h, b], s_ref, accw_ref, unrolled=True)

    for slot, (zs_ref, zw_ref) in enumerate(((zsa_ref, zwa_ref), (zsb_ref, zwb_ref))):
        for h in range(heads):
            y = (ycmp_ref[slot, h]
                 + gates_ref[slot, heads + h:heads + h + 1, :] * _normalized(accs_ref[slot, h])
                 * zs_ref[h].astype(F32)
                 + gates_ref[slot, 2 * heads + h:2 * heads + h + 1, :] * _normalized(accw_ref[slot, h])
                 * zw_ref[h].astype(F32))
            o_ref[_tile_rows(tiles[slot]), h * HEAD_DIM:(h + 1) * HEAD_DIM] = y.T.astype(BF16)


def _nsa_attn(qz_t, gate_logits, kv_nat, kv_t, cmp_kv, table, table_c):
    batch = qz_t.shape[0]
    heads = NSA_HEADS_PER_GROUP
    groups = NSA_KV_GROUPS
    _, _, _, _, ov_t, onehot = _static_maps()

    def tile_of(n, second):
        return N_TILES - 1 - n if second else n

    def qz_specs(slot_group):
        return [pl.BlockSpec((None, heads, HEAD_DIM, TILE),
                             lambda g, b, n, second=second: (b, slot_group * groups + g, 0, tile_of(n, second)))
                for second in (False, True)]

    def key_spec(branch):
        return pl.BlockSpec((None, None, SEQ, HEAD_DIM), lambda g, b, n: (b, branch * groups + g, 0, 0))

    def value_spec(branch):
        return pl.BlockSpec((None, None, HEAD_DIM, SEQ), lambda g, b, n: (b, branch * groups + g, 0, 0))

    def cmp_spec(t):
        return pl.BlockSpec((None, None, None, N_CMP, HEAD_DIM), lambda g, b, n: (t, b, g, 0, 0))

    gl_specs = [pl.BlockSpec((None, TILE, HEAD_DIM), lambda g, b, n, second=second: (b, tile_of(n, second), g))
                for second in (False, True)]
    rc_specs = [pl.BlockSpec((heads, None, N_CMP, TILE), lambda g, b, n, second=second: (g, tile_of(n, second), 0, 0))
                for second in (False, True)]

    return pl.pallas_call(
        _nsa_kernel,
        grid=(groups, batch, N_PAIRS),
        in_specs=[
            *qz_specs(0), *qz_specs(1), *qz_specs(2), *qz_specs(3), *gl_specs,
            cmp_spec(0), cmp_spec(1), key_spec(0), value_spec(0), key_spec(1), value_spec(1),
            _resident((heads, N_TILES, TILE, TILE), lambda g, b, n: (g, TAB_S // N_TILES, 0, 0)),
            _resident((heads, TAB_W_ROWS, TILE, TILE), lambda g, b, n: (g, TAB_W // TAB_W_ROWS, 0, 0)),
            *rc_specs,
            _resident((N_SLC, N_CMP), lambda g, b, n: (0, 0)),
            _resident((SEQ, HEAD_DIM), lambda g, b, n: (0, 0)),
        ],
        out_specs=pl.BlockSpec((None, SEQ, heads * HEAD_DIM), lambda g, b, n: (b, 0, g)),
        out_shape=jax.ShapeDtypeStruct((batch, SEQ, ATT_WIDTH), BF16),
        scratch_shapes=[
            pltpu.VMEM((SEQ, 2 * HEAD_DIM), BF16),
            pltpu.VMEM((N_TILES, VT_ROWS, TILE), BF16),
            pltpu.VMEM((N_TILES, VT_ROWS, TILE), BF16),
            pltpu.VMEM((2, heads, 2 * HEAD_DIM, TILE), BF16),
            pltpu.VMEM((2, heads, HEAD_DIM, TILE), BF16),
            pltpu.VMEM((2, HEAD_DIM, TILE), F32),
            pltpu.VMEM((2, heads, HEAD_DIM, TILE), F32),
            pltpu.VMEM((heads, CAUSAL_STEPS, TILE, TILE), F32),
            pltpu.VMEM((2, heads, VT_ROWS, TILE), F32),
            pltpu.VMEM((2, heads, VT_ROWS, TILE), F32),
        ],
        compiler_params=pltpu.CompilerParams(dimension_semantics=("parallel", "parallel", "arbitrary"),
                                             vmem_limit_bytes=VMEM_LIMIT),
        name="nsa_attn",
    )(qz_t, qz_t, qz_t, qz_t, qz_t, qz_t, qz_t, qz_t, gate_logits, gate_logits, cmp_kv, cmp_kv,
      kv_nat, kv_t, kv_nat, kv_t, table, table, table_c, table_c,
      jnp.asarray(ov_t), jnp.asarray(onehot, dtype=BF16))


def kernel(x, norm_pre, norm_post, rel_table, w_in_a, w_out_a, kv_norm, w_kv, cmp_pos_k, cmp_pos_v,
           cmp_w1_k, cmp_w2_k, cmp_w1_v, cmp_w2_v, w_in_b, w_out_b):
    batch = x.shape[0]
    m = batch * SEQ
    heads = NSA_HEADS_PER_GROUP
    groups = NSA_KV_GROUPS
    x2d = x.reshape(m, D_MODEL)
    table, table_c = _bias_tables(rel_table)

    tile_slots = 8
    tiles_per_part = ATT_WIDTH // (tile_slots * HEAD_DIM)

    kinds_a = [(kind,) * tile_slots for kind in ("tq", "n", "t", "ts") for _ in range(tiles_per_part)]
    k_nat, qvz_t = _norm_matmul(x2d, norm_pre[0], w_in_a[0].astype(BF16), tile_kinds=kinds_a)
    att_a = _attn_a(k_nat, qvz_t, table)
    h1 = _out_proj(att_a.reshape(m, ATT_WIDTH), w_out_a[0], x2d, norm_post[0])

    kinds_kv = [("c",) * tile_slots] + [("n",) * groups + ("t",) * groups] * 2
    kv_nat, kv_t, cmp_chunks = _norm_matmul(h1, kv_norm, w_kv.astype(BF16), tile_kinds=kinds_kv)
    pos = jnp.stack([cmp_pos_k, cmp_pos_v]).reshape(2, 2, 1, CMP_CHUNK)
    cmp_kv = _compress(cmp_chunks, pos, jnp.stack([cmp_w1_k, cmp_w1_v]).astype(BF16),
                       jnp.stack([cmp_w2_k, cmp_w2_v]).astype(BF16))

    w_b = w_in_b[0]
    n_main = (1 + NSA_BRANCHES) * ATT_WIDTH
    wg = w_b[:, n_main:].reshape(D_MODEL, NSA_BRANCHES, groups, heads).transpose(0, 2, 1, 3)
    wg = jnp.pad(wg.reshape(D_MODEL, groups, NSA_BRANCHES * heads),
                 ((0, 0), (0, 0), (0, HEAD_DIM - NSA_BRANCHES * heads))).reshape(D_MODEL, groups * HEAD_DIM)
    kinds_b = [(kind,) * tile_slots for kind in ("tq",) + ("ts",) * NSA_BRANCHES for _ in range(tiles_per_part)]
    qz_t, gate_logits = _norm_matmul(h1, norm_pre[1], w_b.astype(BF16), wg.astype(BF16), tile_kinds=kinds_b)
    att_b = _nsa_attn(qz_t, gate_logits.reshape(batch, SEQ, groups * HEAD_DIM), kv_nat, kv_t, cmp_kv,
                      table, table_c)
    out = _out_proj(att_b.reshape(m, ATT_WIDTH), w_out_b[0], h1, norm_post[1])
    return out.reshape(batch, SEQ, D_MODEL)
```

```python
import functools
import math

import numpy as np
import jax
import jax.numpy as jnp
from jax import lax
from jax.experimental import pallas as pl
from jax.experimental.pallas import tpu as pltpu

F32 = jnp.float32
BF16 = jnp.bfloat16

D_MODEL = 2048
SEQ = 2048
N_HEADS = 16
HEAD_DIM = 128
ATT_WIDTH = N_HEADS * HEAD_DIM
ATT_SCALE = HEAD_DIM ** -0.5
DIL_PATTERNS = ((128, 1), (512, 4), (2048, 16))
REL_BUCKETS = 32
REL_MAX_EXACT = 16
REL_MAX_DISTANCE = 2048
NSA_KV_GROUPS = 4
NSA_HEADS_PER_GROUP = 4
NSA_BRANCHES = 3
CMP_BLOCK = 32
CMP_STRIDE = 16
CMP_HIDDEN = 256
SLC_BLOCK = 64
SLC_TOP_N = 16
WIN_SIZE = 512
RMS_EPS = 1e-6
NEG = -1e30
FORCE_SCORE = 1e9
LOG2E = math.log2(math.e)

TILE = 256
N_TILES = SEQ // TILE
N_PAIRS = N_TILES // 2
CAUSAL_STEPS = N_TILES + 1
STEP_UNROLL = 3
N_CMP = SEQ // CMP_STRIDE
CMP_CHUNK = CMP_STRIDE * HEAD_DIM
N_SLC = SEQ // SLC_BLOCK
WIN_TILES = -(-(WIN_SIZE - 1) // TILE) + 1
HEADS_PER_STEP = 4
VT_ROWS = HEAD_DIM + 16
MXU_COLS = 256

TAB_A, TAB_S, TAB_W = 0, N_TILES, 2 * N_TILES
TAB_W_ROWS = 4
TAB_ROWS = 2 * N_TILES + TAB_W_ROWS
VMEM_LIMIT = 52 * 1024 * 1024


def _np_bucket(dist):
    n = np.maximum(dist, 0)
    nf = np.maximum(n, 1).astype(np.float32)
    log_b = REL_MAX_EXACT + (
        np.log(nf / np.float32(REL_MAX_EXACT)) / np.float32(math.log(REL_MAX_DISTANCE / REL_MAX_EXACT))
        * np.float32(REL_BUCKETS - REL_MAX_EXACT)).astype(np.int32)
    return np.where(n < REL_MAX_EXACT, n, np.minimum(log_b, REL_BUCKETS - 1)).astype(np.int32)


@functools.lru_cache(maxsize=None)
def _static_maps():
    ki = np.arange(TILE)[:, None]
    qi = np.arange(TILE)[None, :]
    dist = TILE * np.arange(N_TILES)[:, None, None] + qi[None] - ki[None]
    bk_t = _np_bucket(dist)
    mult = np.zeros(dist.shape, np.int64)
    for window, dil in DIL_PATTERNS:
        mult += ((dist % dil == 0) & (dist <= window)).astype(np.int64)
    base_a = np.where((dist >= 0) & (mult > 0), np.log2(np.maximum(mult, 1)), NEG)
    base_s = np.where(dist >= 0, 0.0, NEG)
    base_w = np.full((TAB_W_ROWS, TILE, TILE), NEG)
    base_w[:WIN_TILES] = np.where((dist >= 0) & (dist < WIN_SIZE), 0.0, NEG)[:WIN_TILES]
    base_t = np.concatenate([base_a, base_s, base_w]).astype(np.float32)
    ci = np.arange(N_CMP)[:, None]
    t = TILE * np.arange(N_TILES)[:, None, None] + qi[None]
    dist_c = t - (CMP_STRIDE * ci[None] + CMP_BLOCK - 1)
    bk_c = _np_bucket(dist_c)
    base_c = np.where((dist_c >= 0) & (ci[None] < N_CMP - 1), 0.0, NEG).astype(np.float32)
    cs = np.arange(N_CMP)[None, :] * CMP_STRIDE
    sj = np.arange(N_SLC)[:, None] * SLC_BLOCK
    ov_t = ((cs < sj + SLC_BLOCK) & (cs + CMP_BLOCK > sj) & (np.arange(N_CMP)[None, :] < N_CMP - 1))
    onehot = np.zeros((SEQ, HEAD_DIM), np.float32)
    onehot[np.arange(SEQ), np.arange(SEQ) // SLC_BLOCK] = 1.0
    return bk_t.astype(np.int32), base_t, bk_c.astype(np.int32), base_c, ov_t.astype(np.float32), onehot


def _bias_kernel(tab_ref, bkt_ref, baset_ref, bkc_ref, basec_ref, out_ref, outc_ref, *, present_t, present_c):
    h = pl.program_id(0)
    tv = [tab_ref[b, h] * LOG2E for b in range(REL_BUCKETS)]

    def lookup(bk, present):
        val = jnp.full(bk.shape, tv[present[0]], F32)
        for b in present[1:]:
            val = jnp.where(bk == b, tv[b], val)
        return val

    for d in range(N_TILES):
        g = lookup(bkt_ref[d], present_t[d])
        out_ref[TAB_A + d] = g + baset_ref[TAB_A + d]
        out_ref[TAB_S + d] = g + baset_ref[TAB_S + d]
        if d < TAB_W_ROWS:
            out_ref[TAB_W + d] = g + baset_ref[TAB_W + d]
        outc_ref[d] = lookup(bkc_ref[d], present_c[d]) + basec_ref[d]


def _bias_tables(rel_table):
    bk_t, base_t, bk_c, base_c, _, _ = _static_maps()
    present_t = tuple(tuple(int(b) for b in np.unique(bk_t[d])) for d in range(N_TILES))
    present_c = tuple(tuple(int(b) for b in np.unique(bk_c[d])) for d in range(N_TILES))
    return pl.pallas_call(
        functools.partial(_bias_kernel, present_t=present_t, present_c=present_c),
        grid=(N_HEADS,),
        in_specs=[
            pl.BlockSpec(memory_space=pltpu.SMEM),
            pl.BlockSpec((N_TILES, TILE, TILE), lambda h: (0, 0, 0)),
            pl.BlockSpec((TAB_ROWS, TILE, TILE), lambda h: (0, 0, 0)),
            pl.BlockSpec((N_TILES, N_CMP, TILE), lambda h: (0, 0, 0)),
            pl.BlockSpec((N_TILES, N_CMP, TILE), lambda h: (0, 0, 0)),
        ],
        out_specs=[
            pl.BlockSpec((None, TAB_ROWS, TILE, TILE), lambda h: (h, 0, 0, 0)),
            pl.BlockSpec((None, N_TILES, N_CMP, TILE), lambda h: (h, 0, 0, 0)),
        ],
        out_shape=[
            jax.ShapeDtypeStruct((N_HEADS, TAB_ROWS, TILE, TILE), F32),
            jax.ShapeDtypeStruct((N_HEADS, N_TILES, N_CMP, TILE), F32),
        ],
        compiler_params=pltpu.CompilerParams(dimension_semantics=("arbitrary",),
                                             vmem_limit_bytes=VMEM_LIMIT),
        name="bias_tables",
    )(rel_table, jnp.asarray(bk_t), jnp.asarray(base_t), jnp.asarray(bk_c), jnp.asarray(base_c))


def _silu(z):
    half = 0.5 * z
    return half + half * jnp.tanh(half)


TRANSPOSED_KINDS = ("t", "tq", "ts")


def _nm_kernel(*refs, tile_kinds, gate):
    refs = list(refs)
    x_ref, g_ref, w_ref = refs[:3]
    del refs[:3]
    wg_ref = refs.pop(0) if gate else None
    kinds_used = {k for kinds in tile_kinds for k in kinds}
    on_ref = refs.pop(0) if "n" in kinds_used else None
    ot_ref = refs.pop(0) if kinds_used & set(TRANSPOSED_KINDS) else None
    og_ref = refs.pop(0) if gate else None
    oc_ref = refs.pop(0) if "c" in kinds_used else None
    xn_ref = refs.pop(0)
    res_ref = refs.pop(0) if "c" in kinds_used else None
    j = pl.program_id(1)

    @pl.when(j == 0)
    def _():
        x = x_ref[...]
        ms = jnp.mean(x * x, axis=-1, keepdims=True)
        xn_ref[...] = (x * lax.rsqrt(ms + RMS_EPS) * g_ref[...]).astype(BF16)
        if gate:
            og_ref[...] = jnp.dot(xn_ref[...], wg_ref[...], preferred_element_type=F32)

    def product(kinds):
        pos = {"n": 0, "t": 0}
        for cc in range(len(kinds) // 2):
            w_cols = w_ref[:, cc * MXU_COLS:(cc + 1) * MXU_COLS].astype(BF16)
            res = jnp.dot(xn_ref[...], w_cols, preferred_element_type=F32)
            for u in range(2):
                c = 2 * cc + u
                piece = res[:, u * HEAD_DIM:(u + 1) * HEAD_DIM]
                kind = kinds[c]
                if kind == "n":
                    on_ref[pos["n"]] = piece.astype(BF16)
                    pos["n"] += 1
                elif kind == "c":
                    res_ref[c] = piece
                else:
                    if kind == "tq":
                        piece = piece * (ATT_SCALE * LOG2E)
                    elif kind == "ts":
                        piece = _silu(piece)
                    ot_ref[pos["t"]] = piece.T.astype(BF16)
                    pos["t"] += 1
        if "c" in kinds:
            rows = res_ref.shape[1] // CMP_STRIDE
            for c in range(len(kinds)):
                for i in range(CMP_STRIDE):
                    oc_ref[c, :, i * HEAD_DIM:(i + 1) * HEAD_DIM] = res_ref[
                        c, pl.ds(i, rows, stride=CMP_STRIDE), :].astype(BF16)

    for kinds in sorted(set(tile_kinds)):
        tiles = [t for t, k in enumerate(tile_kinds) if k == kinds]
        cond = functools.reduce(lambda a, b: a | b, [j == t for t in tiles])
        pl.when(cond)(functools.partial(product, kinds))


def _norm_matmul(x2d, gain, w_bf, wg_bf=None, *, tile_kinds, tm=1024, tn=1024):
    m, d = x2d.shape
    slots = tn // HEAD_DIM
    batch = m // SEQ
    spb = SEQ // tm
    gate = wg_bf is not None
    assert all(len(k) == slots for k in tile_kinds)

    def family(match):
        counts = [sum(1 for k in kinds if match(k)) for kinds in tile_kinds]
        tiles = [t for t, c in enumerate(counts) if c]
        per_tile = counts[tiles[0]] if tiles else 0
        assert all(counts[t] == per_tile for t in tiles)

        def block(j):
            return jnp.minimum(sum(jnp.where(j > t, 1, 0) for t in tiles), len(tiles) - 1)
        return tiles, per_tile, block

    n_tiles, n_per, n_block = family(lambda k: k == "n")
    t_tiles, t_per, t_block = family(lambda k: k in TRANSPOSED_KINDS)
    c_tiles, c_per, _ = family(lambda k: k == "c")
    assert c_tiles in ([], [0]) and c_per in (0, slots)

    in_specs = [
        pl.BlockSpec((tm, d), lambda i, j: (i, 0)),
        pl.BlockSpec((1, d), lambda i, j: (0, 0)),
        pl.BlockSpec((d, tn), lambda i, j: (0, j)),
    ]
    args = [x2d, gain.reshape(1, d), w_bf]
    out_shape, out_specs = [], []
    scratch = [pltpu.VMEM((tm, d), BF16)]
    if gate:
        in_specs.append(pl.BlockSpec((d, wg_bf.shape[1]), lambda i, j: (0, 0)))
        args.append(wg_bf)
    if n_tiles:
        out_shape.append(jax.ShapeDtypeStruct((batch, n_per * len(n_tiles), SEQ, HEAD_DIM), BF16))
        out_specs.append(pl.BlockSpec((None, n_per, tm, HEAD_DIM),
                                      lambda i, j: (i // spb, n_block(j), i % spb, 0)))
    if t_tiles:
        out_shape.append(jax.ShapeDtypeStruct((batch, t_per * len(t_tiles), HEAD_DIM, SEQ), BF16))
        out_specs.append(pl.BlockSpec((None, t_per, HEAD_DIM, tm),
                                      lambda i, j: (i // spb, t_block(j), 0, i % spb)))
    if gate:
        out_shape.append(jax.ShapeDtypeStruct((m, wg_bf.shape[1]), F32))
        out_specs.append(pl.BlockSpec((tm, wg_bf.shape[1]), lambda i, j: (i, 0)))
    if c_tiles:
        out_shape.append(jax.ShapeDtypeStruct((batch, slots, N_CMP, CMP_CHUNK), BF16))
        out_specs.append(pl.BlockSpec((None, slots, tm // CMP_STRIDE, CMP_CHUNK),
                                      lambda i, j: (i // spb, 0, i % spb, 0)))
        scratch.append(pltpu.VMEM((slots, tm, HEAD_DIM), F32))
    return pl.pallas_call(
        functools.partial(_nm_kernel, tile_kinds=tuple(tuple(k) for k in tile_kinds), gate=gate),
        grid=(m // tm, len(tile_kinds)),
        in_specs=in_specs,
        out_specs=out_specs,
        out_shape=out_shape,
        scratch_shapes=scratch,
        compiler_params=pltpu.CompilerParams(dimension_semantics=("parallel", "arbitrary"),
                                             vmem_limit_bytes=VMEM_LIMIT),
        name="norm_matmul",
    )(*args)


def _op_kernel(a_ref, w_ref, r_ref, g_ref, o_ref, wbf_ref, *, parts):
    @pl.when(pl.program_id(0) == 0)
    def _():
        step = 256
        for r in range(w_ref.shape[0] // step):
            wbf_ref[r * step:(r + 1) * step, :] = w_ref[r * step:(r + 1) * step, :].astype(BF16)

    rows = a_ref.shape[0] // parts
    for r in range(parts):
        sl = slice(r * rows, (r + 1) * rows)
        y = jnp.dot(a_ref[sl, :], wbf_ref[...], preferred_element_type=F32)
        ms = jnp.mean(y * y, axis=-1, keepdims=True)
        o_ref[sl, :] = r_ref[sl, :] + y * lax.rsqrt(ms + RMS_EPS) * g_ref[...]


def _out_proj(a2d, w, resid2d, gain, *, tm=512, parts=2):
    m, k = a2d.shape
    n = w.shape[1]
    return pl.pallas_call(
        functools.partial(_op_kernel, parts=parts),
        grid=(m // tm,),
        in_specs=[
            pl.BlockSpec((tm, k), lambda i: (i, 0)),
            pl.BlockSpec((k, n), lambda i: (0, 0), pipeline_mode=pl.Buffered(1)),
            pl.BlockSpec((tm, n), lambda i: (i, 0)),
            pl.BlockSpec((1, n), lambda i: (0, 0)),
        ],
        out_specs=pl.BlockSpec((tm, n), lambda i: (i, 0)),
        out_shape=jax.ShapeDtypeStruct((m, n), F32),
        scratch_shapes=[pltpu.VMEM((k, n), BF16)],
        compiler_params=pltpu.CompilerParams(dimension_semantics=("arbitrary",),
                                             vmem_limit_bytes=VMEM_LIMIT),
        name="out_proj",
    )(a2d, w, resid2d, gain.reshape(1, n))


def _value_tiles(src_ref, dst_ref):
    for c in range(N_TILES):
        dst_ref[c, 0:HEAD_DIM, :] = src_ref[:, c * TILE:(c + 1) * TILE]
        dst_ref[c, HEAD_DIM:VT_ROWS, :] = jnp.ones((VT_ROWS - HEAD_DIM, TILE), BF16)


def _pair_tiles(n):
    return (n, N_TILES - 1 - n)


def _tile_rows(idx):
    return pl.ds(pl.multiple_of(idx * TILE, TILE), TILE)


def _normalized(acc):
    return acc[0:HEAD_DIM, :] / acc[HEAD_DIM:HEAD_DIM + 1, :]


class _PairSoftmax:
    def __init__(self, step_fn, heads, q_ref, k_fn, vt_fn, bias_fn, s_ref, acc_ref):
        self.step_fn, self.heads, self.q_ref, self.k_fn, self.vt_fn = step_fn, heads, q_ref, k_fn, vt_fn
        self.bias_fn, self.s_ref, self.acc_ref = bias_fn, s_ref, acc_ref

    def initial_max(self):
        neg = (jnp.full((1, TILE), NEG, F32),) * self.heads
        return neg, neg

    def scores(self, t, ms):
        slot, j, b = self.step_fn(t)
        first = slot == 0
        m0, m1 = list(ms[0]), list(ms[1])
        for h in range(self.heads):
            s = (jnp.dot(self.k_fn(h, j), self.q_ref[slot, h], preferred_element_type=F32)
                 + self.bias_fn(h, b))
            self.s_ref[h, t] = s
            cm = jnp.max(s, axis=0, keepdims=True)
            m0[h] = jnp.where(first, jnp.maximum(m0[h], cm), m0[h])
            m1[h] = jnp.where(first, m1[h], jnp.maximum(m1[h], cm))
        return tuple(m0), tuple(m1)

    def clear(self):
        for slot in range(2):
            for h in range(self.heads):
                self.acc_ref[slot, h] = jnp.zeros((VT_ROWS, TILE), F32)

    def weighted(self, t, ms):
        slot, j, _ = self.step_fn(t)
        first = slot == 0
        for h in range(self.heads):
            p = jnp.exp2((self.s_ref[h, t] - jnp.where(first, ms[0][h], ms[1][h])).astype(BF16))
            self.acc_ref[slot, h] += jnp.dot(self.vt_fn(h, j), p, preferred_element_type=F32)

    def run(self, steps):
        ms = lax.fori_loop(0, steps, self.scores, self.initial_max(), unroll=True)
        self.clear()

        def body(t, carry):
            self.weighted(t, ms)
            return carry

        lax.fori_loop(0, steps, body, 0, unroll=STEP_UNROLL)


def _causal_step(n):
    def step(t):
        first = t <= n
        return (jnp.where(first, 0, 1), jnp.where(first, t, t - n - 1), jnp.where(first, n - t, N_TILES - t))
    return step


def _window_step(n):
    def step(t):
        if isinstance(t, int):
            slot, d = divmod(t, WIN_TILES)
            idx = _pair_tiles(n)[slot]
        else:
            slot = jnp.where(t >= WIN_TILES, 1, 0)
            d = t - WIN_TILES * slot
            idx = jnp.where(slot == 0, n, N_TILES - 1 - n)
        j = idx - d
        return slot, jnp.maximum(j, 0), jnp.where(j >= 0, d, TAB_W_ROWS - 1)
    return step


def _attn_a_kernel(qa_ref, qb_ref, k_ref, v_ref, za_ref, zb_ref, r_ref, o_ref,
                   vt_ref, qt_ref, s_ref, acc_ref):
    n = pl.program_id(2)
    heads = HEADS_PER_STEP

    @pl.when(n == 0)
    def _():
        for h in range(heads):
            _value_tiles(v_ref.at[h], vt_ref.at[h])

    for slot, q_ref in enumerate((qa_ref, qb_ref)):
        for h in range(heads):
            qt_ref[slot, h] = q_ref[h]

    _PairSoftmax(_causal_step(n), heads, qt_ref, lambda h, j: k_ref[h, _tile_rows(j), :],
                 lambda h, j: vt_ref[h, j], lambda h, delta: r_ref[h, delta], s_ref, acc_ref).run(CAUSAL_STEPS)
    for slot, (z_ref, idx) in enumerate(zip((za_ref, zb_ref), _pair_tiles(n))):
        for h in range(heads):
            y = (_normalized(acc_ref[slot, h]) * z_ref[h].astype(F32)).T
            o_ref[_tile_rows(idx), h * HEAD_DIM:(h + 1) * HEAD_DIM] = y.astype(BF16)


def _resident(block_shape, index_map):
    return pl.BlockSpec(block_shape, index_map, pipeline_mode=pl.Buffered(1))


def _attn_a(k_nat, qvz_t, table):
    batch = k_nat.shape[0]
    heads = HEADS_PER_STEP
    hgroups = N_HEADS // heads

    def tile_slot(s, second):
        def index(hg, b, n):
            return (b, s * hgroups + hg, 0, N_TILES - 1 - n if second else n)
        return pl.BlockSpec((None, heads, HEAD_DIM, TILE), index)

    return pl.pallas_call(
        _attn_a_kernel,
        grid=(hgroups, batch, N_PAIRS),
        in_specs=[tile_slot(0, False), tile_slot(0, True),
                  pl.BlockSpec((None, heads, SEQ, HEAD_DIM), lambda hg, b, n: (b, hg, 0, 0)),
                  pl.BlockSpec((None, heads, HEAD_DIM, SEQ), lambda hg, b, n: (b, hgroups + hg, 0, 0)),
                  tile_slot(2, False), tile_slot(2, True),
                  _resident((heads, N_TILES, TILE, TILE), lambda hg, b, n: (hg, TAB_A // N_TILES, 0, 0))],
        out_specs=pl.BlockSpec((None, SEQ, heads * HEAD_DIM), lambda hg, b, n: (b, 0, hg)),
        out_shape=jax.ShapeDtypeStruct((batch, SEQ, ATT_WIDTH), BF16),
        scratch_shapes=[
            pltpu.VMEM((heads, N_TILES, VT_ROWS, TILE), BF16),
            pltpu.VMEM((2, heads, HEAD_DIM, TILE), BF16),
            pltpu.VMEM((heads, CAUSAL_STEPS, TILE, TILE), F32),
            pltpu.VMEM((2, heads, VT_ROWS, TILE), F32),
        ],
        compiler_params=pltpu.CompilerParams(dimension_semantics=("parallel", "parallel", "arbitrary"),
                                             vmem_limit_bytes=VMEM_LIMIT),
        name="dilated_attn",
    )(qvz_t, qvz_t, k_nat, qvz_t, qvz_t, qvz_t, table)


def _cmp_kernel(c_ref, pos_ref, w1_ref, w2_ref, o_ref):
    c = c_ref[...].astype(F32)
    x_lo = (c + pos_ref[0]).astype(BF16)
    x_hi = (c + pos_ref[1]).astype(BF16)
    a = jnp.dot(x_lo, w1_ref[0:CMP_CHUNK, :], preferred_element_type=F32)
    bm = jnp.dot(x_hi, w1_ref[CMP_CHUNK:2 * CMP_CHUNK, :], preferred_element_type=F32)
    hid = jax.nn.gelu(a + pltpu.roll(bm, N_CMP - 1, 0))
    res = jnp.dot(hid.astype(BF16), w2_ref[...], preferred_element_type=F32)
    o_ref[...] = jnp.where(pl.program_id(0) == 1, res.T, res).astype(BF16)


def _compress(chunks, pos, w1_bf, w2_bf):
    batch = chunks.shape[0]
    return pl.pallas_call(
        _cmp_kernel,
        grid=(2, batch, NSA_KV_GROUPS),
        in_specs=[
            pl.BlockSpec((None, None, N_CMP, CMP_CHUNK), lambda t, b, g: (b, t * NSA_KV_GROUPS + g, 0, 0)),
            pl.BlockSpec((None, 2, 1, CMP_CHUNK), lambda t, b, g: (t, 0, 0, 0)),
            pl.BlockSpec((None, 2 * CMP_CHUNK, CMP_HIDDEN), lambda t, b, g: (t, 0, 0)),
            pl.BlockSpec((None, CMP_HIDDEN, HEAD_DIM), lambda t, b, g: (t, 0, 0)),
        ],
        out_specs=pl.BlockSpec((None, None, None, N_CMP, HEAD_DIM), lambda t, b, g: (t, b, g, 0, 0)),
        out_shape=jax.ShapeDtypeStruct((2, batch, NSA_KV_GROUPS, N_CMP, HEAD_DIM), BF16),
        compiler_params=pltpu.CompilerParams(dimension_semantics=("arbitrary", "arbitrary", "arbitrary"),
                                             vmem_limit_bytes=VMEM_LIMIT),
        name="compress_kv",
    )(chunks, pos, w1_bf, w2_bf)


def _nsa_kernel(qa_ref, qb_ref, zca_ref, zcb_ref, zsa_ref, zsb_ref, zwa_ref, zwb_ref, gla_ref, glb_ref,
                kc_ref, vct_ref, ks_ref, vs_ref, kw_ref, vw_ref, rs_ref, rw_ref, rca_ref, rcb_ref,
                ovt_ref, onehot_ref, o_ref,
                kaug_ref, vst_ref, vwt_ref, qa_sc, qw_sc, gates_ref, ycmp_ref, s_ref, accs_ref, accw_ref):
    n = pl.program_id(2)
    heads = NSA_HEADS_PER_GROUP
    tiles = _pair_tiles(n)

    @pl.when(n == 0)
    def _():
        kaug_ref[:, 0:HEAD_DIM] = ks_ref[...]
        kaug_ref[:, HEAD_DIM:2 * HEAD_DIM] = onehot_ref[...]
        _value_tiles(vs_ref, vst_ref)
        _value_tiles(vw_ref, vwt_ref)

    kc = kc_ref[...]
    vct = vct_ref[...]
    for slot, (q_ref, rc_ref, zc_ref, gl_ref) in enumerate(
            ((qa_ref, rca_ref, zca_ref, gla_ref), (qb_ref, rcb_ref, zcb_ref, glb_ref))):
        psum = jnp.zeros((N_CMP, TILE), F32)
        gates_ref[slot] = jax.nn.sigmoid(gl_ref[...].T)
        for h in range(heads):
            qt = q_ref[h]
            qw_sc[slot, h] = qt
            qa_sc[slot, h, 0:HEAD_DIM, :] = qt
            bias = rc_ref[h]
            s = jnp.dot(kc, qt, preferred_element_type=F32) + bias
            m = jnp.max(s, axis=0, keepdims=True)
            p = jnp.where(bias > 0.5 * NEG, jnp.exp2(s - m), 0.0)
            p = p / jnp.maximum(jnp.sum(p, axis=0, keepdims=True), 1e-30)
            psum = psum + p
            o_cmp = jnp.dot(vct, p.astype(BF16), preferred_element_type=F32)
            ycmp_ref[slot, h] = gates_ref[slot, h:h + 1, :] * o_cmp * zc_ref[h].astype(F32)

        imp = jnp.dot(ovt_ref[...], psum, precision=lax.Precision.HIGHEST, preferred_element_type=F32)
        blk = lax.broadcasted_iota(jnp.int32, (N_SLC, TILE), 0)
        tq = tiles[slot] * TILE + lax.broadcasted_iota(jnp.int32, (N_SLC, TILE), 1)
        cur = lax.shift_right_logical(tq, int(math.log2(SLC_BLOCK)))
        forced = (blk == 0) | (blk == cur) | (blk == cur - 1)
        score = jnp.where(forced, FORCE_SCORE, jnp.where(blk > cur, -FORCE_SCORE, imp))
        group = 8
        rows = [score[r:r + group, :] for r in range(0, N_SLC, group)]
        row_idx = lax.broadcasted_iota(jnp.int32, (group, TILE), 0)
        cnts = [jnp.zeros((group, TILE), jnp.int32) for _ in rows]
        for jj in range(N_SLC):
            sj = score[jj:jj + 1, :]
            for r, sr in enumerate(rows):
                lo = r * group
                if lo > jj:
                    beats = jnp.where(sj >= sr, 1, 0)
                elif lo + group - 1 <= jj:
                    beats = jnp.where(sj > sr, 1, 0)
                else:
                    beats = jnp.where(row_idx + lo > jj, jnp.where(sj >= sr, 1, 0), jnp.where(sj > sr, 1, 0))
                cnts[r] = cnts[r] + beats
        cnt = jnp.concatenate(cnts, axis=0)
        mask = jnp.where(cnt < SLC_TOP_N, 0.0, NEG)
        mask = jnp.concatenate([mask, jnp.zeros((HEAD_DIM - N_SLC, TILE), F32)], axis=0).astype(BF16)
        for h in range(heads):
            qa_sc[slot, h, HEAD_DIM:2 * HEAD_DIM, :] = mask

    selected = _PairSoftmax(_causal_step(n), heads, qa_sc, lambda h, j: kaug_ref[_tile_rows(j), :],
                            lambda h, j: vst_ref[j], lambda h, delta: rs_ref[h, delta], s_ref, accs_ref)
    window = _PairSoftmax(_window_step(n), heads, qw_sc, lambda h, j: kw_ref[_tile_rows(j), :],
                          lambda h, j: vwt_ref[j], lambda h, b: rw_ref[h, b], s_ref, accw_ref)
    selected.run(CAUSAL_STEPS)
    ms_win = window.initial_max()
    for t in range(2 * WIN_TILES):
        ms_win = window.scores(t, ms_win)
    window.clear()
    for t in range(2 * WIN_TILES):
        window.weighted(t, ms_win)

    for slot, (zs_ref, zw_ref) in enumerate(((zsa_ref, zwa_ref), (zsb_ref, zwb_ref))):
        for h in range(heads):
            y = (ycmp_ref[slot, h]
                 + gates_ref[slot, heads + h:heads + h + 1, :] * _normalized(accs_ref[slot, h])
                 * zs_ref[h].astype(F32)
                 + gates_ref[slot, 2 * heads + h:2 * heads + h + 1, :] * _normalized(accw_ref[slot, h])
                 * zw_ref[h].astype(F32))
            o_ref[_tile_rows(tiles[slot]), h * HEAD_DIM:(h + 1) * HEAD_DIM] = y.T.astype(BF16)


def _nsa_attn(qz_t, gate_logits, kv_nat, kv_t, cmp_kv, table, table_c):
    batch = qz_t.shape[0]
    heads = NSA_HEADS_PER_GROUP
    groups = NSA_KV_GROUPS
    _, _, _, _, ov_t, onehot = _static_maps()

    def tile_of(n, second):
        return N_TILES - 1 - n if second else n

    def qz_specs(slot_group):
        return [pl.BlockSpec((None, heads, HEAD_DIM, TILE),
                             lambda g, b, n, second=second: (b, slot_group * groups + g, 0, tile_of(n, second)))
                for second in (False, True)]

    def key_spec(branch):
        return pl.BlockSpec((None, None, SEQ, HEAD_DIM), lambda g, b, n: (b, branch * groups + g, 0, 0))

    def value_spec(branch):
        return pl.BlockSpec((None, None, HEAD_DIM, SEQ), lambda g, b, n: (b, branch * groups + g, 0, 0))

    def cmp_spec(t):
        return pl.BlockSpec((None, None, None, N_CMP, HEAD_DIM), lambda g, b, n: (t, b, g, 0, 0))

    gl_specs = [pl.BlockSpec((None, TILE, HEAD_DIM), lambda g, b, n, second=second: (b, tile_of(n, second), g))
                for second in (False, True)]
    rc_specs = [pl.BlockSpec((heads, None, N_CMP, TILE), lambda g, b, n, second=second: (g, tile_of(n, second), 0, 0))
                for second in (False, True)]

    return pl.pallas_call(
        _nsa_kernel,
        grid=(groups, batch, N_PAIRS),
        in_specs=[
            *qz_specs(0), *qz_specs(1), *qz_specs(2), *qz_specs(3), *gl_specs,
            cmp_spec(0), cmp_spec(1), key_spec(0), value_spec(0), key_spec(1), value_spec(1),
            _resident((heads, N_TILES, TILE, TILE), lambda g, b, n: (g, TAB_S // N_TILES, 0, 0)),
            _resident((heads, TAB_W_ROWS, TILE, TILE), lambda g, b, n: (g, TAB_W // TAB_W_ROWS, 0, 0)),
            *rc_specs,
            _resident((N_SLC, N_CMP), lambda g, b, n: (0, 0)),
            _resident((SEQ, HEAD_DIM), lambda g, b, n: (0, 0)),
        ],
        out_specs=pl.BlockSpec((None, SEQ, heads * HEAD_DIM), lambda g, b, n: (b, 0, g)),
        out_shape=jax.ShapeDtypeStruct((batch, SEQ, ATT_WIDTH), BF16),
        scratch_shapes=[
            pltpu.VMEM((SEQ, 2 * HEAD_DIM), BF16),
            pltpu.VMEM((N_TILES, VT_ROWS, TILE), BF16),
            pltpu.VMEM((N_TILES, VT_ROWS, TILE), BF16),
            pltpu.VMEM((2, heads, 2 * HEAD_DIM, TILE), BF16),
            pltpu.VMEM((2, heads, HEAD_DIM, TILE), BF16),
            pltpu.VMEM((2, HEAD_DIM, TILE), F32),
            pltpu.VMEM((2, heads, HEAD_DIM, TILE), F32),
            pltpu.VMEM((heads, CAUSAL_STEPS, TILE, TILE), F32),
            pltpu.VMEM((2, heads, VT_ROWS, TILE), F32),
            pltpu.VMEM((2, heads, VT_ROWS, TILE), F32),
        ],
        compiler_params=pltpu.CompilerParams(dimension_semantics=("parallel", "parallel", "arbitrary"),
                                             vmem_limit_bytes=VMEM_LIMIT),
        name="nsa_attn",
    )(qz_t, qz_t, qz_t, qz_t, qz_t, qz_t, qz_t, qz_t, gate_logits, gate_logits, cmp_kv, cmp_kv,
      kv_nat, kv_t, kv_nat, kv_t, table, table, table_c, table_c,
      jnp.asarray(ov_t), jnp.asarray(onehot, dtype=BF16))


def kernel(x, norm_pre, norm_post, rel_table, w_in_a, w_out_a, kv_norm, w_kv, cmp_pos_k, cmp_pos_v,
           cmp_w1_k, cmp_w2_k, cmp_w1_v, cmp_w2_v, w_in_b, w_out_b):
    batch = x.shape[0]
    m = batch * SEQ
    heads = NSA_HEADS_PER_GROUP
    groups = NSA_KV_GROUPS
    x2d = x.reshape(m, D_MODEL)
    table, table_c = _bias_tables(rel_table)

    tile_slots = 8
    tiles_per_part = ATT_WIDTH // (tile_slots * HEAD_DIM)

    kinds_a = [(kind,) * tile_slots for kind in ("tq", "n", "t", "ts") for _ in range(tiles_per_part)]
    k_nat, qvz_t = _norm_matmul(x2d, norm_pre[0], w_in_a[0], tile_kinds=kinds_a)
    att_a = _attn_a(k_nat, qvz_t, table)
    h1 = _out_proj(att_a.reshape(m, ATT_WIDTH), w_out_a[0], x2d, norm_post[0])

    kinds_kv = [("c",) * tile_slots] + [("n",) * groups + ("t",) * groups] * 2
    kv_nat, kv_t, cmp_chunks = _norm_matmul(h1, kv_norm, w_kv.astype(BF16), tile_kinds=kinds_kv)
    pos = jnp.stack([cmp_pos_k, cmp_pos_v]).reshape(2, 2, 1, CMP_CHUNK)
    cmp_kv = _compress(cmp_chunks, pos, jnp.stack([cmp_w1_k, cmp_w1_v]).astype(BF16),
                       jnp.stack([cmp_w2_k, cmp_w2_v]).astype(BF16))

    w_b = w_in_b[0]
    n_main = (1 + NSA_BRANCHES) * ATT_WIDTH
    wg = w_b[:, n_main:].reshape(D_MODEL, NSA_BRANCHES, groups, heads).transpose(0, 2, 1, 3)
    wg = jnp.pad(wg.reshape(D_MODEL, groups, NSA_BRANCHES * heads),
                 ((0, 0), (0, 0), (0, HEAD_DIM - NSA_BRANCHES * heads))).reshape(D_MODEL, groups * HEAD_DIM)
    kinds_b = [(kind,) * tile_slots for kind in ("tq",) + ("ts",) * NSA_BRANCHES for _ in range(tiles_per_part)]
    qz_t, gate_logits = _norm_matmul(h1, norm_pre[1], w_b.astype(BF16), wg.astype(BF16), tile_kinds=kinds_b)
    att_b = _nsa_attn(qz_t, gate_logits.reshape(batch, SEQ, groups * HEAD_DIM), kv_nat, kv_t, cmp_kv,
                      table, table_c)
    out = _out_proj(att_b.reshape(m, ATT_WIDTH), w_out_b[0], h1, norm_post[1])
    return out.reshape(batch, SEQ, D_MODEL)
```

```python
import functools
import math

import numpy as np
import jax
import jax.numpy as jnp
from jax import lax
from jax.experimental import pallas as pl
from jax.experimental.pallas import tpu as pltpu

F32 = jnp.float32
BF16 = jnp.bfloat16

D_MODEL = 2048
SEQ = 2048
N_HEADS = 16
HEAD_DIM = 128
ATT_WIDTH = N_HEADS * HEAD_DIM
ATT_SCALE = HEAD_DIM ** -0.5
DIL_PATTERNS = ((128, 1), (512, 4), (2048, 16))
REL_BUCKETS = 32
REL_MAX_EXACT = 16
REL_MAX_DISTANCE = 2048
NSA_KV_GROUPS = 4
NSA_HEADS_PER_GROUP = 4
NSA_BRANCHES = 3
CMP_BLOCK = 32
CMP_STRIDE = 16
CMP_HIDDEN = 256
SLC_BLOCK = 64
SLC_TOP_N = 16
WIN_SIZE = 512
RMS_EPS = 1e-6
NEG = -1e30
FORCE_SCORE = 1e9
LOG2E = math.log2(math.e)

TILE = 256
N_TILES = SEQ // TILE
N_PAIRS = N_TILES // 2
CAUSAL_STEPS = N_TILES + 1
STEP_UNROLL = 3
N_CMP = SEQ // CMP_STRIDE
CMP_CHUNK = CMP_STRIDE * HEAD_DIM
N_SLC = SEQ // SLC_BLOCK
WIN_TILES = -(-(WIN_SIZE - 1) // TILE) + 1
HEADS_PER_STEP = 4
VT_ROWS = HEAD_DIM + 16
MXU_COLS = 256

TAB_A, TAB_S, TAB_W = 0, N_TILES, 2 * N_TILES
TAB_W_ROWS = 4
TAB_ROWS = 2 * N_TILES + TAB_W_ROWS
VMEM_LIMIT = 52 * 1024 * 1024


def _np_bucket(dist):
    n = np.maximum(dist, 0)
    nf = np.maximum(n, 1).astype(np.float32)
    log_b = REL_MAX_EXACT + (
        np.log(nf / np.float32(REL_MAX_EXACT)) / np.float32(math.log(REL_MAX_DISTANCE / REL_MAX_EXACT))
        * np.float32(REL_BUCKETS - REL_MAX_EXACT)).astype(np.int32)
    return np.where(n < REL_MAX_EXACT, n, np.minimum(log_b, REL_BUCKETS - 1)).astype(np.int32)


@functools.lru_cache(maxsize=None)
def _static_maps():
    ki = np.arange(TILE)[:, None]
    qi = np.arange(TILE)[None, :]
    dist = TILE * np.arange(N_TILES)[:, None, None] + qi[None] - ki[None]
    bk_t = _np_bucket(dist)
    mult = np.zeros(dist.shape, np.int64)
    for window, dil in DIL_PATTERNS:
        mult += ((dist % dil == 0) & (dist <= window)).astype(np.int64)
    base_a = np.where((dist >= 0) & (mult > 0), np.log2(np.maximum(mult, 1)), NEG)
    base_s = np.where(dist >= 0, 0.0, NEG)
    base_w = np.full((TAB_W_ROWS, TILE, TILE), NEG)
    base_w[:WIN_TILES] = np.where((dist >= 0) & (dist < WIN_SIZE), 0.0, NEG)[:WIN_TILES]
    base_t = np.concatenate([base_a, base_s, base_w]).astype(np.float32)
    ci = np.arange(N_CMP)[:, None]
    t = TILE * np.arange(N_TILES)[:, None, None] + qi[None]
    dist_c = t - (CMP_STRIDE * ci[None] + CMP_BLOCK - 1)
    bk_c = _np_bucket(dist_c)
    base_c = np.where((dist_c >= 0) & (ci[None] < N_CMP - 1), 0.0, NEG).astype(np.float32)
    cs = np.arange(N_CMP)[None, :] * CMP_STRIDE
    sj = np.arange(N_SLC)[:, None] * SLC_BLOCK
    ov_t = ((cs < sj + SLC_BLOCK) & (cs + CMP_BLOCK > sj) & (np.arange(N_CMP)[None, :] < N_CMP - 1))
    onehot = np.zeros((SEQ, HEAD_DIM), np.float32)
    onehot[np.arange(SEQ), np.arange(SEQ) // SLC_BLOCK] = 1.0
    return bk_t.astype(np.int32), base_t, bk_c.astype(np.int32), base_c, ov_t.astype(np.float32), onehot


def _bias_kernel(tab_ref, bkt_ref, baset_ref, bkc_ref, basec_ref, out_ref, outc_ref, *, present_t, present_c):
    h = pl.program_id(0)
    tv = [tab_ref[b, h] * LOG2E for b in range(REL_BUCKETS)]

    def lookup(bk, present):
        val = jnp.full(bk.shape, tv[present[0]], F32)
        for b in present[1:]:
            val = jnp.where(bk == b, tv[b], val)
        return val

    for d in range(N_TILES):
        g = lookup(bkt_ref[d], present_t[d])
        out_ref[TAB_A + d] = g + baset_ref[TAB_A + d]
        out_ref[TAB_S + d] = g + baset_ref[TAB_S + d]
        if d < TAB_W_ROWS:
            out_ref[TAB_W + d] = g + baset_ref[TAB_W + d]
        outc_ref[d] = lookup(bkc_ref[d], present_c[d]) + basec_ref[d]


def _bias_tables(rel_table):
    bk_t, base_t, bk_c, base_c, _, _ = _static_maps()
    present_t = tuple(tuple(int(b) for b in np.unique(bk_t[d])) for d in range(N_TILES))
    present_c = tuple(tuple(int(b) for b in np.unique(bk_c[d])) for d in range(N_TILES))
    return pl.pallas_call(
        functools.partial(_bias_kernel, present_t=present_t, present_c=present_c),
        grid=(N_HEADS,),
        in_specs=[
            pl.BlockSpec(memory_space=pltpu.SMEM),
            pl.BlockSpec((N_TILES, TILE, TILE), lambda h: (0, 0, 0)),
            pl.BlockSpec((TAB_ROWS, TILE, TILE), lambda h: (0, 0, 0)),
            pl.BlockSpec((N_TILES, N_CMP, TILE), lambda h: (0, 0, 0)),
            pl.BlockSpec((N_TILES, N_CMP, TILE), lambda h: (0, 0, 0)),
        ],
        out_specs=[
            pl.BlockSpec((None, TAB_ROWS, TILE, TILE), lambda h: (h, 0, 0, 0)),
            pl.BlockSpec((None, N_TILES, N_CMP, TILE), lambda h: (h, 0, 0, 0)),
        ],
        out_shape=[
            jax.ShapeDtypeStruct((N_HEADS, TAB_ROWS, TILE, TILE), F32),
            jax.ShapeDtypeStruct((N_HEADS, N_TILES, N_CMP, TILE), F32),
        ],
        compiler_params=pltpu.CompilerParams(dimension_semantics=("arbitrary",),
                                             vmem_limit_bytes=VMEM_LIMIT),
        name="bias_tables",
    )(rel_table, jnp.asarray(bk_t), jnp.asarray(base_t), jnp.asarray(bk_c), jnp.asarray(base_c))


def _silu(z):
    half = 0.5 * z
    return half + half * jnp.tanh(half)


TRANSPOSED_KINDS = ("t", "tq", "ts")


def _nm_kernel(*refs, tile_kinds, gate, w_transposed):
    refs = list(refs)
    x_ref, g_ref, w_ref = refs[:3]
    del refs[:3]
    wg_ref = refs.pop(0) if gate else None
    kinds_used = {k for kinds in tile_kinds for k in kinds}
    on_ref = refs.pop(0) if "n" in kinds_used else None
    ot_ref = refs.pop(0) if kinds_used & set(TRANSPOSED_KINDS) else None
    og_ref = refs.pop(0) if gate else None
    oc_ref = refs.pop(0) if "c" in kinds_used else None
    xn_ref = refs.pop(0)
    res_ref = refs.pop(0) if "c" in kinds_used else None
    j = pl.program_id(1)

    @pl.when(j == 0)
    def _():
        x = x_ref[...]
        ms = jnp.mean(x * x, axis=-1, keepdims=True)
        xn_ref[...] = (x * lax.rsqrt(ms + RMS_EPS) * g_ref[...]).astype(BF16)
        if gate:
            og_ref[...] = jnp.dot(xn_ref[...], wg_ref[...], preferred_element_type=F32)

    def product(kinds):
        pos = {"n": 0, "t": 0}
        for cc in range(len(kinds) // 2):
            if w_transposed:
                w_cols = w_ref[cc * MXU_COLS:(cc + 1) * MXU_COLS, :].astype(BF16)
                res = lax.dot_general(xn_ref[...], w_cols, (((1,), (1,)), ((), ())), preferred_element_type=F32)
            else:
                w_cols = w_ref[:, cc * MXU_COLS:(cc + 1) * MXU_COLS].astype(BF16)
                res = jnp.dot(xn_ref[...], w_cols, preferred_element_type=F32)
            for u in range(2):
                c = 2 * cc + u
                piece = res[:, u * HEAD_DIM:(u + 1) * HEAD_DIM]
                kind = kinds[c]
                if kind == "n":
                    on_ref[pos["n"]] = piece.astype(BF16)
                    pos["n"] += 1
                elif kind == "c":
                    res_ref[c] = piece
                else:
                    if kind == "tq":
                        piece = piece * (ATT_SCALE * LOG2E)
                    elif kind == "ts":
                        piece = _silu(piece)
                    ot_ref[pos["t"]] = piece.T.astype(BF16)
                    pos["t"] += 1
        if "c" in kinds:
            rows = res_ref.shape[1] // CMP_STRIDE
            for c in range(len(kinds)):
                for i in range(CMP_STRIDE):
                    oc_ref[c, :, i * HEAD_DIM:(i + 1) * HEAD_DIM] = res_ref[
                        c, pl.ds(i, rows, stride=CMP_STRIDE), :].astype(BF16)

    for kinds in sorted(set(tile_kinds)):
        tiles = [t for t, k in enumerate(tile_kinds) if k == kinds]
        cond = functools.reduce(lambda a, b: a | b, [j == t for t in tiles])
        pl.when(cond)(functools.partial(product, kinds))


def _norm_matmul(x2d, gain, w, wg_bf=None, *, tile_kinds, w_transposed=False, tm=1024, tn=1024):
    m, d = x2d.shape
    slots = tn // HEAD_DIM
    batch = m // SEQ
    spb = SEQ // tm
    gate = wg_bf is not None
    assert all(len(k) == slots for k in tile_kinds)

    def family(match):
        counts = [sum(1 for k in kinds if match(k)) for kinds in tile_kinds]
        tiles = [t for t, c in enumerate(counts) if c]
        per_tile = counts[tiles[0]] if tiles else 0
        assert all(counts[t] == per_tile for t in tiles)

        def block(j):
            return jnp.minimum(sum(jnp.where(j > t, 1, 0) for t in tiles), len(tiles) - 1)
        return tiles, per_tile, block

    n_tiles, n_per, n_block = family(lambda k: k == "n")
    t_tiles, t_per, t_block = family(lambda k: k in TRANSPOSED_KINDS)
    c_tiles, c_per, _ = family(lambda k: k == "c")
    assert c_tiles in ([], [0]) and c_per in (0, slots)

    in_specs = [
        pl.BlockSpec((tm, d), lambda i, j: (i, 0)),
        pl.BlockSpec((1, d), lambda i, j: (0, 0)),
        pl.BlockSpec((tn, d), lambda i, j: (j, 0)) if w_transposed else pl.BlockSpec((d, tn), lambda i, j: (0, j)),
    ]
    args = [x2d, gain.reshape(1, d), w]
    out_shape, out_specs = [], []
    scratch = [pltpu.VMEM((tm, d), BF16)]
    if gate:
        in_specs.append(pl.BlockSpec((d, wg_bf.shape[1]), lambda i, j: (0, 0)))
        args.append(wg_bf)
    if n_tiles:
        out_shape.append(jax.ShapeDtypeStruct((batch, n_per * len(n_tiles), SEQ, HEAD_DIM), BF16))
        out_specs.append(pl.BlockSpec((None, n_per, tm, HEAD_DIM),
                                      lambda i, j: (i // spb, n_block(j), i % spb, 0)))
    if t_tiles:
        out_shape.append(jax.ShapeDtypeStruct((batch, t_per * len(t_tiles), HEAD_DIM, SEQ), BF16))
        out_specs.append(pl.BlockSpec((None, t_per, HEAD_DIM, tm),
                                      lambda i, j: (i // spb, t_block(j), 0, i % spb)))
    if gate:
        out_shape.append(jax.ShapeDtypeStruct((m, wg_bf.shape[1]), F32))
        out_specs.append(pl.BlockSpec((tm, wg_bf.shape[1]), lambda i, j: (i, 0)))
    if c_tiles:
        out_shape.append(jax.ShapeDtypeStruct((batch, slots, N_CMP, CMP_CHUNK), BF16))
        out_specs.append(pl.BlockSpec((None, slots, tm // CMP_STRIDE, CMP_CHUNK),
                                      lambda i, j: (i // spb, 0, i % spb, 0)))
        scratch.append(pltpu.VMEM((slots, tm, HEAD_DIM), F32))
    return pl.pallas_call(
        functools.partial(_nm_kernel, tile_kinds=tuple(tuple(k) for k in tile_kinds), gate=gate,
                          w_transposed=w_transposed),
        grid=(m // tm, len(tile_kinds)),
        in_specs=in_specs,
        out_specs=out_specs,
        out_shape=out_shape,
        scratch_shapes=scratch,
        compiler_params=pltpu.CompilerParams(dimension_semantics=("parallel", "arbitrary"),
                                             vmem_limit_bytes=VMEM_LIMIT),
        name="norm_matmul",
    )(*args)


def _op_kernel(a_ref, w_ref, r_ref, g_ref, o_ref, wbf_ref, *, parts):
    @pl.when(pl.program_id(0) == 0)
    def _():
        step = 256
        for r in range(w_ref.shape[0] // step):
            wbf_ref[r * step:(r + 1) * step, :] = w_ref[r * step:(r + 1) * step, :].astype(BF16)

    rows = a_ref.shape[0] // parts
    for r in range(parts):
        sl = slice(r * rows, (r + 1) * rows)
        y = jnp.dot(a_ref[sl, :], wbf_ref[...], preferred_element_type=F32)
        ms = jnp.mean(y * y, axis=-1, keepdims=True)
        o_ref[sl, :] = r_ref[sl, :] + y * lax.rsqrt(ms + RMS_EPS) * g_ref[...]


def _out_proj(a2d, w, resid2d, gain, *, tm=512, parts=2):
    m, k = a2d.shape
    n = w.shape[1]
    return pl.pallas_call(
        functools.partial(_op_kernel, parts=parts),
        grid=(m // tm,),
        in_specs=[
            pl.BlockSpec((tm, k), lambda i: (i, 0)),
            pl.BlockSpec((k, n), lambda i: (0, 0), pipeline_mode=pl.Buffered(1)),
            pl.BlockSpec((tm, n), lambda i: (i, 0)),
            pl.BlockSpec((1, n), lambda i: (0, 0)),
        ],
        out_specs=pl.BlockSpec((tm, n), lambda i: (i, 0)),
        out_shape=jax.ShapeDtypeStruct((m, n), F32),
        scratch_shapes=[pltpu.VMEM((k, n), BF16)],
        compiler_params=pltpu.CompilerParams(dimension_semantics=("arbitrary",),
                                             vmem_limit_bytes=VMEM_LIMIT),
        name="out_proj",
    )(a2d, w, resid2d, gain.reshape(1, n))


def _value_tiles(src_ref, dst_ref):
    for c in range(N_TILES):
        dst_ref[c, 0:HEAD_DIM, :] = src_ref[:, c * TILE:(c + 1) * TILE]
        dst_ref[c, HEAD_DIM:VT_ROWS, :] = jnp.ones((VT_ROWS - HEAD_DIM, TILE), BF16)


def _pair_tiles(n):
    return (n, N_TILES - 1 - n)


def _tile_rows(idx):
    return pl.ds(pl.multiple_of(idx * TILE, TILE), TILE)


def _normalized(acc):
    return acc[0:HEAD_DIM, :] / acc[HEAD_DIM:HEAD_DIM + 1, :]


class _PairSoftmax:
    def __init__(self, step_fn, heads, q_ref, k_fn, vt_fn, bias_fn, s_ref, acc_ref):
        self.step_fn, self.heads, self.q_ref, self.k_fn, self.vt_fn = step_fn, heads, q_ref, k_fn, vt_fn
        self.bias_fn, self.s_ref, self.acc_ref = bias_fn, s_ref, acc_ref

    def initial_max(self):
        neg = (jnp.full((1, TILE), NEG, F32),) * self.heads
        return neg, neg

    def scores(self, t, ms):
        slot, j, b = self.step_fn(t)
        first = slot == 0
        m0, m1 = list(ms[0]), list(ms[1])
        for h in range(self.heads):
            s = (jnp.dot(self.k_fn(h, j), self.q_ref[slot, h], preferred_element_type=F32)
                 + self.bias_fn(h, b))
            self.s_ref[h, t] = s
            cm = jnp.max(s, axis=0, keepdims=True)
            m0[h] = jnp.where(first, jnp.maximum(m0[h], cm), m0[h])
            m1[h] = jnp.where(first, m1[h], jnp.maximum(m1[h], cm))
        return tuple(m0), tuple(m1)

    def clear(self):
        for slot in range(2):
            for h in range(self.heads):
                self.acc_ref[slot, h] = jnp.zeros((VT_ROWS, TILE), F32)

    def weighted(self, t, ms):
        slot, j, _ = self.step_fn(t)
        first = slot == 0
        for h in range(self.heads):
            p = jnp.exp2((self.s_ref[h, t] - jnp.where(first, ms[0][h], ms[1][h])).astype(BF16))
            self.acc_ref[slot, h] += jnp.dot(self.vt_fn(h, j), p, preferred_element_type=F32)

    def run(self, steps):
        ms = lax.fori_loop(0, steps, self.scores, self.initial_max(), unroll=True)
        self.clear()

        def body(t, carry):
            self.weighted(t, ms)
            return carry

        lax.fori_loop(0, steps, body, 0, unroll=STEP_UNROLL)

    def run_unrolled(self, steps):
        ms = self.initial_max()
        for t in range(steps):
            ms = self.scores(t, ms)
        self.clear()
        for t in range(steps):
            self.weighted(t, ms)


def _causal_step(n):
    def step(t):
        first = t <= n
        return (jnp.where(first, 0, 1), jnp.where(first, t, t - n - 1), jnp.where(first, n - t, N_TILES - t))
    return step


def _window_step(n):
    def step(t):
        if isinstance(t, int):
            slot, d = divmod(t, WIN_TILES)
            idx = _pair_tiles(n)[slot]
        else:
            slot = jnp.where(t >= WIN_TILES, 1, 0)
            d = t - WIN_TILES * slot
            idx = jnp.where(slot == 0, n, N_TILES - 1 - n)
        j = idx - d
        return slot, jnp.maximum(j, 0), jnp.where(j >= 0, d, TAB_W_ROWS - 1)
    return step


def _attn_a_kernel(qa_ref, qb_ref, k_ref, v_ref, za_ref, zb_ref, r_ref, o_ref,
                   vt_ref, qt_ref, s_ref, acc_ref):
    n = pl.program_id(2)
    heads = HEADS_PER_STEP

    @pl.when(n == 0)
    def _():
        for h in range(heads):
            _value_tiles(v_ref.at[h], vt_ref.at[h])

    for slot, q_ref in enumerate((qa_ref, qb_ref)):
        for h in range(heads):
            qt_ref[slot, h] = q_ref[h]

    _PairSoftmax(_causal_step(n), heads, qt_ref, lambda h, j: k_ref[h, _tile_rows(j), :],
                 lambda h, j: vt_ref[h, j], lambda h, delta: r_ref[h, delta], s_ref, acc_ref).run(CAUSAL_STEPS)
    for slot, (z_ref, idx) in enumerate(zip((za_ref, zb_ref), _pair_tiles(n))):
        for h in range(heads):
            y = (_normalized(acc_ref[slot, h]) * z_ref[h].astype(F32)).T
            o_ref[_tile_rows(idx), h * HEAD_DIM:(h + 1) * HEAD_DIM] = y.astype(BF16)


def _resident(block_shape, index_map):
    return pl.BlockSpec(block_shape, index_map, pipeline_mode=pl.Buffered(1))


def _attn_a(k_nat, qvz_t, table):
    batch = k_nat.shape[0]
    heads = HEADS_PER_STEP
    hgroups = N_HEADS // heads

    def tile_slot(s, second):
        def index(hg, b, n):
            return (b, s * hgroups + hg, 0, N_TILES - 1 - n if second else n)
        return pl.BlockSpec((None, heads, HEAD_DIM, TILE), index)

    return pl.pallas_call(
        _attn_a_kernel,
        grid=(hgroups, batch, N_PAIRS),
        in_specs=[tile_slot(0, False), tile_slot(0, True),
                  pl.BlockSpec((None, heads, SEQ, HEAD_DIM), lambda hg, b, n: (b, hg, 0, 0)),
                  pl.BlockSpec((None, heads, HEAD_DIM, SEQ), lambda hg, b, n: (b, hgroups + hg, 0, 0)),
                  tile_slot(2, False), tile_slot(2, True),
                  _resident((heads, N_TILES, TILE, TILE), lambda hg, b, n: (hg, TAB_A // N_TILES, 0, 0))],
        out_specs=pl.BlockSpec((None, SEQ, heads * HEAD_DIM), lambda hg, b, n: (b, 0, hg)),
        out_shape=jax.ShapeDtypeStruct((batch, SEQ, ATT_WIDTH), BF16),
        scratch_shapes=[
            pltpu.VMEM((heads, N_TILES, VT_ROWS, TILE), BF16),
            pltpu.VMEM((2, heads, HEAD_DIM, TILE), BF16),
            pltpu.VMEM((heads, CAUSAL_STEPS, TILE, TILE), F32),
            pltpu.VMEM((2, heads, VT_ROWS, TILE), F32),
        ],
        compiler_params=pltpu.CompilerParams(dimension_semantics=("parallel", "parallel", "arbitrary"),
                                             vmem_limit_bytes=VMEM_LIMIT),
        name="dilated_attn",
    )(qvz_t, qvz_t, k_nat, qvz_t, qvz_t, qvz_t, table)


def _cmp_kernel(c_ref, pos_ref, w1_ref, w2_ref, o_ref):
    c = c_ref[...].astype(F32)
    x_lo = (c + pos_ref[0]).astype(BF16)
    x_hi = (c + pos_ref[1]).astype(BF16)
    a = jnp.dot(x_lo, w1_ref[0:CMP_CHUNK, :], preferred_element_type=F32)
    bm = jnp.dot(x_hi, w1_ref[CMP_CHUNK:2 * CMP_CHUNK, :], preferred_element_type=F32)
    hid = jax.nn.gelu(a + pltpu.roll(bm, N_CMP - 1, 0))
    res = jnp.dot(hid.astype(BF16), w2_ref[...], preferred_element_type=F32)
    o_ref[...] = jnp.where(pl.program_id(0) == 1, res.T, res).astype(BF16)


def _compress(chunks, pos, w1_bf, w2_bf):
    batch = chunks.shape[0]
    return pl.pallas_call(
        _cmp_kernel,
        grid=(2, batch, NSA_KV_GROUPS),
        in_specs=[
            pl.BlockSpec((None, None, N_CMP, CMP_CHUNK), lambda t, b, g: (b, t * NSA_KV_GROUPS + g, 0, 0)),
            pl.BlockSpec((None, 2, 1, CMP_CHUNK), lambda t, b, g: (t, 0, 0, 0)),
            pl.BlockSpec((None, 2 * CMP_CHUNK, CMP_HIDDEN), lambda t, b, g: (t, 0, 0)),
            pl.BlockSpec((None, CMP_HIDDEN, HEAD_DIM), lambda t, b, g: (t, 0, 0)),
        ],
        out_specs=pl.BlockSpec((None, None, None, N_CMP, HEAD_DIM), lambda t, b, g: (t, b, g, 0, 0)),
        out_shape=jax.ShapeDtypeStruct((2, batch, NSA_KV_GROUPS, N_CMP, HEAD_DIM), BF16),
        compiler_params=pltpu.CompilerParams(dimension_semantics=("arbitrary", "arbitrary", "arbitrary"),
                                             vmem_limit_bytes=VMEM_LIMIT),
        name="compress_kv",
    )(chunks, pos, w1_bf, w2_bf)


def _nsa_kernel(qa_ref, qb_ref, zca_ref, zcb_ref, zsa_ref, zsb_ref, zwa_ref, zwb_ref, gla_ref, glb_ref,
                kc_ref, vct_ref, ks_ref, vs_ref, kw_ref, vw_ref, rs_ref, rw_ref, rca_ref, rcb_ref,
                ovt_ref, onehot_ref, o_ref,
                kaug_ref, vst_ref, vwt_ref, qa_sc, qw_sc, gates_ref, ycmp_ref, s_ref, accs_ref, accw_ref):
    n = pl.program_id(2)
    heads = NSA_HEADS_PER_GROUP
    tiles = _pair_tiles(n)

    @pl.when(n == 0)
    def _():
        kaug_ref[:, 0:HEAD_DIM] = ks_ref[...]
        kaug_ref[:, HEAD_DIM:2 * HEAD_DIM] = onehot_ref[...]
        _value_tiles(vs_ref, vst_ref)
        _value_tiles(vw_ref, vwt_ref)

    kc = kc_ref[...]
    vct = vct_ref[...]
    for slot, (q_ref, rc_ref, zc_ref, gl_ref) in enumerate(
            ((qa_ref, rca_ref, zca_ref, gla_ref), (qb_ref, rcb_ref, zcb_ref, glb_ref))):
        psum = jnp.zeros((N_CMP, TILE), F32)
        gates_ref[slot] = jax.nn.sigmoid(gl_ref[...].T)
        for h in range(heads):
            qt = q_ref[h]
            qw_sc[slot, h] = qt
            qa_sc[slot, h, 0:HEAD_DIM, :] = qt
            bias = rc_ref[h]
            s = jnp.dot(kc, qt, preferred_element_type=F32) + bias
            m = jnp.max(s, axis=0, keepdims=True)
            p = jnp.where(bias > 0.5 * NEG, jnp.exp2(s - m), 0.0)
            p = p / jnp.maximum(jnp.sum(p, axis=0, keepdims=True), 1e-30)
            psum = psum + p
            o_cmp = jnp.dot(vct, p.astype(BF16), preferred_element_type=F32)
            ycmp_ref[slot, h] = gates_ref[slot, h:h + 1, :] * o_cmp * zc_ref[h].astype(F32)

        imp = jnp.dot(ovt_ref[...], psum, precision=lax.Precision.HIGHEST, preferred_element_type=F32)
        blk = lax.broadcasted_iota(jnp.int32, (N_SLC, TILE), 0)
        tq = tiles[slot] * TILE + lax.broadcasted_iota(jnp.int32, (N_SLC, TILE), 1)
        cur = lax.shift_right_logical(tq, int(math.log2(SLC_BLOCK)))
        forced = (blk == 0) | (blk == cur) | (blk == cur - 1)
        score = jnp.where(forced, FORCE_SCORE, jnp.where(blk > cur, -FORCE_SCORE, imp))
        group = 8
        rows = [score[r:r + group, :] for r in range(0, N_SLC, group)]
        row_idx = lax.broadcasted_iota(jnp.int32, (group, TILE), 0)
        cnts = [jnp.zeros((group, TILE), jnp.int32) for _ in rows]
        for jj in range(N_SLC):
            sj = score[jj:jj + 1, :]
            for r, sr in enumerate(rows):
                lo = r * group
                if lo > jj:
                    beats = jnp.where(sj >= sr, 1, 0)
                elif lo + group - 1 <= jj:
                    beats = jnp.where(sj > sr, 1, 0)
                else:
                    beats = jnp.where(row_idx + lo > jj, jnp.where(sj >= sr, 1, 0), jnp.where(sj > sr, 1, 0))
                cnts[r] = cnts[r] + beats
        cnt = jnp.concatenate(cnts, axis=0)
        mask = jnp.where(cnt < SLC_TOP_N, 0.0, NEG)
        mask = jnp.concatenate([mask, jnp.zeros((HEAD_DIM - N_SLC, TILE), F32)], axis=0).astype(BF16)
        for h in range(heads):
            qa_sc[slot, h, HEAD_DIM:2 * HEAD_DIM, :] = mask

    _PairSoftmax(_causal_step(n), heads, qa_sc, lambda h, j: kaug_ref[_tile_rows(j), :],
                 lambda h, j: vst_ref[j], lambda h, delta: rs_ref[h, delta], s_ref, accs_ref).run(CAUSAL_STEPS)
    _PairSoftmax(_window_step(n), heads, qw_sc, lambda h, j: kw_ref[_tile_rows(j), :],
                 lambda h, j: vwt_ref[j], lambda h, b: rw_ref[h, b], s_ref, accw_ref).run_unrolled(2 * WIN_TILES)

    for slot, (zs_ref, zw_ref) in enumerate(((zsa_ref, zwa_ref), (zsb_ref, zwb_ref))):
        for h in range(heads):
            y = (ycmp_ref[slot, h]
                 + gates_ref[slot, heads + h:heads + h + 1, :] * _normalized(accs_ref[slot, h])
                 * zs_ref[h].astype(F32)
                 + gates_ref[slot, 2 * heads + h:2 * heads + h + 1, :] * _normalized(accw_ref[slot, h])
                 * zw_ref[h].astype(F32))
            o_ref[_tile_rows(tiles[slot]), h * HEAD_DIM:(h + 1) * HEAD_DIM] = y.T.astype(BF16)


def _nsa_attn(qz_t, gate_logits, kv_nat, kv_t, cmp_kv, table, table_c):
    batch = qz_t.shape[0]
    heads = NSA_HEADS_PER_GROUP
    groups = NSA_KV_GROUPS
    _, _, _, _, ov_t, onehot = _static_maps()

    def tile_of(n, second):
        return N_TILES - 1 - n if second else n

    def qz_specs(slot_group):
        return [pl.BlockSpec((None, heads, HEAD_DIM, TILE),
                             lambda g, b, n, second=second: (b, slot_group * groups + g, 0, tile_of(n, second)))
                for second in (False, True)]

    def key_spec(branch):
        return pl.BlockSpec((None, None, SEQ, HEAD_DIM), lambda g, b, n: (b, branch * groups + g, 0, 0))

    def value_spec(branch):
        return pl.BlockSpec((None, None, HEAD_DIM, SEQ), lambda g, b, n: (b, branch * groups + g, 0, 0))

    def cmp_spec(t):
        return pl.BlockSpec((None, None, None, N_CMP, HEAD_DIM), lambda g, b, n: (t, b, g, 0, 0))

    gl_specs = [pl.BlockSpec((None, TILE, HEAD_DIM), lambda g, b, n, second=second: (b, tile_of(n, second), g))
                for second in (False, True)]
    rc_specs = [pl.BlockSpec((heads, None, N_CMP, TILE), lambda g, b, n, second=second: (g, tile_of(n, second), 0, 0))
                for second in (False, True)]

    return pl.pallas_call(
        _nsa_kernel,
        grid=(groups, batch, N_PAIRS),
        in_specs=[
            *qz_specs(0), *qz_specs(1), *qz_specs(2), *qz_specs(3), *gl_specs,
            cmp_spec(0), cmp_spec(1), key_spec(0), value_spec(0), key_spec(1), value_spec(1),
            _resident((heads, N_TILES, TILE, TILE), lambda g, b, n: (g, TAB_S // N_TILES, 0, 0)),
            _resident((heads, TAB_W_ROWS, TILE, TILE), lambda g, b, n: (g, TAB_W // TAB_W_ROWS, 0, 0)),
            *rc_specs,
            _resident((N_SLC, N_CMP), lambda g, b, n: (0, 0)),
            _resident((SEQ, HEAD_DIM), lambda g, b, n: (0, 0)),
        ],
        out_specs=pl.BlockSpec((None, SEQ, heads * HEAD_DIM), lambda g, b, n: (b, 0, g)),
        out_shape=jax.ShapeDtypeStruct((batch, SEQ, ATT_WIDTH), BF16),
        scratch_shapes=[
            pltpu.VMEM((SEQ, 2 * HEAD_DIM), BF16),
            pltpu.VMEM((N_TILES, VT_ROWS, TILE), BF16),
            pltpu.VMEM((N_TILES, VT_ROWS, TILE), BF16),
            pltpu.VMEM((2, heads, 2 * HEAD_DIM, TILE), BF16),
            pltpu.VMEM((2, heads, HEAD_DIM, TILE), BF16),
            pltpu.VMEM((2, HEAD_DIM, TILE), F32),
            pltpu.VMEM((2, heads, HEAD_DIM, TILE), F32),
            pltpu.VMEM((heads, CAUSAL_STEPS, TILE, TILE), F32),
            pltpu.VMEM((2, heads, VT_ROWS, TILE), F32),
            pltpu.VMEM((2, heads, VT_ROWS, TILE), F32),
        ],
        compiler_params=pltpu.CompilerParams(dimension_semantics=("parallel", "parallel", "arbitrary"),
                                             vmem_limit_bytes=VMEM_LIMIT),
        name="nsa_attn",
    )(qz_t, qz_t, qz_t, qz_t, qz_t, qz_t, qz_t, qz_t, gate_logits, gate_logits, cmp_kv, cmp_kv,
      kv_nat, kv_t, kv_nat, kv_t, table, table, table_c, table_c,
      jnp.asarray(ov_t), jnp.asarray(onehot, dtype=BF16))


def kernel(x, norm_pre, norm_post, rel_table, w_in_a, w_out_a, kv_norm, w_kv, cmp_pos_k, cmp_pos_v,
           cmp_w1_k, cmp_w2_k, cmp_w1_v, cmp_w2_v, w_in_b, w_out_b):
    batch = x.shape[0]
    m = batch * SEQ
    heads = NSA_HEADS_PER_GROUP
    groups = NSA_KV_GROUPS
    x2d = x.reshape(m, D_MODEL)
    table, table_c = _bias_tables(rel_table)

    tile_slots = 8
    tiles_per_part = ATT_WIDTH // (tile_slots * HEAD_DIM)

    kinds_a = [(kind,) * tile_slots for kind in ("tq", "n", "t", "ts") for _ in range(tiles_per_part)]
    k_nat, qvz_t = _norm_matmul(x2d, norm_pre[0], w_in_a[0], tile_kinds=kinds_a)
    att_a = _attn_a(k_nat, qvz_t, table)
    h1 = _out_proj(att_a.reshape(m, ATT_WIDTH), w_out_a[0], x2d, norm_post[0])

    kinds_kv = [("c",) * tile_slots] + [("n",) * groups + ("t",) * groups] * 2
    kv_nat, kv_t, cmp_chunks = _norm_matmul(h1, kv_norm, w_kv, tile_kinds=kinds_kv)
    pos = jnp.stack([cmp_pos_k, cmp_pos_v]).reshape(2, 2, 1, CMP_CHUNK)
    cmp_kv = _compress(cmp_chunks, pos, jnp.stack([cmp_w1_k, cmp_w1_v]).astype(BF16),
                       jnp.stack([cmp_w2_k, cmp_w2_v]).astype(BF16))

    w_bt = w_in_b[0].T
    n_main = (1 + NSA_BRANCHES) * ATT_WIDTH
    wg = w_bt[n_main:].reshape(NSA_BRANCHES, groups, heads, D_MODEL).transpose(1, 0, 2, 3)
    wg = jnp.pad(wg.reshape(groups, NSA_BRANCHES * heads, D_MODEL),
                 ((0, 0), (0, HEAD_DIM - NSA_BRANCHES * heads), (0, 0))).reshape(groups * HEAD_DIM, D_MODEL)
    kinds_b = [(kind,) * tile_slots for kind in ("tq",) + ("ts",) * NSA_BRANCHES for _ in range(tiles_per_part)]
    qz_t, gate_logits = _norm_matmul(h1, norm_pre[1], w_bt, wg.T.astype(BF16), tile_kinds=kinds_b,
                                     w_transposed=True)
    att_b = _nsa_attn(qz_t, gate_logits.reshape(batch, SEQ, groups * HEAD_DIM), kv_nat, kv_t, cmp_kv,
                      table, table_c)
    out = _out_proj(att_b.reshape(m, ATT_WIDTH), w_out_b[0], h1, norm_post[1])
    return out.reshape(batch, SEQ, D_MODEL)
```

```python
import functools
import math

import numpy as np
import jax
import jax.numpy as jnp
from jax import lax
from jax.experimental import pallas as pl
from jax.experimental.pallas import tpu as pltpu

F32 = jnp.float32
BF16 = jnp.bfloat16

D_MODEL = 2048
SEQ = 2048
N_HEADS = 16
HEAD_DIM = 128
ATT_WIDTH = N_HEADS * HEAD_DIM
ATT_SCALE = HEAD_DIM ** -0.5
DIL_PATTERNS = ((128, 1), (512, 4), (2048, 16))
REL_BUCKETS = 32
REL_MAX_EXACT = 16
REL_MAX_DISTANCE = 2048
NSA_KV_GROUPS = 4
NSA_HEADS_PER_GROUP = 4
NSA_BRANCHES = 3
CMP_BLOCK = 32
CMP_STRIDE = 16
CMP_HIDDEN = 256
SLC_BLOCK = 64
SLC_TOP_N = 16
WIN_SIZE = 512
RMS_EPS = 1e-6
NEG = -1e30
FORCE_SCORE = 1e9
LOG2E = math.log2(math.e)

TILE = 256
N_TILES = SEQ // TILE
N_PAIRS = N_TILES // 2
CAUSAL_STEPS = N_TILES + 1
STEP_UNROLL = 3
N_CMP = SEQ // CMP_STRIDE
CMP_CHUNK = CMP_STRIDE * HEAD_DIM
N_SLC = SEQ // SLC_BLOCK
WIN_TILES = -(-(WIN_SIZE - 1) // TILE) + 1
HEADS_PER_STEP = 4
VT_ROWS = HEAD_DIM + 16
MXU_COLS = 256

TAB_A, TAB_S, TAB_W = 0, N_TILES, 2 * N_TILES
TAB_W_ROWS = 4
TAB_ROWS = 2 * N_TILES + TAB_W_ROWS
GATHER_MIN_BUCKETS = 6
VMEM_LIMIT = 52 * 1024 * 1024


def _np_bucket(dist):
    n = np.maximum(dist, 0)
    nf = np.maximum(n, 1).astype(np.float32)
    log_b = REL_MAX_EXACT + (
        np.log(nf / np.float32(REL_MAX_EXACT)) / np.float32(math.log(REL_MAX_DISTANCE / REL_MAX_EXACT))
        * np.float32(REL_BUCKETS - REL_MAX_EXACT)).astype(np.int32)
    return np.where(n < REL_MAX_EXACT, n, np.minimum(log_b, REL_BUCKETS - 1)).astype(np.int32)


@functools.lru_cache(maxsize=None)
def _static_maps():
    ki = np.arange(TILE)[:, None]
    qi = np.arange(TILE)[None, :]
    dist = TILE * np.arange(N_TILES)[:, None, None] + qi[None] - ki[None]
    bk_t = _np_bucket(dist)
    mult = np.zeros(dist.shape, np.int64)
    for window, dil in DIL_PATTERNS:
        mult += ((dist % dil == 0) & (dist <= window)).astype(np.int64)
    base_a = np.where((dist >= 0) & (mult > 0), np.log2(np.maximum(mult, 1)), NEG)
    base_s = np.where(dist >= 0, 0.0, NEG)
    base_w = np.full((TAB_W_ROWS, TILE, TILE), NEG)
    base_w[:WIN_TILES] = np.where((dist >= 0) & (dist < WIN_SIZE), 0.0, NEG)[:WIN_TILES]
    base_t = np.concatenate([base_a, base_s, base_w]).astype(np.float32)
    ci = np.arange(N_CMP)[:, None]
    t = TILE * np.arange(N_TILES)[:, None, None] + qi[None]
    dist_c = t - (CMP_STRIDE * ci[None] + CMP_BLOCK - 1)
    bk_c = _np_bucket(dist_c)
    base_c = np.where((dist_c >= 0) & (ci[None] < N_CMP - 1), 0.0, NEG).astype(np.float32)
    cs = np.arange(N_CMP)[None, :] * CMP_STRIDE
    sj = np.arange(N_SLC)[:, None] * SLC_BLOCK
    ov_t = ((cs < sj + SLC_BLOCK) & (cs + CMP_BLOCK > sj) & (np.arange(N_CMP)[None, :] < N_CMP - 1))
    onehot = np.zeros((SEQ, HEAD_DIM), np.float32)
    onehot[np.arange(SEQ), np.arange(SEQ) // SLC_BLOCK] = 1.0
    return bk_t.astype(np.int32), base_t, bk_c.astype(np.int32), base_c, ov_t.astype(np.float32), onehot


def _bias_kernel(tab_ref, tabt_ref, bkt_ref, baset_ref, bkc_ref, basec_ref, out_ref, outc_ref, *,
                 present_t, present_c):
    h = pl.program_id(0)
    tv = [tab_ref[b, h] * LOG2E for b in range(REL_BUCKETS)]
    lanes = 128
    row = jnp.broadcast_to(tabt_ref[pl.ds(h, 1), :] * LOG2E, (TILE, lanes))

    def lookup(bk, present):
        if len(present) > GATHER_MIN_BUCKETS:
            parts = [jnp.take_along_axis(row[:bk.shape[0]], bk[:, c:c + lanes], axis=1)
                     for c in range(0, bk.shape[1], lanes)]
            return jnp.concatenate(parts, axis=1)
        val = jnp.full(bk.shape, tv[present[0]], F32)
        for b in present[1:]:
            val = jnp.where(bk == b, tv[b], val)
        return val

    for d in range(N_TILES):
        g = lookup(bkt_ref[d], present_t[d])
        out_ref[TAB_A + d] = g + baset_ref[TAB_A + d]
        out_ref[TAB_S + d] = g + baset_ref[TAB_S + d]
        if d < TAB_W_ROWS:
            out_ref[TAB_W + d] = g + baset_ref[TAB_W + d]
        outc_ref[d] = lookup(bkc_ref[d], present_c[d]) + basec_ref[d]


def _bias_tables(rel_table):
    bk_t, base_t, bk_c, base_c, _, _ = _static_maps()
    present_t = tuple(tuple(int(b) for b in np.unique(bk_t[d])) for d in range(N_TILES))
    present_c = tuple(tuple(int(b) for b in np.unique(bk_c[d])) for d in range(N_TILES))
    return pl.pallas_call(
        functools.partial(_bias_kernel, present_t=present_t, present_c=present_c),
        grid=(N_HEADS,),
        in_specs=[
            pl.BlockSpec(memory_space=pltpu.SMEM),
            pl.BlockSpec((N_HEADS, 128), lambda h: (0, 0)),
            pl.BlockSpec((N_TILES, TILE, TILE), lambda h: (0, 0, 0)),
            pl.BlockSpec((TAB_ROWS, TILE, TILE), lambda h: (0, 0, 0)),
            pl.BlockSpec((N_TILES, N_CMP, TILE), lambda h: (0, 0, 0)),
            pl.BlockSpec((N_TILES, N_CMP, TILE), lambda h: (0, 0, 0)),
        ],
        out_specs=[
            pl.BlockSpec((None, TAB_ROWS, TILE, TILE), lambda h: (h, 0, 0, 0)),
            pl.BlockSpec((None, N_TILES, N_CMP, TILE), lambda h: (h, 0, 0, 0)),
        ],
        out_shape=[
            jax.ShapeDtypeStruct((N_HEADS, TAB_ROWS, TILE, TILE), F32),
            jax.ShapeDtypeStruct((N_HEADS, N_TILES, N_CMP, TILE), F32),
        ],
        compiler_params=pltpu.CompilerParams(dimension_semantics=("arbitrary",),
                                             vmem_limit_bytes=VMEM_LIMIT),
        name="bias_tables",
    )(rel_table, jnp.pad(rel_table.T, ((0, 0), (0, 128 - REL_BUCKETS))),
      jnp.asarray(bk_t), jnp.asarray(base_t), jnp.asarray(bk_c), jnp.asarray(base_c))


def _silu(z):
    half = 0.5 * z
    return half + half * jnp.tanh(half)


TRANSPOSED_KINDS = ("t", "tq", "ts")


def _nm_kernel(*refs, tile_kinds, gate, w_transposed):
    refs = list(refs)
    x_ref, g_ref, w_ref = refs[:3]
    del refs[:3]
    wg_ref = refs.pop(0) if gate else None
    kinds_used = {k for kinds in tile_kinds for k in kinds}
    on_ref = refs.pop(0) if "n" in kinds_used else None
    ot_ref = refs.pop(0) if kinds_used & set(TRANSPOSED_KINDS) else None
    og_ref = refs.pop(0) if gate else None
    oc_ref = refs.pop(0) if "c" in kinds_used else None
    xn_ref = refs.pop(0)
    res_ref = refs.pop(0) if "c" in kinds_used else None
    j = pl.program_id(1)

    @pl.when(j == 0)
    def _():
        x = x_ref[...]
        ms = jnp.mean(x * x, axis=-1, keepdims=True)
        xn_ref[...] = (x * lax.rsqrt(ms + RMS_EPS) * g_ref[...]).astype(BF16)
        if gate:
            og_ref[...] = jnp.dot(xn_ref[...], wg_ref[...], preferred_element_type=F32)

    def product(kinds):
        pos = {"n": 0, "t": 0}
        for cc in range(len(kinds) // 2):
            if w_transposed:
                w_cols = w_ref[cc * MXU_COLS:(cc + 1) * MXU_COLS, :].astype(BF16)
                res = lax.dot_general(xn_ref[...], w_cols, (((1,), (1,)), ((), ())), preferred_element_type=F32)
            else:
                w_cols = w_ref[:, cc * MXU_COLS:(cc + 1) * MXU_COLS].astype(BF16)
                res = jnp.dot(xn_ref[...], w_cols, preferred_element_type=F32)
            for u in range(2):
                c = 2 * cc + u
                piece = res[:, u * HEAD_DIM:(u + 1) * HEAD_DIM]
                kind = kinds[c]
                if kind == "n":
                    on_ref[pos["n"]] = piece.astype(BF16)
                    pos["n"] += 1
                elif kind == "c":
                    res_ref[c] = piece
                else:
                    if kind == "tq":
                        piece = piece * (ATT_SCALE * LOG2E)
                    elif kind == "ts":
                        piece = _silu(piece)
                    ot_ref[pos["t"]] = piece.T.astype(BF16)
                    pos["t"] += 1
        if "c" in kinds:
            rows = res_ref.shape[1] // CMP_STRIDE
            for c in range(len(kinds)):
                for i in range(CMP_STRIDE):
                    oc_ref[c, :, i * HEAD_DIM:(i + 1) * HEAD_DIM] = res_ref[
                        c, pl.ds(i, rows, stride=CMP_STRIDE), :].astype(BF16)

    for kinds in sorted(set(tile_kinds)):
        tiles = [t for t, k in enumerate(tile_kinds) if k == kinds]
        cond = functools.reduce(lambda a, b: a | b, [j == t for t in tiles])
        pl.when(cond)(functools.partial(product, kinds))


def _norm_matmul(x2d, gain, w, wg_bf=None, *, tile_kinds, w_transposed=False, tm=1024, tn=1024):
    m, d = x2d.shape
    slots = tn // HEAD_DIM
    batch = m // SEQ
    spb = SEQ // tm
    gate = wg_bf is not None
    assert all(len(k) == slots for k in tile_kinds)

    def family(match):
        counts = [sum(1 for k in kinds if match(k)) for kinds in tile_kinds]
        tiles = [t for t, c in enumerate(counts) if c]
        per_tile = counts[tiles[0]] if tiles else 0
        assert all(counts[t] == per_tile for t in tiles)

        def block(j):
            return jnp.minimum(sum(jnp.where(j > t, 1, 0) for t in tiles), len(tiles) - 1)
        return tiles, per_tile, block

    n_tiles, n_per, n_block = family(lambda k: k == "n")
    t_tiles, t_per, t_block = family(lambda k: k in TRANSPOSED_KINDS)
    c_tiles, c_per, _ = family(lambda k: k == "c")
    assert c_tiles in ([], [0]) and c_per in (0, slots)

    in_specs = [
        pl.BlockSpec((tm, d), lambda i, j: (i, 0)),
        pl.BlockSpec((1, d), lambda i, j: (0, 0)),
        pl.BlockSpec((tn, d), lambda i, j: (j, 0)) if w_transposed else pl.BlockSpec((d, tn), lambda i, j: (0, j)),
    ]
    args = [x2d, gain.reshape(1, d), w]
    out_shape, out_specs = [], []
    scratch = [pltpu.VMEM((tm, d), BF16)]
    if gate:
        in_specs.append(pl.BlockSpec((d, wg_bf.shape[1]), lambda i, j: (0, 0)))
        args.append(wg_bf)
    if n_tiles:
        out_shape.append(jax.ShapeDtypeStruct((batch, n_per * len(n_tiles), SEQ, HEAD_DIM), BF16))
        out_specs.append(pl.BlockSpec((None, n_per, tm, HEAD_DIM),
                                      lambda i, j: (i // spb, n_block(j), i % spb, 0)))
    if t_tiles:
        out_shape.append(jax.ShapeDtypeStruct((batch, t_per * len(t_tiles), HEAD_DIM, SEQ), BF16))
        out_specs.append(pl.BlockSpec((None, t_per, HEAD_DIM, tm),
                                      lambda i, j: (i // spb, t_block(j), 0, i % spb)))
    if gate:
        out_shape.append(jax.ShapeDtypeStruct((m, wg_bf.shape[1]), F32))
        out_specs.append(pl.BlockSpec((tm, wg_bf.shape[1]), lambda i, j: (i, 0)))
    if c_tiles:
        out_shape.append(jax.ShapeDtypeStruct((batch, slots, N_CMP, CMP_CHUNK), BF16))
        out_specs.append(pl.BlockSpec((None, slots, tm // CMP_STRIDE, CMP_CHUNK),
                                      lambda i, j: (i // spb, 0, i % spb, 0)))
        scratch.append(pltpu.VMEM((slots, tm, HEAD_DIM), F32))
    return pl.pallas_call(
        functools.partial(_nm_kernel, tile_kinds=tuple(tuple(k) for k in tile_kinds), gate=gate,
                          w_transposed=w_transposed),
        grid=(m // tm, len(tile_kinds)),
        in_specs=in_specs,
        out_specs=out_specs,
        out_shape=out_shape,
        scratch_shapes=scratch,
        compiler_params=pltpu.CompilerParams(dimension_semantics=("parallel", "arbitrary"),
                                             vmem_limit_bytes=VMEM_LIMIT),
        name="norm_matmul",
    )(*args)


def _op_kernel(a_ref, w_ref, r_ref, g_ref, o_ref, wbf_ref, *, parts):
    @pl.when(pl.program_id(0) == 0)
    def _():
        step = 256
        for r in range(w_ref.shape[0] // step):
            wbf_ref[r * step:(r + 1) * step, :] = w_ref[r * step:(r + 1) * step, :].astype(BF16)

    rows = a_ref.shape[0] // parts
    for r in range(parts):
        sl = slice(r * rows, (r + 1) * rows)
        y = jnp.dot(a_ref[sl, :], wbf_ref[...], preferred_element_type=F32)
        ms = jnp.mean(y * y, axis=-1, keepdims=True)
        o_ref[sl, :] = r_ref[sl, :] + y * lax.rsqrt(ms + RMS_EPS) * g_ref[...]


def _out_proj(a2d, w, resid2d, gain, *, tm=512, parts=2):
    m, k = a2d.shape
    n = w.shape[1]
    return pl.pallas_call(
        functools.partial(_op_kernel, parts=parts),
        grid=(m // tm,),
        in_specs=[
            pl.BlockSpec((tm, k), lambda i: (i, 0)),
            pl.BlockSpec((k, n), lambda i: (0, 0), pipeline_mode=pl.Buffered(1)),
            pl.BlockSpec((tm, n), lambda i: (i, 0)),
            pl.BlockSpec((1, n), lambda i: (0, 0)),
        ],
        out_specs=pl.BlockSpec((tm, n), lambda i: (i, 0)),
        out_shape=jax.ShapeDtypeStruct((m, n), F32),
        scratch_shapes=[pltpu.VMEM((k, n), BF16)],
        compiler_params=pltpu.CompilerParams(dimension_semantics=("arbitrary",),
                                             vmem_limit_bytes=VMEM_LIMIT),
        name="out_proj",
    )(a2d, w, resid2d, gain.reshape(1, n))


def _value_tiles(src_ref, dst_ref):
    for c in range(N_TILES):
        dst_ref[c, 0:HEAD_DIM, :] = src_ref[:, c * TILE:(c + 1) * TILE]
        dst_ref[c, HEAD_DIM:VT_ROWS, :] = jnp.ones((VT_ROWS - HEAD_DIM, TILE), BF16)


def _pair_tiles(n):
    return (n, N_TILES - 1 - n)


def _tile_rows(idx):
    return pl.ds(pl.multiple_of(idx * TILE, TILE), TILE)


def _normalized(acc):
    return acc[0:HEAD_DIM, :] / acc[HEAD_DIM:HEAD_DIM + 1, :]


class _PairSoftmax:
    def __init__(self, step_fn, heads, q_ref, k_fn, vt_fn, bias_fn, s_ref, acc_ref):
        self.step_fn, self.heads, self.q_ref, self.k_fn, self.vt_fn = step_fn, heads, q_ref, k_fn, vt_fn
        self.bias_fn, self.s_ref, self.acc_ref = bias_fn, s_ref, acc_ref

    def initial_max(self):
        neg = (jnp.full((1, TILE), NEG, F32),) * self.heads
        return neg, neg

    def scores(self, t, ms):
        slot, j, b = self.step_fn(t)
        first = slot == 0
        m0, m1 = list(ms[0]), list(ms[1])
        for h in range(self.heads):
            s = (jnp.dot(self.k_fn(h, j), self.q_ref[slot, h], preferred_element_type=F32)
                 + self.bias_fn(h, b))
            self.s_ref[h, t] = s
            cm = jnp.max(s, axis=0, keepdims=True)
            m0[h] = jnp.where(first, jnp.maximum(m0[h], cm), m0[h])
            m1[h] = jnp.where(first, m1[h], jnp.maximum(m1[h], cm))
        return tuple(m0), tuple(m1)

    def clear(self):
        for slot in range(2):
            for h in range(self.heads):
                self.acc_ref[slot, h] = jnp.zeros((VT_ROWS, TILE), F32)

    def weighted(self, t, ms):
        slot, j, _ = self.step_fn(t)
        first = slot == 0
        for h in range(self.heads):
            p = jnp.exp2((self.s_ref[h, t] - jnp.where(first, ms[0][h], ms[1][h])).astype(BF16))
            self.acc_ref[slot, h] += jnp.dot(self.vt_fn(h, j), p, preferred_element_type=F32)

    def run(self, steps):
        ms = lax.fori_loop(0, steps, self.scores, self.initial_max(), unroll=True)
        self.clear()

        def body(t, carry):
            self.weighted(t, ms)
            return carry

        lax.fori_loop(0, steps, body, 0, unroll=STEP_UNROLL)

    def run_unrolled(self, steps):
        ms = self.initial_max()
        for t in range(steps):
            ms = self.scores(t, ms)
        self.clear()
        for t in range(steps):
            self.weighted(t, ms)


def _causal_step(n):
    def step(t):
        first = t <= n
        return (jnp.where(first, 0, 1), jnp.where(first, t, t - n - 1), jnp.where(first, n - t, N_TILES - t))
    return step


def _window_step(n):
    def step(t):
        if isinstance(t, int):
            slot, d = divmod(t, WIN_TILES)
            idx = _pair_tiles(n)[slot]
        else:
            slot = jnp.where(t >= WIN_TILES, 1, 0)
            d = t - WIN_TILES * slot
            idx = jnp.where(slot == 0, n, N_TILES - 1 - n)
        j = idx - d
        return slot, jnp.maximum(j, 0), jnp.where(j >= 0, d, TAB_W_ROWS - 1)
    return step


def _attn_a_kernel(qa_ref, qb_ref, k_ref, v_ref, za_ref, zb_ref, r_ref, o_ref,
                   vt_ref, qt_ref, s_ref, acc_ref):
    n = pl.program_id(2)
    heads = HEADS_PER_STEP

    @pl.when(n == 0)
    def _():
        for h in range(heads):
            _value_tiles(v_ref.at[h], vt_ref.at[h])

    for slot, q_ref in enumerate((qa_ref, qb_ref)):
        for h in range(heads):
            qt_ref[slot, h] = q_ref[h]

    _PairSoftmax(_causal_step(n), heads, qt_ref, lambda h, j: k_ref[h, _tile_rows(j), :],
                 lambda h, j: vt_ref[h, j], lambda h, delta: r_ref[h, delta], s_ref, acc_ref).run(CAUSAL_STEPS)
    for slot, (z_ref, idx) in enumerate(zip((za_ref, zb_ref), _pair_tiles(n))):
        for h in range(heads):
            y = (_normalized(acc_ref[slot, h]) * z_ref[h].astype(F32)).T
            o_ref[_tile_rows(idx), h * HEAD_DIM:(h + 1) * HEAD_DIM] = y.astype(BF16)


def _resident(block_shape, index_map):
    return pl.BlockSpec(block_shape, index_map, pipeline_mode=pl.Buffered(1))


def _attn_a(k_nat, qvz_t, table):
    batch = k_nat.shape[0]
    heads = HEADS_PER_STEP
    hgroups = N_HEADS // heads

    def tile_slot(s, second):
        def index(hg, b, n):
            return (b, s * hgroups + hg, 0, N_TILES - 1 - n if second else n)
        return pl.BlockSpec((None, heads, HEAD_DIM, TILE), index)

    return pl.pallas_call(
        _attn_a_kernel,
        grid=(hgroups, batch, N_PAIRS),
        in_specs=[tile_slot(0, False), tile_slot(0, True),
                  pl.BlockSpec((None, heads, SEQ, HEAD_DIM), lambda hg, b, n: (b, hg, 0, 0)),
                  pl.BlockSpec((None, heads, HEAD_DIM, SEQ), lambda hg, b, n: (b, hgroups + hg, 0, 0)),
                  tile_slot(2, False), tile_slot(2, True),
                  _resident((heads, N_TILES, TILE, TILE), lambda hg, b, n: (hg, TAB_A // N_TILES, 0, 0))],
        out_specs=pl.BlockSpec((None, SEQ, heads * HEAD_DIM), lambda hg, b, n: (b, 0, hg)),
        out_shape=jax.ShapeDtypeStruct((batch, SEQ, ATT_WIDTH), BF16),
        scratch_shapes=[
            pltpu.VMEM((heads, N_TILES, VT_ROWS, TILE), BF16),
            pltpu.VMEM((2, heads, HEAD_DIM, TILE), BF16),
            pltpu.VMEM((heads, CAUSAL_STEPS, TILE, TILE), F32),
            pltpu.VMEM((2, heads, VT_ROWS, TILE), F32),
        ],
        compiler_params=pltpu.CompilerParams(dimension_semantics=("parallel", "parallel", "arbitrary"),
                                             vmem_limit_bytes=VMEM_LIMIT),
        name="dilated_attn",
    )(qvz_t, qvz_t, k_nat, qvz_t, qvz_t, qvz_t, table)


def _cmp_kernel(c_ref, pos_ref, w1_ref, w2_ref, o_ref):
    c = c_ref[...].astype(F32)
    x_lo = (c + pos_ref[0]).astype(BF16)
    x_hi = (c + pos_ref[1]).astype(BF16)
    a = jnp.dot(x_lo, w1_ref[0:CMP_CHUNK, :], preferred_element_type=F32)
    bm = jnp.dot(x_hi, w1_ref[CMP_CHUNK:2 * CMP_CHUNK, :], preferred_element_type=F32)
    hid = jax.nn.gelu(a + pltpu.roll(bm, N_CMP - 1, 0))
    res = jnp.dot(hid.astype(BF16), w2_ref[...], preferred_element_type=F32)
    o_ref[...] = jnp.where(pl.program_id(0) == 1, res.T, res).astype(BF16)


def _compress(chunks, pos, w1_bf, w2_bf):
    batch = chunks.shape[0]
    return pl.pallas_call(
        _cmp_kernel,
        grid=(2, batch, NSA_KV_GROUPS),
        in_specs=[
            pl.BlockSpec((None, None, N_CMP, CMP_CHUNK), lambda t, b, g: (b, t * NSA_KV_GROUPS + g, 0, 0)),
            pl.BlockSpec((None, 2, 1, CMP_CHUNK), lambda t, b, g: (t, 0, 0, 0)),
            pl.BlockSpec((None, 2 * CMP_CHUNK, CMP_HIDDEN), lambda t, b, g: (t, 0, 0)),
            pl.BlockSpec((None, CMP_HIDDEN, HEAD_DIM), lambda t, b, g: (t, 0, 0)),
        ],
        out_specs=pl.BlockSpec((None, None, None, N_CMP, HEAD_DIM), lambda t, b, g: (t, b, g, 0, 0)),
        out_shape=jax.ShapeDtypeStruct((2, batch, NSA_KV_GROUPS, N_CMP, HEAD_DIM), BF16),
        compiler_params=pltpu.CompilerParams(dimension_semantics=("arbitrary", "arbitrary", "arbitrary"),
                                             vmem_limit_bytes=VMEM_LIMIT),
        name="compress_kv",
    )(chunks, pos, w1_bf, w2_bf)


def _nsa_kernel(qa_ref, qb_ref, zca_ref, zcb_ref, zsa_ref, zsb_ref, zwa_ref, zwb_ref, gla_ref, glb_ref,
                kc_ref, vct_ref, ks_ref, vs_ref, kw_ref, vw_ref, rs_ref, rw_ref, rca_ref, rcb_ref,
                ovt_ref, onehot_ref, o_ref,
                kaug_ref, vst_ref, vwt_ref, qa_sc, qw_sc, gates_ref, ycmp_ref, s_ref, accs_ref, accw_ref):
    n = pl.program_id(2)
    heads = NSA_HEADS_PER_GROUP
    tiles = _pair_tiles(n)

    @pl.when(n == 0)
    def _():
        kaug_ref[:, 0:HEAD_DIM] = ks_ref[...]
        kaug_ref[:, HEAD_DIM:2 * HEAD_DIM] = onehot_ref[...]
        _value_tiles(vs_ref, vst_ref)
        _value_tiles(vw_ref, vwt_ref)

    kc = kc_ref[...]
    vct = vct_ref[...]
    for slot, (q_ref, rc_ref, zc_ref, gl_ref) in enumerate(
            ((qa_ref, rca_ref, zca_ref, gla_ref), (qb_ref, rcb_ref, zcb_ref, glb_ref))):
        psum = jnp.zeros((N_CMP, TILE), F32)
        gates_ref[slot] = jax.nn.sigmoid(gl_ref[...].T)
        for h in range(heads):
            qt = q_ref[h]
            qw_sc[slot, h] = qt
            qa_sc[slot, h, 0:HEAD_DIM, :] = qt
            bias = rc_ref[h]
            s = jnp.dot(kc, qt, preferred_element_type=F32) + bias
            m = jnp.max(s, axis=0, keepdims=True)
            p = jnp.where(bias > 0.5 * NEG, jnp.exp2(s - m), 0.0)
            p = p / jnp.maximum(jnp.sum(p, axis=0, keepdims=True), 1e-30)
            psum = psum + p
            o_cmp = jnp.dot(vct, p.astype(BF16), preferred_element_type=F32)
            ycmp_ref[slot, h] = gates_ref[slot, h:h + 1, :] * o_cmp * zc_ref[h].astype(F32)

        imp = jnp.dot(ovt_ref[...], psum, precision=lax.Precision.HIGHEST, preferred_element_type=F32)
        blk = lax.broadcasted_iota(jnp.int32, (N_SLC, TILE), 0)
        tq = tiles[slot] * TILE + lax.broadcasted_iota(jnp.int32, (N_SLC, TILE), 1)
        cur = lax.shift_right_logical(tq, int(math.log2(SLC_BLOCK)))
        forced = (blk == 0) | (blk == cur) | (blk == cur - 1)
        score = jnp.where(forced, FORCE_SCORE, jnp.where(blk > cur, -FORCE_SCORE, imp))
        group = 8
        rows = [score[r:r + group, :] for r in range(0, N_SLC, group)]
        row_idx = lax.broadcasted_iota(jnp.int32, (group, TILE), 0)
        cnts = [jnp.zeros((group, TILE), jnp.int32) for _ in rows]
        for jj in range(N_SLC):
            sj = score[jj:jj + 1, :]
            for r, sr in enumerate(rows):
                lo = r * group
                if lo > jj:
                    beats = jnp.where(sj >= sr, 1, 0)
                elif lo + group - 1 <= jj:
                    beats = jnp.where(sj > sr, 1, 0)
                else:
                    beats = jnp.where(row_idx + lo > jj, jnp.where(sj >= sr, 1, 0), jnp.where(sj > sr, 1, 0))
                cnts[r] = cnts[r] + beats
        cnt = jnp.concatenate(cnts, axis=0)
        mask = jnp.where(cnt < SLC_TOP_N, 0.0, NEG)
        mask = jnp.concatenate([mask, jnp.zeros((HEAD_DIM - N_SLC, TILE), F32)], axis=0).astype(BF16)
        for h in range(heads):
            qa_sc[slot, h, HEAD_DIM:2 * HEAD_DIM, :] = mask

    _PairSoftmax(_causal_step(n), heads, qa_sc, lambda h, j: kaug_ref[_tile_rows(j), :],
                 lambda h, j: vst_ref[j], lambda h, delta: rs_ref[h, delta], s_ref, accs_ref).run(CAUSAL_STEPS)
    _PairSoftmax(_window_step(n), heads, qw_sc, lambda h, j: kw_ref[_tile_rows(j), :],
                 lambda h, j: vwt_ref[j], lambda h, b: rw_ref[h, b], s_ref, accw_ref).run_unrolled(2 * WIN_TILES)

    for slot, (zs_ref, zw_ref) in enumerate(((zsa_ref, zwa_ref), (zsb_ref, zwb_ref))):
        for h in range(heads):
            y = (ycmp_ref[slot, h]
                 + gates_ref[slot, heads + h:heads + h + 1, :] * _normalized(accs_ref[slot, h])
                 * zs_ref[h].astype(F32)
                 + gates_ref[slot, 2 * heads + h:2 * heads + h + 1, :] * _normalized(accw_ref[slot, h])
                 * zw_ref[h].astype(F32))
            o_ref[_tile_rows(tiles[slot]), h * HEAD_DIM:(h + 1) * HEAD_DIM] = y.T.astype(BF16)


def _nsa_attn(qz_t, gate_logits, kv_nat, kv_t, cmp_kv, table, table_c):
    batch = qz_t.shape[0]
    heads = NSA_HEADS_PER_GROUP
    groups = NSA_KV_GROUPS
    _, _, _, _, ov_t, onehot = _static_maps()

    def tile_of(n, second):
        return N_TILES - 1 - n if second else n

    def qz_specs(slot_group):
        return [pl.BlockSpec((None, heads, HEAD_DIM, TILE),
                             lambda g, b, n, second=second: (b, slot_group * groups + g, 0, tile_of(n, second)))
                for second in (False, True)]

    def key_spec(branch):
        return pl.BlockSpec((None, None, SEQ, HEAD_DIM), lambda g, b, n: (b, branch * groups + g, 0, 0))

    def value_spec(branch):
        return pl.BlockSpec((None, None, HEAD_DIM, SEQ), lambda g, b, n: (b, branch * groups + g, 0, 0))

    def cmp_spec(t):
        return pl.BlockSpec((None, None, None, N_CMP, HEAD_DIM), lambda g, b, n: (t, b, g, 0, 0))

    gl_specs = [pl.BlockSpec((None, TILE, HEAD_DIM), lambda g, b, n, second=second: (b, tile_of(n, second), g))
                for second in (False, True)]
    rc_specs = [pl.BlockSpec((heads, None, N_CMP, TILE), lambda g, b, n, second=second: (g, tile_of(n, second), 0, 0))
                for second in (False, True)]

    return pl.pallas_call(
        _nsa_kernel,
        grid=(groups, batch, N_PAIRS),
        in_specs=[
            *qz_specs(0), *qz_specs(1), *qz_specs(2), *qz_specs(3), *gl_specs,
            cmp_spec(0), cmp_spec(1), key_spec(0), value_spec(0), key_spec(1), value_spec(1),
            _resident((heads, N_TILES, TILE, TILE), lambda g, b, n: (g, TAB_S // N_TILES, 0, 0)),
            _resident((heads, TAB_W_ROWS, TILE, TILE), lambda g, b, n: (g, TAB_W // TAB_W_ROWS, 0, 0)),
            *rc_specs,
            _resident((N_SLC, N_CMP), lambda g, b, n: (0, 0)),
            _resident((SEQ, HEAD_DIM), lambda g, b, n: (0, 0)),
        ],
        out_specs=pl.BlockSpec((None, SEQ, heads * HEAD_DIM), lambda g, b, n: (b, 0, g)),
        out_shape=jax.ShapeDtypeStruct((batch, SEQ, ATT_WIDTH), BF16),
        scratch_shapes=[
            pltpu.VMEM((SEQ, 2 * HEAD_DIM), BF16),
            pltpu.VMEM((N_TILES, VT_ROWS, TILE), BF16),
            pltpu.VMEM((N_TILES, VT_ROWS, TILE), BF16),
            pltpu.VMEM((2, heads, 2 * HEAD_DIM, TILE), BF16),
            pltpu.VMEM((2, heads, HEAD_DIM, TILE), BF16),
            pltpu.VMEM((2, HEAD_DIM, TILE), F32),
            pltpu.VMEM((2, heads, HEAD_DIM, TILE), F32),
            pltpu.VMEM((heads, CAUSAL_STEPS, TILE, TILE), F32),
            pltpu.VMEM((2, heads, VT_ROWS, TILE), F32),
            pltpu.VMEM((2, heads, VT_ROWS, TILE), F32),
        ],
        compiler_params=pltpu.CompilerParams(dimension_semantics=("parallel", "parallel", "arbitrary"),
                                             vmem_limit_bytes=VMEM_LIMIT),
        name="nsa_attn",
    )(qz_t, qz_t, qz_t, qz_t, qz_t, qz_t, qz_t, qz_t, gate_logits, gate_logits, cmp_kv, cmp_kv,
      kv_nat, kv_t, kv_nat, kv_t, table, table, table_c, table_c,
      jnp.asarray(ov_t), jnp.asarray(onehot, dtype=BF16))


def kernel(x, norm_pre, norm_post, rel_table, w_in_a, w_out_a, kv_norm, w_kv, cmp_pos_k, cmp_pos_v,
           cmp_w1_k, cmp_w2_k, cmp_w1_v, cmp_w2_v, w_in_b, w_out_b):
    batch = x.shape[0]
    m = batch * SEQ
    heads = NSA_HEADS_PER_GROUP
    groups = NSA_KV_GROUPS
    x2d = x.reshape(m, D_MODEL)
    table, table_c = _bias_tables(rel_table)

    tile_slots = 8
    tiles_per_part = ATT_WIDTH // (tile_slots * HEAD_DIM)

    kinds_a = [(kind,) * tile_slots for kind in ("tq", "n", "t", "ts") for _ in range(tiles_per_part)]
    k_nat, qvz_t = _norm_matmul(x2d, norm_pre[0], w_in_a[0], tile_kinds=kinds_a)
    att_a = _attn_a(k_nat, qvz_t, table)
    h1 = _out_proj(att_a.reshape(m, ATT_WIDTH), w_out_a[0], x2d, norm_post[0])

    kinds_kv = [("c",) * tile_slots] + [("n",) * groups + ("t",) * groups] * 2
    kv_nat, kv_t, cmp_chunks = _norm_matmul(h1, kv_norm, w_kv, tile_kinds=kinds_kv)
    pos = jnp.stack([cmp_pos_k, cmp_pos_v]).reshape(2, 2, 1, CMP_CHUNK)
    cmp_kv = _compress(cmp_chunks, pos, jnp.stack([cmp_w1_k, cmp_w1_v]).astype(BF16),
                       jnp.stack([cmp_w2_k, cmp_w2_v]).astype(BF16))

    w_bt = w_in_b[0].T
    n_main = (1 + NSA_BRANCHES) * ATT_WIDTH
    wg = w_bt[n_main:].reshape(NSA_BRANCHES, groups, heads, D_MODEL).transpose(1, 0, 2, 3)
    wg = jnp.pad(wg.reshape(groups, NSA_BRANCHES * heads, D_MODEL),
                 ((0, 0), (0, HEAD_DIM - NSA_BRANCHES * heads), (0, 0))).reshape(groups * HEAD_DIM, D_MODEL)
    kinds_b = [(kind,) * tile_slots for kind in ("tq",) + ("ts",) * NSA_BRANCHES for _ in range(tiles_per_part)]
    qz_t, gate_logits = _norm_matmul(h1, norm_pre[1], w_bt, wg.T.astype(BF16), tile_kinds=kinds_b,
                                     w_transposed=True)
    att_b = _nsa_attn(qz_t, gate_logits.reshape(batch, SEQ, groups * HEAD_DIM), kv_nat, kv_t, cmp_kv,
                      table, table_c)
    out = _out_proj(att_b.reshape(m, ATT_WIDTH), w_out_b[0], h1, norm_post[1])
    return out.reshape(batch, SEQ, D_MODEL)
```

```python
import functools
import math

import numpy as np
import jax
import jax.numpy as jnp
from jax import lax
from jax.experimental import pallas as pl
from jax.experimental.pallas import tpu as pltpu

F32 = jnp.float32
BF16 = jnp.bfloat16

D_MODEL = 2048
SEQ = 2048
N_HEADS = 16
HEAD_DIM = 128
ATT_WIDTH = N_HEADS * HEAD_DIM
ATT_SCALE = HEAD_DIM ** -0.5
DIL_PATTERNS = ((128, 1), (512, 4), (2048, 16))
REL_BUCKETS = 32
REL_MAX_EXACT = 16
REL_MAX_DISTANCE = 2048
NSA_KV_GROUPS = 4
NSA_HEADS_PER_GROUP = 4
NSA_BRANCHES = 3
CMP_BLOCK = 32
CMP_STRIDE = 16
CMP_HIDDEN = 256
SLC_BLOCK = 64
SLC_TOP_N = 16
WIN_SIZE = 512
RMS_EPS = 1e-6
NEG = -1e30
FORCE_SCORE = 1e9
LOG2E = math.log2(math.e)

TILE = 256
N_TILES = SEQ // TILE
N_PAIRS = N_TILES // 2
CAUSAL_STEPS = N_TILES + 1
STEP_UNROLL = 3
N_CMP = SEQ // CMP_STRIDE
CMP_CHUNK = CMP_STRIDE * HEAD_DIM
N_SLC = SEQ // SLC_BLOCK
WIN_TILES = -(-(WIN_SIZE - 1) // TILE) + 1
HEADS_PER_STEP = 4
VT_ROWS = HEAD_DIM + 16
MXU_COLS = 256

TAB_A, TAB_S, TAB_W = 0, N_TILES, 2 * N_TILES
TAB_W_ROWS = 4
TAB_ROWS = 2 * N_TILES + TAB_W_ROWS
GATHER_MIN_BUCKETS = 6
VMEM_LIMIT = 54 * 1024 * 1024


def _np_bucket(dist):
    n = np.maximum(dist, 0)
    nf = np.maximum(n, 1).astype(np.float32)
    log_b = REL_MAX_EXACT + (
        np.log(nf / np.float32(REL_MAX_EXACT)) / np.float32(math.log(REL_MAX_DISTANCE / REL_MAX_EXACT))
        * np.float32(REL_BUCKETS - REL_MAX_EXACT)).astype(np.int32)
    return np.where(n < REL_MAX_EXACT, n, np.minimum(log_b, REL_BUCKETS - 1)).astype(np.int32)


@functools.lru_cache(maxsize=None)
def _static_maps():
    ki = np.arange(TILE)[:, None]
    qi = np.arange(TILE)[None, :]
    dist = TILE * np.arange(N_TILES)[:, None, None] + qi[None] - ki[None]
    bk_t = _np_bucket(dist)
    mult = np.zeros(dist.shape, np.int64)
    for window, dil in DIL_PATTERNS:
        mult += ((dist % dil == 0) & (dist <= window)).astype(np.int64)
    base_a = np.where((dist >= 0) & (mult > 0), np.log2(np.maximum(mult, 1)), NEG)
    base_s = np.where(dist >= 0, 0.0, NEG)
    base_w = np.full((TAB_W_ROWS, TILE, TILE), NEG)
    base_w[:WIN_TILES] = np.where((dist >= 0) & (dist < WIN_SIZE), 0.0, NEG)[:WIN_TILES]
    base_t = np.concatenate([base_a, base_s, base_w]).astype(np.float32)
    ci = np.arange(N_CMP)[:, None]
    t = TILE * np.arange(N_TILES)[:, None, None] + qi[None]
    dist_c = t - (CMP_STRIDE * ci[None] + CMP_BLOCK - 1)
    bk_c = _np_bucket(dist_c)
    base_c = np.where((dist_c >= 0) & (ci[None] < N_CMP - 1), 0.0, NEG).astype(np.float32)
    cs = np.arange(N_CMP)[None, :] * CMP_STRIDE
    sj = np.arange(N_SLC)[:, None] * SLC_BLOCK
    ov_t = ((cs < sj + SLC_BLOCK) & (cs + CMP_BLOCK > sj) & (np.arange(N_CMP)[None, :] < N_CMP - 1))
    onehot = np.zeros((SEQ, HEAD_DIM), np.float32)
    onehot[np.arange(SEQ), np.arange(SEQ) // SLC_BLOCK] = 1.0
    return bk_t.astype(np.int32), base_t, bk_c.astype(np.int32), base_c, ov_t.astype(np.float32), onehot


def _bias_kernel(tab_ref, tabt_ref, bkt_ref, baset_ref, bkc_ref, basec_ref, out_ref, outc_ref, *,
                 present_t, present_c):
    h = pl.program_id(0)
    tv = [tab_ref[b, h] * LOG2E for b in range(REL_BUCKETS)]
    lanes = 128
    row = jnp.broadcast_to(tabt_ref[pl.ds(h, 1), :] * LOG2E, (TILE, lanes))

    def lookup(bk, present):
        if len(present) > GATHER_MIN_BUCKETS:
            parts = [jnp.take_along_axis(row[:bk.shape[0]], bk[:, c:c + lanes], axis=1)
                     for c in range(0, bk.shape[1], lanes)]
            return jnp.concatenate(parts, axis=1)
        val = jnp.full(bk.shape, tv[present[0]], F32)
        for b in present[1:]:
            val = jnp.where(bk == b, tv[b], val)
        return val

    for d in range(N_TILES):
        g = lookup(bkt_ref[d], present_t[d])
        out_ref[TAB_A + d] = g + baset_ref[TAB_A + d]
        out_ref[TAB_S + d] = g + baset_ref[TAB_S + d]
        if d < TAB_W_ROWS:
            out_ref[TAB_W + d] = g + baset_ref[TAB_W + d]
        outc_ref[d] = lookup(bkc_ref[d], present_c[d]) + basec_ref[d]


def _bias_tables(rel_table):
    bk_t, base_t, bk_c, base_c, _, _ = _static_maps()
    present_t = tuple(tuple(int(b) for b in np.unique(bk_t[d])) for d in range(N_TILES))
    present_c = tuple(tuple(int(b) for b in np.unique(bk_c[d])) for d in range(N_TILES))
    return pl.pallas_call(
        functools.partial(_bias_kernel, present_t=present_t, present_c=present_c),
        grid=(N_HEADS,),
        in_specs=[
            pl.BlockSpec(memory_space=pltpu.SMEM),
            pl.BlockSpec((N_HEADS, 128), lambda h: (0, 0)),
            pl.BlockSpec((N_TILES, TILE, TILE), lambda h: (0, 0, 0)),
            pl.BlockSpec((TAB_ROWS, TILE, TILE), lambda h: (0, 0, 0)),
            pl.BlockSpec((N_TILES, N_CMP, TILE), lambda h: (0, 0, 0)),
            pl.BlockSpec((N_TILES, N_CMP, TILE), lambda h: (0, 0, 0)),
        ],
        out_specs=[
            pl.BlockSpec((None, TAB_ROWS, TILE, TILE), lambda h: (h, 0, 0, 0)),
            pl.BlockSpec((None, N_TILES, N_CMP, TILE), lambda h: (h, 0, 0, 0)),
        ],
        out_shape=[
            jax.ShapeDtypeStruct((N_HEADS, TAB_ROWS, TILE, TILE), F32),
            jax.ShapeDtypeStruct((N_HEADS, N_TILES, N_CMP, TILE), F32),
        ],
        compiler_params=pltpu.CompilerParams(dimension_semantics=("arbitrary",),
                                             vmem_limit_bytes=VMEM_LIMIT),
        name="bias_tables",
    )(rel_table, jnp.pad(rel_table.T, ((0, 0), (0, 128 - REL_BUCKETS))),
      jnp.asarray(bk_t), jnp.asarray(base_t), jnp.asarray(bk_c), jnp.asarray(base_c))


def _silu(z):
    half = 0.5 * z
    return half + half * jnp.tanh(half)


TRANSPOSED_KINDS = ("t", "tq", "ts")


def _nm_kernel(*refs, tile_kinds, gate, w_transposed):
    refs = list(refs)
    x_ref, g_ref, w_ref = refs[:3]
    del refs[:3]
    wg_ref = refs.pop(0) if gate else None
    kinds_used = {k for kinds in tile_kinds for k in kinds}
    on_ref = refs.pop(0) if "n" in kinds_used else None
    ot_ref = refs.pop(0) if kinds_used & set(TRANSPOSED_KINDS) else None
    og_ref = refs.pop(0) if gate else None
    oc_ref = refs.pop(0) if "c" in kinds_used else None
    xn_ref = refs.pop(0)
    res_ref = refs.pop(0)
    j = pl.program_id(1)
    n_tiles = len(tile_kinds)
    n_groups = len(tile_kinds[0]) // 2

    @pl.when(j == 0)
    def _():
        x = x_ref[...]
        ms = jnp.mean(x * x, axis=-1, keepdims=True)
        xn_ref[...] = (x * lax.rsqrt(ms + RMS_EPS) * g_ref[...]).astype(BF16)
        if gate:
            og_ref[...] = jnp.dot(xn_ref[...], wg_ref[...], preferred_element_type=F32)

    def product(cc):
        if w_transposed:
            w_cols = w_ref[cc * MXU_COLS:(cc + 1) * MXU_COLS, :].astype(BF16)
            return lax.dot_general(xn_ref[...], w_cols, (((1,), (1,)), ((), ())), preferred_element_type=F32)
        w_cols = w_ref[:, cc * MXU_COLS:(cc + 1) * MXU_COLS].astype(BF16)
        return jnp.dot(xn_ref[...], w_cols, preferred_element_type=F32)

    def store(kinds, cc, pos):
        for c in (2 * cc, 2 * cc + 1):
            kind = kinds[c]
            if kind == "n":
                on_ref[pos["n"]] = res_ref[c].astype(BF16)
                pos["n"] += 1
            elif kind == "c":
                rows = res_ref.shape[1] // CMP_STRIDE
                for i in range(CMP_STRIDE):
                    oc_ref[c, :, i * HEAD_DIM:(i + 1) * HEAD_DIM] = res_ref[
                        c, pl.ds(i, rows, stride=CMP_STRIDE), :].astype(BF16)
            else:
                piece = res_ref[c]
                if kind == "tq":
                    piece = piece * (ATT_SCALE * LOG2E)
                elif kind == "ts":
                    piece = _silu(piece)
                ot_ref[pos["t"]] = piece.T.astype(BF16)
                pos["t"] += 1

    def step(prev_kinds, multiply):
        pos = {"n": 0, "t": 0}
        for cc in range(n_groups):
            res = product(cc) if multiply else None
            if prev_kinds is not None:
                store(prev_kinds, cc, pos)
            if multiply:
                res_ref[2 * cc] = res[:, 0:HEAD_DIM]
                res_ref[2 * cc + 1] = res[:, HEAD_DIM:2 * HEAD_DIM]

    pl.when(j == 0)(functools.partial(step, None, True))
    for kinds in sorted(set(tile_kinds)):
        after = [t + 1 for t, k in enumerate(tile_kinds) if k == kinds]
        inner = [t for t in after if t < n_tiles]
        if inner:
            cond = functools.reduce(lambda a, b: a | b, [j == t for t in inner])
            pl.when(cond)(functools.partial(step, kinds, True))
        if n_tiles in after:
            pl.when(j == n_tiles)(functools.partial(step, kinds, False))


def _norm_matmul(x2d, gain, w, wg_bf=None, *, tile_kinds, w_transposed=False, tm=1024, tn=1024):
    m, d = x2d.shape
    slots = tn // HEAD_DIM
    batch = m // SEQ
    spb = SEQ // tm
    gate = wg_bf is not None
    last = len(tile_kinds) - 1
    assert all(len(k) == slots for k in tile_kinds)

    def family(match):
        counts = [sum(1 for k in kinds if match(k)) for kinds in tile_kinds]
        tiles = [t for t, c in enumerate(counts) if c]
        per_tile = counts[tiles[0]] if tiles else 0
        assert all(counts[t] == per_tile for t in tiles)

        def block(j):
            return jnp.minimum(sum(jnp.where(j - 1 > t, 1, 0) for t in tiles), len(tiles) - 1)
        return tiles, per_tile, block

    n_tiles, n_per, n_block = family(lambda k: k == "n")
    t_tiles, t_per, t_block = family(lambda k: k in TRANSPOSED_KINDS)
    c_tiles, c_per, _ = family(lambda k: k == "c")
    assert c_tiles in ([], [0]) and c_per in (0, slots)

    in_specs = [
        pl.BlockSpec((tm, d), lambda i, j: (i, 0)),
        pl.BlockSpec((1, d), lambda i, j: (0, 0)),
        (pl.BlockSpec((tn, d), lambda i, j: (jnp.minimum(j, last), 0)) if w_transposed
         else pl.BlockSpec((d, tn), lambda i, j: (0, jnp.minimum(j, last)))),
    ]
    args = [x2d, gain.reshape(1, d), w]
    out_shape, out_specs = [], []
    scratch = [pltpu.VMEM((tm, d), BF16), pltpu.VMEM((slots, tm, HEAD_DIM), F32)]
    if gate:
        in_specs.append(pl.BlockSpec((d, wg_bf.shape[1]), lambda i, j: (0, 0), pipeline_mode=pl.Buffered(1)))
        args.append(wg_bf)
    if n_tiles:
        out_shape.append(jax.ShapeDtypeStruct((batch, n_per * len(n_tiles), SEQ, HEAD_DIM), BF16))
        out_specs.append(pl.BlockSpec((None, n_per, tm, HEAD_DIM),
                                      lambda i, j: (i // spb, n_block(j), i % spb, 0)))
    if t_tiles:
        out_shape.append(jax.ShapeDtypeStruct((batch, t_per * len(t_tiles), HEAD_DIM, SEQ), BF16))
        out_specs.append(pl.BlockSpec((None, t_per, HEAD_DIM, tm),
                                      lambda i, j: (i // spb, t_block(j), 0, i % spb)))
    if gate:
        out_shape.append(jax.ShapeDtypeStruct((m, wg_bf.shape[1]), F32))
        out_specs.append(pl.BlockSpec((tm, wg_bf.shape[1]), lambda i, j: (i, 0)))
    if c_tiles:
        out_shape.append(jax.ShapeDtypeStruct((batch, slots, N_CMP, CMP_CHUNK), BF16))
        out_specs.append(pl.BlockSpec((None, slots, tm // CMP_STRIDE, CMP_CHUNK),
                                      lambda i, j: (i // spb, 0, i % spb, 0)))
    return pl.pallas_call(
        functools.partial(_nm_kernel, tile_kinds=tuple(tuple(k) for k in tile_kinds), gate=gate,
                          w_transposed=w_transposed),
        grid=(m // tm, len(tile_kinds) + 1),
        in_specs=in_specs,
        out_specs=out_specs,
        out_shape=out_shape,
        scratch_shapes=scratch,
        compiler_params=pltpu.CompilerParams(dimension_semantics=("parallel", "arbitrary"),
                                             vmem_limit_bytes=VMEM_LIMIT),
        name="norm_matmul",
    )(*args)


def _op_kernel(a_ref, w_ref, r_ref, g_ref, o_ref, wbf_ref, *, parts):
    @pl.when(pl.program_id(0) == 0)
    def _():
        step = 256
        for r in range(w_ref.shape[0] // step):
            wbf_ref[r * step:(r + 1) * step, :] = w_ref[r * step:(r + 1) * step, :].astype(BF16)

    rows = a_ref.shape[0] // parts
    for r in range(parts):
        sl = slice(r * rows, (r + 1) * rows)
        y = jnp.dot(a_ref[sl, :], wbf_ref[...], preferred_element_type=F32)
        ms = jnp.mean(y * y, axis=-1, keepdims=True)
        o_ref[sl, :] = r_ref[sl, :] + y * lax.rsqrt(ms + RMS_EPS) * g_ref[...]


def _out_proj(a2d, w, resid2d, gain, *, tm=512, parts=2):
    m, k = a2d.shape
    n = w.shape[1]
    return pl.pallas_call(
        functools.partial(_op_kernel, parts=parts),
        grid=(m // tm,),
        in_specs=[
            pl.BlockSpec((tm, k), lambda i: (i, 0)),
            pl.BlockSpec((k, n), lambda i: (0, 0), pipeline_mode=pl.Buffered(1)),
            pl.BlockSpec((tm, n), lambda i: (i, 0)),
            pl.BlockSpec((1, n), lambda i: (0, 0)),
        ],
        out_specs=pl.BlockSpec((tm, n), lambda i: (i, 0)),
        out_shape=jax.ShapeDtypeStruct((m, n), F32),
        scratch_shapes=[pltpu.VMEM((k, n), BF16)],
        compiler_params=pltpu.CompilerParams(dimension_semantics=("arbitrary",),
                                             vmem_limit_bytes=VMEM_LIMIT),
        name="out_proj",
    )(a2d, w, resid2d, gain.reshape(1, n))


def _value_tiles(src_ref, dst_ref):
    for c in range(N_TILES):
        dst_ref[c, 0:HEAD_DIM, :] = src_ref[:, c * TILE:(c + 1) * TILE]
        dst_ref[c, HEAD_DIM:VT_ROWS, :] = jnp.ones((VT_ROWS - HEAD_DIM, TILE), BF16)


def _pair_tiles(n):
    return (n, N_TILES - 1 - n)


def _tile_rows(idx):
    return pl.ds(pl.multiple_of(idx * TILE, TILE), TILE)


def _normalized(acc):
    return acc[0:HEAD_DIM, :] / acc[HEAD_DIM:HEAD_DIM + 1, :]


class _PairSoftmax:
    def __init__(self, step_fn, heads, q_ref, k_fn, vt_fn, bias_fn, s_ref, acc_ref):
        self.step_fn, self.heads, self.q_ref, self.k_fn, self.vt_fn = step_fn, heads, q_ref, k_fn, vt_fn
        self.bias_fn, self.s_ref, self.acc_ref = bias_fn, s_ref, acc_ref

    def initial_max(self):
        neg = (jnp.full((1, TILE), NEG, F32),) * self.heads
        return neg, neg

    def scores(self, t, ms):
        slot, j, b = self.step_fn(t)
        first = slot == 0
        m0, m1 = list(ms[0]), list(ms[1])
        for h in range(self.heads):
            s = (jnp.dot(self.k_fn(h, j), self.q_ref[slot, h], preferred_element_type=F32)
                 + self.bias_fn(h, b))
            self.s_ref[h, t] = s
            cm = jnp.max(s, axis=0, keepdims=True)
            m0[h] = jnp.where(first, jnp.maximum(m0[h], cm), m0[h])
            m1[h] = jnp.where(first, m1[h], jnp.maximum(m1[h], cm))
        return tuple(m0), tuple(m1)

    def clear(self):
        for slot in range(2):
            for h in range(self.heads):
                self.acc_ref[slot, h] = jnp.zeros((VT_ROWS, TILE), F32)

    def weighted(self, t, ms):
        slot, j, _ = self.step_fn(t)
        first = slot == 0
        for h in range(self.heads):
            p = jnp.exp2((self.s_ref[h, t] - jnp.where(first, ms[0][h], ms[1][h])).astype(BF16))
            self.acc_ref[slot, h] += jnp.dot(self.vt_fn(h, j), p, preferred_element_type=F32)

    def run(self, steps):
        ms = lax.fori_loop(0, steps, self.scores, self.initial_max(), unroll=True)
        self.clear()

        def body(t, carry):
            self.weighted(t, ms)
            return carry

        lax.fori_loop(0, steps, body, 0, unroll=STEP_UNROLL)

    def run_unrolled(self, steps):
        ms = self.initial_max()
        for t in range(steps):
            ms = self.scores(t, ms)
        self.clear()
        for t in range(steps):
            self.weighted(t, ms)


def _causal_step(n):
    def step(t):
        first = t <= n
        return (jnp.where(first, 0, 1), jnp.where(first, t, t - n - 1), jnp.where(first, n - t, N_TILES - t))
    return step


def _window_step(n):
    def step(t):
        if isinstance(t, int):
            slot, d = divmod(t, WIN_TILES)
            idx = _pair_tiles(n)[slot]
        else:
            slot = jnp.where(t >= WIN_TILES, 1, 0)
            d = t - WIN_TILES * slot
            idx = jnp.where(slot == 0, n, N_TILES - 1 - n)
        j = idx - d
        return slot, jnp.maximum(j, 0), jnp.where(j >= 0, d, TAB_W_ROWS - 1)
    return step


def _attn_a_kernel(qa_ref, qb_ref, k_ref, v_ref, za_ref, zb_ref, r_ref, o_ref,
                   vt_ref, qt_ref, s_ref, acc_ref):
    n = pl.program_id(2)
    heads = HEADS_PER_STEP

    @pl.when(n == 0)
    def _():
        for h in range(heads):
            _value_tiles(v_ref.at[h], vt_ref.at[h])

    for slot, q_ref in enumerate((qa_ref, qb_ref)):
        for h in range(heads):
            qt_ref[slot, h] = q_ref[h]

    _PairSoftmax(_causal_step(n), heads, qt_ref, lambda h, j: k_ref[h, _tile_rows(j), :],
                 lambda h, j: vt_ref[h, j], lambda h, delta: r_ref[h, delta], s_ref, acc_ref).run(CAUSAL_STEPS)
    for slot, (z_ref, idx) in enumerate(zip((za_ref, zb_ref), _pair_tiles(n))):
        for h in range(heads):
            y = (_normalized(acc_ref[slot, h]) * z_ref[h].astype(F32)).T
            o_ref[_tile_rows(idx), h * HEAD_DIM:(h + 1) * HEAD_DIM] = y.astype(BF16)


def _resident(block_shape, index_map):
    return pl.BlockSpec(block_shape, index_map, pipeline_mode=pl.Buffered(1))


def _attn_a(k_nat, qvz_t, table):
    batch = k_nat.shape[0]
    heads = HEADS_PER_STEP
    hgroups = N_HEADS // heads

    def tile_slot(s, second):
        def index(hg, b, n):
            return (b, s * hgroups + hg, 0, N_TILES - 1 - n if second else n)
        return pl.BlockSpec((None, heads, HEAD_DIM, TILE), index)

    return pl.pallas_call(
        _attn_a_kernel,
        grid=(hgroups, batch, N_PAIRS),
        in_specs=[tile_slot(0, False), tile_slot(0, True),
                  pl.BlockSpec((None, heads, SEQ, HEAD_DIM), lambda hg, b, n: (b, hg, 0, 0)),
                  pl.BlockSpec((None, heads, HEAD_DIM, SEQ), lambda hg, b, n: (b, hgroups + hg, 0, 0)),
                  tile_slot(2, False), tile_slot(2, True),
                  _resident((heads, N_TILES, TILE, TILE), lambda hg, b, n: (hg, TAB_A // N_TILES, 0, 0))],
        out_specs=pl.BlockSpec((None, SEQ, heads * HEAD_DIM), lambda hg, b, n: (b, 0, hg)),
        out_shape=jax.ShapeDtypeStruct((batch, SEQ, ATT_WIDTH), BF16),
        scratch_shapes=[
            pltpu.VMEM((heads, N_TILES, VT_ROWS, TILE), BF16),
            pltpu.VMEM((2, heads, HEAD_DIM, TILE), BF16),
            pltpu.VMEM((heads, CAUSAL_STEPS, TILE, TILE), F32),
            pltpu.VMEM((2, heads, VT_ROWS, TILE), F32),
        ],
        compiler_params=pltpu.CompilerParams(dimension_semantics=("parallel", "parallel", "arbitrary"),
                                             vmem_limit_bytes=VMEM_LIMIT),
        name="dilated_attn",
    )(qvz_t, qvz_t, k_nat, qvz_t, qvz_t, qvz_t, table)


def _cmp_kernel(c_ref, pos_ref, w1_ref, w2_ref, o_ref):
    c = c_ref[...].astype(F32)
    x_lo = (c + pos_ref[0]).astype(BF16)
    x_hi = (c + pos_ref[1]).astype(BF16)
    a = jnp.dot(x_lo, w1_ref[0:CMP_CHUNK, :], preferred_element_type=F32)
    bm = jnp.dot(x_hi, w1_ref[CMP_CHUNK:2 * CMP_CHUNK, :], preferred_element_type=F32)
    hid = jax.nn.gelu(a + pltpu.roll(bm, N_CMP - 1, 0))
    res = jnp.dot(hid.astype(BF16), w2_ref[...], preferred_element_type=F32)
    o_ref[...] = jnp.where(pl.program_id(0) == 1, res.T, res).astype(BF16)


def _compress(chunks, pos, w1_bf, w2_bf):
    batch = chunks.shape[0]
    return pl.pallas_call(
        _cmp_kernel,
        grid=(2, batch, NSA_KV_GROUPS),
        in_specs=[
            pl.BlockSpec((None, None, N_CMP, CMP_CHUNK), lambda t, b, g: (b, t * NSA_KV_GROUPS + g, 0, 0)),
            pl.BlockSpec((None, 2, 1, CMP_CHUNK), lambda t, b, g: (t, 0, 0, 0)),
            pl.BlockSpec((None, 2 * CMP_CHUNK, CMP_HIDDEN), lambda t, b, g: (t, 0, 0)),
            pl.BlockSpec((None, CMP_HIDDEN, HEAD_DIM), lambda t, b, g: (t, 0, 0)),
        ],
        out_specs=pl.BlockSpec((None, None, None, N_CMP, HEAD_DIM), lambda t, b, g: (t, b, g, 0, 0)),
        out_shape=jax.ShapeDtypeStruct((2, batch, NSA_KV_GROUPS, N_CMP, HEAD_DIM), BF16),
        compiler_params=pltpu.CompilerParams(dimension_semantics=("arbitrary", "arbitrary", "arbitrary"),
                                             vmem_limit_bytes=VMEM_LIMIT),
        name="compress_kv",
    )(chunks, pos, w1_bf, w2_bf)


def _nsa_kernel(qa_ref, qb_ref, zca_ref, zcb_ref, zsa_ref, zsb_ref, zwa_ref, zwb_ref, gla_ref, glb_ref,
                kc_ref, vct_ref, ks_ref, vs_ref, kw_ref, vw_ref, rs_ref, rw_ref, rca_ref, rcb_ref,
                ovt_ref, onehot_ref, o_ref,
                kaug_ref, vst_ref, vwt_ref, qa_sc, qw_sc, gates_ref, ycmp_ref, s_ref, accs_ref, accw_ref):
    n = pl.program_id(2)
    heads = NSA_HEADS_PER_GROUP
    tiles = _pair_tiles(n)

    @pl.when(n == 0)
    def _():
        kaug_ref[:, 0:HEAD_DIM] = ks_ref[...]
        kaug_ref[:, HEAD_DIM:2 * HEAD_DIM] = onehot_ref[...]
        _value_tiles(vs_ref, vst_ref)
        _value_tiles(vw_ref, vwt_ref)

    kc = kc_ref[...]
    vct = vct_ref[...]
    for slot, (q_ref, rc_ref, zc_ref, gl_ref) in enumerate(
            ((qa_ref, rca_ref, zca_ref, gla_ref), (qb_ref, rcb_ref, zcb_ref, glb_ref))):
        psum = jnp.zeros((N_CMP, TILE), F32)
        gates_ref[slot] = jax.nn.sigmoid(gl_ref[...].T)
        for h in range(heads):
            qt = q_ref[h]
            qw_sc[slot, h] = qt
            qa_sc[slot, h, 0:HEAD_DIM, :] = qt
            bias = rc_ref[h]
            s = jnp.dot(kc, qt, preferred_element_type=F32) + bias
            m = jnp.max(s, axis=0, keepdims=True)
            p = jnp.where(bias > 0.5 * NEG, jnp.exp2(s - m), 0.0)
            p = p / jnp.maximum(jnp.sum(p, axis=0, keepdims=True), 1e-30)
            psum = psum + p
            o_cmp = jnp.dot(vct, p.astype(BF16), preferred_element_type=F32)
            ycmp_ref[slot, h] = gates_ref[slot, h:h + 1, :] * o_cmp * zc_ref[h].astype(F32)

        imp = jnp.dot(ovt_ref[...], psum, precision=lax.Precision.HIGHEST, preferred_element_type=F32)
        blk = lax.broadcasted_iota(jnp.int32, (N_SLC, TILE), 0)
        tq = tiles[slot] * TILE + lax.broadcasted_iota(jnp.int32, (N_SLC, TILE), 1)
        cur = lax.shift_right_logical(tq, int(math.log2(SLC_BLOCK)))
        forced = (blk == 0) | (blk == cur) | (blk == cur - 1)
        score = jnp.where(forced, FORCE_SCORE, jnp.where(blk > cur, -FORCE_SCORE, imp))
        group = 8
        rows = [score[r:r + group, :] for r in range(0, N_SLC, group)]
        row_idx = lax.broadcasted_iota(jnp.int32, (group, TILE), 0)
        cnts = [jnp.zeros((group, TILE), jnp.int32) for _ in rows]
        for jj in range(N_SLC):
            sj = score[jj:jj + 1, :]
            for r, sr in enumerate(rows):
                lo = r * group
                if lo > jj:
                    beats = jnp.where(sj >= sr, 1, 0)
                elif lo + group - 1 <= jj:
                    beats = jnp.where(sj > sr, 1, 0)
                else:
                    beats = jnp.where(row_idx + lo > jj, jnp.where(sj >= sr, 1, 0), jnp.where(sj > sr, 1, 0))
                cnts[r] = cnts[r] + beats
        cnt = jnp.concatenate(cnts, axis=0)
        mask = jnp.where(cnt < SLC_TOP_N, 0.0, NEG)
        mask = jnp.concatenate([mask, jnp.zeros((HEAD_DIM - N_SLC, TILE), F32)], axis=0).astype(BF16)
        for h in range(heads):
            qa_sc[slot, h, HEAD_DIM:2 * HEAD_DIM, :] = mask

    _PairSoftmax(_causal_step(n), heads, qa_sc, lambda h, j: kaug_ref[_tile_rows(j), :],
                 lambda h, j: vst_ref[j], lambda h, delta: rs_ref[h, delta], s_ref, accs_ref).run(CAUSAL_STEPS)
    _PairSoftmax(_window_step(n), heads, qw_sc, lambda h, j: kw_ref[_tile_rows(j), :],
                 lambda h, j: vwt_ref[j], lambda h, b: rw_ref[h, b], s_ref, accw_ref).run_unrolled(2 * WIN_TILES)

    for slot, (zs_ref, zw_ref) in enumerate(((zsa_ref, zwa_ref), (zsb_ref, zwb_ref))):
        for h in range(heads):
            y = (ycmp_ref[slot, h]
                 + gates_ref[slot, heads + h:heads + h + 1, :] * _normalized(accs_ref[slot, h])
                 * zs_ref[h].astype(F32)
                 + gates_ref[slot, 2 * heads + h:2 * heads + h + 1, :] * _normalized(accw_ref[slot, h])
                 * zw_ref[h].astype(F32))
            o_ref[_tile_rows(tiles[slot]), h * HEAD_DIM:(h + 1) * HEAD_DIM] = y.T.astype(BF16)


def _nsa_attn(qz_t, gate_logits, kv_nat, kv_t, cmp_kv, table, table_c):
    batch = qz_t.shape[0]
    heads = NSA_HEADS_PER_GROUP
    groups = NSA_KV_GROUPS
    _, _, _, _, ov_t, onehot = _static_maps()

    def tile_of(n, second):
        return N_TILES - 1 - n if second else n

    def qz_specs(slot_group):
        return [pl.BlockSpec((None, heads, HEAD_DIM, TILE),
                             lambda g, b, n, second=second: (b, slot_group * groups + g, 0, tile_of(n, second)))
                for second in (False, True)]

    def key_spec(branch):
        return pl.BlockSpec((None, None, SEQ, HEAD_DIM), lambda g, b, n: (b, branch * groups + g, 0, 0))

    def value_spec(branch):
        return pl.BlockSpec((None, None, HEAD_DIM, SEQ), lambda g, b, n: (b, branch * groups + g, 0, 0))

    def cmp_spec(t):
        return pl.BlockSpec((None, None, None, N_CMP, HEAD_DIM), lambda g, b, n: (t, b, g, 0, 0))

    gl_specs = [pl.BlockSpec((None, TILE, HEAD_DIM), lambda g, b, n, second=second: (b, tile_of(n, second), g))
                for second in (False, True)]
    rc_specs = [pl.BlockSpec((heads, None, N_CMP, TILE), lambda g, b, n, second=second: (g, tile_of(n, second), 0, 0))
                for second in (False, True)]

    return pl.pallas_call(
        _nsa_kernel,
        grid=(groups, batch, N_PAIRS),
        in_specs=[
            *qz_specs(0), *qz_specs(1), *qz_specs(2), *qz_specs(3), *gl_specs,
            cmp_spec(0), cmp_spec(1), key_spec(0), value_spec(0), key_spec(1), value_spec(1),
            _resident((heads, N_TILES, TILE, TILE), lambda g, b, n: (g, TAB_S // N_TILES, 0, 0)),
            _resident((heads, TAB_W_ROWS, TILE, TILE), lambda g, b, n: (g, TAB_W // TAB_W_ROWS, 0, 0)),
            *rc_specs,
            _resident((N_SLC, N_CMP), lambda g, b, n: (0, 0)),
            _resident((SEQ, HEAD_DIM), lambda g, b, n: (0, 0)),
        ],
        out_specs=pl.BlockSpec((None, SEQ, heads * HEAD_DIM), lambda g, b, n: (b, 0, g)),
        out_shape=jax.ShapeDtypeStruct((batch, SEQ, ATT_WIDTH), BF16),
        scratch_shapes=[
            pltpu.VMEM((SEQ, 2 * HEAD_DIM), BF16),
            pltpu.VMEM((N_TILES, VT_ROWS, TILE), BF16),
            pltpu.VMEM((N_TILES, VT_ROWS, TILE), BF16),
            pltpu.VMEM((2, heads, 2 * HEAD_DIM, TILE), BF16),
            pltpu.VMEM((2, heads, HEAD_DIM, TILE), BF16),
            pltpu.VMEM((2, HEAD_DIM, TILE), F32),
            pltpu.VMEM((2, heads, HEAD_DIM, TILE), F32),
            pltpu.VMEM((heads, CAUSAL_STEPS, TILE, TILE), F32),
            pltpu.VMEM((2, heads, VT_ROWS, TILE), F32),
            pltpu.VMEM((2, heads, VT_ROWS, TILE), F32),
        ],
        compiler_params=pltpu.CompilerParams(dimension_semantics=("parallel", "parallel", "arbitrary"),
                                             vmem_limit_bytes=VMEM_LIMIT),
        name="nsa_attn",
    )(qz_t, qz_t, qz_t, qz_t, qz_t, qz_t, qz_t, qz_t, gate_logits, gate_logits, cmp_kv, cmp_kv,
      kv_nat, kv_t, kv_nat, kv_t, table, table, table_c, table_c,
      jnp.asarray(ov_t), jnp.asarray(onehot, dtype=BF16))


def kernel(x, norm_pre, norm_post, rel_table, w_in_a, w_out_a, kv_norm, w_kv, cmp_pos_k, cmp_pos_v,
           cmp_w1_k, cmp_w2_k, cmp_w1_v, cmp_w2_v, w_in_b, w_out_b):
    batch = x.shape[0]
    m = batch * SEQ
    heads = NSA_HEADS_PER_GROUP
    groups = NSA_KV_GROUPS
    x2d = x.reshape(m, D_MODEL)
    table, table_c = _bias_tables(rel_table)

    tile_slots = 8
    tiles_per_part = ATT_WIDTH // (tile_slots * HEAD_DIM)

    kinds_a = [(kind,) * tile_slots for kind in ("tq", "n", "t", "ts") for _ in range(tiles_per_part)]
    k_nat, qvz_t = _norm_matmul(x2d, norm_pre[0], w_in_a[0], tile_kinds=kinds_a)
    att_a = _attn_a(k_nat, qvz_t, table)
    h1 = _out_proj(att_a.reshape(m, ATT_WIDTH), w_out_a[0], x2d, norm_post[0])

    kinds_kv = [("c",) * tile_slots] + [("n",) * groups + ("t",) * groups] * 2
    kv_nat, kv_t, cmp_chunks = _norm_matmul(h1, kv_norm, w_kv, tile_kinds=kinds_kv)
    pos = jnp.stack([cmp_pos_k, cmp_pos_v]).reshape(2, 2, 1, CMP_CHUNK)
    cmp_kv = _compress(cmp_chunks, pos, jnp.stack([cmp_w1_k, cmp_w1_v]).astype(BF16),
                       jnp.stack([cmp_w2_k, cmp_w2_v]).astype(BF16))

    w_bt = w_in_b[0].T
    n_main = (1 + NSA_BRANCHES) * ATT_WIDTH
    wg = w_bt[n_main:].reshape(NSA_BRANCHES, groups, heads, D_MODEL).transpose(1, 0, 2, 3)
    wg = jnp.pad(wg.reshape(groups, NSA_BRANCHES * heads, D_MODEL),
                 ((0, 0), (0, HEAD_DIM - NSA_BRANCHES * heads), (0, 0))).reshape(groups * HEAD_DIM, D_MODEL)
    kinds_b = [(kind,) * tile_slots for kind in ("tq",) + ("ts",) * NSA_BRANCHES for _ in range(tiles_per_part)]
    qz_t, gate_logits = _norm_matmul(h1, norm_pre[1], w_bt, wg.T.astype(BF16), tile_kinds=kinds_b,
                                     w_transposed=True)
    att_b = _nsa_attn(qz_t, gate_logits.reshape(batch, SEQ, groups * HEAD_DIM), kv_nat, kv_t, cmp_kv,
                      table, table_c)
    out = _out_proj(att_b.reshape(m, ATT_WIDTH), w_out_b[0], h1, norm_post[1])
    return out.reshape(batch, SEQ, D_MODEL)
```

```python
import functools
import math

import numpy as np
import jax
import jax.numpy as jnp
from jax import lax
from jax.experimental import pallas as pl
from jax.experimental.pallas import tpu as pltpu

F32 = jnp.float32
BF16 = jnp.bfloat16

D_MODEL = 2048
SEQ = 2048
N_HEADS = 16
HEAD_DIM = 128
ATT_WIDTH = N_HEADS * HEAD_DIM
ATT_SCALE = HEAD_DIM ** -0.5
DIL_PATTERNS = ((128, 1), (512, 4), (2048, 16))
REL_BUCKETS = 32
REL_MAX_EXACT = 16
REL_MAX_DISTANCE = 2048
NSA_KV_GROUPS = 4
NSA_HEADS_PER_GROUP = 4
NSA_BRANCHES = 3
CMP_BLOCK = 32
CMP_STRIDE = 16
CMP_HIDDEN = 256
SLC_BLOCK = 64
SLC_TOP_N = 16
WIN_SIZE = 512
RMS_EPS = 1e-6
NEG = -1e30
FORCE_SCORE = 1e9
LOG2E = math.log2(math.e)

TILE = 256
N_TILES = SEQ // TILE
N_PAIRS = N_TILES // 2
CAUSAL_STEPS = N_TILES + 1
STEP_UNROLL = 3
N_CMP = SEQ // CMP_STRIDE
CMP_CHUNK = CMP_STRIDE * HEAD_DIM
N_SLC = SEQ // SLC_BLOCK
WIN_TILES = -(-(WIN_SIZE - 1) // TILE) + 1
HEADS_PER_STEP = 4
VT_ROWS = HEAD_DIM + 16
MXU_COLS = 512

TAB_A, TAB_S, TAB_W = 0, N_TILES, 2 * N_TILES
TAB_W_ROWS = 4
TAB_ROWS = 2 * N_TILES + TAB_W_ROWS
GATHER_MIN_BUCKETS = 6
VMEM_LIMIT = 52 * 1024 * 1024


def _np_bucket(dist):
    n = np.maximum(dist, 0)
    nf = np.maximum(n, 1).astype(np.float32)
    log_b = REL_MAX_EXACT + (
        np.log(nf / np.float32(REL_MAX_EXACT)) / np.float32(math.log(REL_MAX_DISTANCE / REL_MAX_EXACT))
        * np.float32(REL_BUCKETS - REL_MAX_EXACT)).astype(np.int32)
    return np.where(n < REL_MAX_EXACT, n, np.minimum(log_b, REL_BUCKETS - 1)).astype(np.int32)


@functools.lru_cache(maxsize=None)
def _static_maps():
    ki = np.arange(TILE)[:, None]
    qi = np.arange(TILE)[None, :]
    dist = TILE * np.arange(N_TILES)[:, None, None] + qi[None] - ki[None]
    bk_t = _np_bucket(dist)
    mult = np.zeros(dist.shape, np.int64)
    for window, dil in DIL_PATTERNS:
        mult += ((dist % dil == 0) & (dist <= window)).astype(np.int64)
    base_a = np.where((dist >= 0) & (mult > 0), np.log2(np.maximum(mult, 1)), NEG)
    base_s = np.where(dist >= 0, 0.0, NEG)
    base_w = np.full((TAB_W_ROWS, TILE, TILE), NEG)
    base_w[:WIN_TILES] = np.where((dist >= 0) & (dist < WIN_SIZE), 0.0, NEG)[:WIN_TILES]
    base_t = np.concatenate([base_a, base_s, base_w]).astype(np.float32)
    ci = np.arange(N_CMP)[:, None]
    t = TILE * np.arange(N_TILES)[:, None, None] + qi[None]
    dist_c = t - (CMP_STRIDE * ci[None] + CMP_BLOCK - 1)
    bk_c = _np_bucket(dist_c)
    base_c = np.where((dist_c >= 0) & (ci[None] < N_CMP - 1), 0.0, NEG).astype(np.float32)
    cs = np.arange(N_CMP)[None, :] * CMP_STRIDE
    sj = np.arange(N_SLC)[:, None] * SLC_BLOCK
    ov_t = ((cs < sj + SLC_BLOCK) & (cs + CMP_BLOCK > sj) & (np.arange(N_CMP)[None, :] < N_CMP - 1))
    onehot = np.zeros((SEQ, HEAD_DIM), np.float32)
    onehot[np.arange(SEQ), np.arange(SEQ) // SLC_BLOCK] = 1.0
    return bk_t.astype(np.int32), base_t, bk_c.astype(np.int32), base_c, ov_t.astype(np.float32), onehot


def _bias_kernel(tab_ref, tabt_ref, bkt_ref, baset_ref, bkc_ref, basec_ref, out_ref, outc_ref, *,
                 present_t, present_c):
    h = pl.program_id(0)
    tv = [tab_ref[b, h] * LOG2E for b in range(REL_BUCKETS)]
    lanes = 128
    row = jnp.broadcast_to(tabt_ref[pl.ds(h, 1), :] * LOG2E, (TILE, lanes))

    def lookup(bk, present):
        if len(present) > GATHER_MIN_BUCKETS:
            parts = [jnp.take_along_axis(row[:bk.shape[0]], bk[:, c:c + lanes], axis=1)
                     for c in range(0, bk.shape[1], lanes)]
            return jnp.concatenate(parts, axis=1)
        val = jnp.full(bk.shape, tv[present[0]], F32)
        for b in present[1:]:
            val = jnp.where(bk == b, tv[b], val)
        return val

    for d in range(N_TILES):
        g = lookup(bkt_ref[d], present_t[d])
        out_ref[TAB_A + d] = g + baset_ref[TAB_A + d]
        out_ref[TAB_S + d] = g + baset_ref[TAB_S + d]
        if d < TAB_W_ROWS:
            out_ref[TAB_W + d] = g + baset_ref[TAB_W + d]
        outc_ref[d] = lookup(bkc_ref[d], present_c[d]) + basec_ref[d]


def _bias_tables(rel_table):
    bk_t, base_t, bk_c, base_c, _, _ = _static_maps()
    present_t = tuple(tuple(int(b) for b in np.unique(bk_t[d])) for d in range(N_TILES))
    present_c = tuple(tuple(int(b) for b in np.unique(bk_c[d])) for d in range(N_TILES))
    return pl.pallas_call(
        functools.partial(_bias_kernel, present_t=present_t, present_c=present_c),
        grid=(N_HEADS,),
        in_specs=[
            pl.BlockSpec(memory_space=pltpu.SMEM),
            pl.BlockSpec((N_HEADS, 128), lambda h: (0, 0)),
            pl.BlockSpec((N_TILES, TILE, TILE), lambda h: (0, 0, 0)),
            pl.BlockSpec((TAB_ROWS, TILE, TILE), lambda h: (0, 0, 0)),
            pl.BlockSpec((N_TILES, N_CMP, TILE), lambda h: (0, 0, 0)),
            pl.BlockSpec((N_TILES, N_CMP, TILE), lambda h: (0, 0, 0)),
        ],
        out_specs=[
            pl.BlockSpec((None, TAB_ROWS, TILE, TILE), lambda h: (h, 0, 0, 0)),
            pl.BlockSpec((None, N_TILES, N_CMP, TILE), lambda h: (h, 0, 0, 0)),
        ],
        out_shape=[
            jax.ShapeDtypeStruct((N_HEADS, TAB_ROWS, TILE, TILE), F32),
            jax.ShapeDtypeStruct((N_HEADS, N_TILES, N_CMP, TILE), F32),
        ],
        compiler_params=pltpu.CompilerParams(dimension_semantics=("arbitrary",),
                                             vmem_limit_bytes=VMEM_LIMIT),
        name="bias_tables",
    )(rel_table, jnp.pad(rel_table.T, ((0, 0), (0, 128 - REL_BUCKETS))),
      jnp.asarray(bk_t), jnp.asarray(base_t), jnp.asarray(bk_c), jnp.asarray(base_c))


def _silu(z):
    half = 0.5 * z
    return half + half * jnp.tanh(half)


TRANSPOSED_KINDS = ("t", "tq", "ts")


def _nm_kernel(*refs, tile_kinds, gate, w_transposed):
    refs = list(refs)
    x_ref, g_ref, w_ref = refs[:3]
    del refs[:3]
    wg_ref = refs.pop(0) if gate else None
    kinds_used = {k for kinds in tile_kinds for k in kinds}
    on_ref = refs.pop(0) if "n" in kinds_used else None
    ot_ref = refs.pop(0) if kinds_used & set(TRANSPOSED_KINDS) else None
    og_ref = refs.pop(0) if gate else None
    oc_ref = refs.pop(0) if "c" in kinds_used else None
    xn_ref = refs.pop(0)
    res_ref = refs.pop(0) if "c" in kinds_used else None
    j = pl.program_id(1)

    @pl.when(j == 0)
    def _():
        x = x_ref[...]
        ms = jnp.mean(x * x, axis=-1, keepdims=True)
        xn_ref[...] = (x * lax.rsqrt(ms + RMS_EPS) * g_ref[...]).astype(BF16)
        if gate:
            og_ref[...] = jnp.dot(xn_ref[...], wg_ref[...], preferred_element_type=F32)

    def product(kinds):
        pos = {"n": 0, "t": 0}
        per_group = MXU_COLS // HEAD_DIM
        for cc in range(len(kinds) // per_group):
            if w_transposed:
                w_cols = w_ref[cc * MXU_COLS:(cc + 1) * MXU_COLS, :].astype(BF16)
                res = lax.dot_general(xn_ref[...], w_cols, (((1,), (1,)), ((), ())), preferred_element_type=F32)
            else:
                w_cols = w_ref[:, cc * MXU_COLS:(cc + 1) * MXU_COLS].astype(BF16)
                res = jnp.dot(xn_ref[...], w_cols, preferred_element_type=F32)
            for u in range(per_group):
                c = per_group * cc + u
                piece = res[:, u * HEAD_DIM:(u + 1) * HEAD_DIM]
                kind = kinds[c]
                if kind == "n":
                    on_ref[pos["n"]] = piece.astype(BF16)
                    pos["n"] += 1
                elif kind == "c":
                    res_ref[c] = piece
                else:
                    if kind == "tq":
                        piece = piece * (ATT_SCALE * LOG2E)
                    elif kind == "ts":
                        piece = _silu(piece)
                    ot_ref[pos["t"]] = piece.T.astype(BF16)
                    pos["t"] += 1
        if "c" in kinds:
            rows = res_ref.shape[1] // CMP_STRIDE
            for c in range(len(kinds)):
                for i in range(CMP_STRIDE):
                    oc_ref[c, :, i * HEAD_DIM:(i + 1) * HEAD_DIM] = res_ref[
                        c, pl.ds(i, rows, stride=CMP_STRIDE), :].astype(BF16)

    for kinds in sorted(set(tile_kinds)):
        tiles = [t for t, k in enumerate(tile_kinds) if k == kinds]
        cond = functools.reduce(lambda a, b: a | b, [j == t for t in tiles])
        pl.when(cond)(functools.partial(product, kinds))


def _norm_matmul(x2d, gain, w, wg_bf=None, *, tile_kinds, w_transposed=False, tm=1024, tn=1024):
    m, d = x2d.shape
    slots = tn // HEAD_DIM
    batch = m // SEQ
    spb = SEQ // tm
    gate = wg_bf is not None
    assert all(len(k) == slots for k in tile_kinds)

    def family(match):
        counts = [sum(1 for k in kinds if match(k)) for kinds in tile_kinds]
        tiles = [t for t, c in enumerate(counts) if c]
        per_tile = counts[tiles[0]] if tiles else 0
        assert all(counts[t] == per_tile for t in tiles)

        def block(j):
            return jnp.minimum(sum(jnp.where(j > t, 1, 0) for t in tiles), len(tiles) - 1)
        return tiles, per_tile, block

    n_tiles, n_per, n_block = family(lambda k: k == "n")
    t_tiles, t_per, t_block = family(lambda k: k in TRANSPOSED_KINDS)
    c_tiles, c_per, _ = family(lambda k: k == "c")
    assert c_tiles in ([], [0]) and c_per in (0, slots)

    in_specs = [
        pl.BlockSpec((tm, d), lambda i, j: (i, 0)),
        pl.BlockSpec((1, d), lambda i, j: (0, 0)),
        pl.BlockSpec((tn, d), lambda i, j: (j, 0)) if w_transposed else pl.BlockSpec((d, tn), lambda i, j: (0, j)),
    ]
    args = [x2d, gain.reshape(1, d), w]
    out_shape, out_specs = [], []
    scratch = [pltpu.VMEM((tm, d), BF16)]
    if gate:
        in_specs.append(pl.BlockSpec((d, wg_bf.shape[1]), lambda i, j: (0, 0)))
        args.append(wg_bf)
    if n_tiles:
        out_shape.append(jax.ShapeDtypeStruct((batch, n_per * len(n_tiles), SEQ, HEAD_DIM), BF16))
        out_specs.append(pl.BlockSpec((None, n_per, tm, HEAD_DIM),
                                      lambda i, j: (i // spb, n_block(j), i % spb, 0)))
    if t_tiles:
        out_shape.append(jax.ShapeDtypeStruct((batch, t_per * len(t_tiles), HEAD_DIM, SEQ), BF16))
        out_specs.append(pl.BlockSpec((None, t_per, HEAD_DIM, tm),
                                      lambda i, j: (i // spb, t_block(j), 0, i % spb)))
    if gate:
        out_shape.append(jax.ShapeDtypeStruct((m, wg_bf.shape[1]), F32))
        out_specs.append(pl.BlockSpec((tm, wg_bf.shape[1]), lambda i, j: (i, 0)))
    if c_tiles:
        out_shape.append(jax.ShapeDtypeStruct((batch, slots, N_CMP, CMP_CHUNK), BF16))
        out_specs.append(pl.BlockSpec((None, slots, tm // CMP_STRIDE, CMP_CHUNK),
                                      lambda i, j: (i // spb, 0, i % spb, 0)))
        scratch.append(pltpu.VMEM((slots, tm, HEAD_DIM), F32))
    return pl.pallas_call(
        functools.partial(_nm_kernel, tile_kinds=tuple(tuple(k) for k in tile_kinds), gate=gate,
                          w_transposed=w_transposed),
        grid=(m // tm, len(tile_kinds)),
        in_specs=in_specs,
        out_specs=out_specs,
        out_shape=out_shape,
        scratch_shapes=scratch,
        compiler_params=pltpu.CompilerParams(dimension_semantics=("parallel", "arbitrary"),
                                             vmem_limit_bytes=VMEM_LIMIT),
        name="norm_matmul",
    )(*args)


def _op_kernel(a_ref, w_ref, r_ref, g_ref, o_ref, wbf_ref, *, parts):
    @pl.when(pl.program_id(0) == 0)
    def _():
        step = 256
        for r in range(w_ref.shape[0] // step):
            wbf_ref[r * step:(r + 1) * step, :] = w_ref[r * step:(r + 1) * step, :].astype(BF16)

    rows = a_ref.shape[0] // parts
    for r in range(parts):
        sl = slice(r * rows, (r + 1) * rows)
        y = jnp.dot(a_ref[sl, :], wbf_ref[...], preferred_element_type=F32)
        ms = jnp.mean(y * y, axis=-1, keepdims=True)
        o_ref[sl, :] = r_ref[sl, :] + y * lax.rsqrt(ms + RMS_EPS) * g_ref[...]


def _out_proj(a2d, w, resid2d, gain, *, tm=512, parts=2):
    m, k = a2d.shape
    n = w.shape[1]
    return pl.pallas_call(
        functools.partial(_op_kernel, parts=parts),
        grid=(m // tm,),
        in_specs=[
            pl.BlockSpec((tm, k), lambda i: (i, 0)),
            pl.BlockSpec((k, n), lambda i: (0, 0), pipeline_mode=pl.Buffered(1)),
            pl.BlockSpec((tm, n), lambda i: (i, 0)),
            pl.BlockSpec((1, n), lambda i: (0, 0)),
        ],
        out_specs=pl.BlockSpec((tm, n), lambda i: (i, 0)),
        out_shape=jax.ShapeDtypeStruct((m, n), F32),
        scratch_shapes=[pltpu.VMEM((k, n), BF16)],
        compiler_params=pltpu.CompilerParams(dimension_semantics=("arbitrary",),
                                             vmem_limit_bytes=VMEM_LIMIT),
        name="out_proj",
    )(a2d, w, resid2d, gain.reshape(1, n))


def _value_tiles(src_ref, dst_ref):
    for c in range(N_TILES):
        dst_ref[c, 0:HEAD_DIM, :] = src_ref[:, c * TILE:(c + 1) * TILE]
        dst_ref[c, HEAD_DIM:VT_ROWS, :] = jnp.ones((VT_ROWS - HEAD_DIM, TILE), BF16)


def _pair_tiles(n):
    return (n, N_TILES - 1 - n)


def _tile_rows(idx):
    return pl.ds(pl.multiple_of(idx * TILE, TILE), TILE)


def _normalized(acc):
    return acc[0:HEAD_DIM, :] / acc[HEAD_DIM:HEAD_DIM + 1, :]


class _PairSoftmax:
    def __init__(self, step_fn, heads, q_ref, k_fn, vt_fn, bias_fn, s_ref, acc_ref):
        self.step_fn, self.heads, self.q_ref, self.k_fn, self.vt_fn = step_fn, heads, q_ref, k_fn, vt_fn
        self.bias_fn, self.s_ref, self.acc_ref = bias_fn, s_ref, acc_ref

    def initial_max(self):
        neg = (jnp.full((1, TILE), NEG, F32),) * self.heads
        return neg, neg

    def scores(self, t, ms):
        slot, j, b = self.step_fn(t)
        first = slot == 0
        m0, m1 = list(ms[0]), list(ms[1])
        for h in range(self.heads):
            s = (jnp.dot(self.k_fn(h, j), self.q_ref[slot, h], preferred_element_type=F32)
                 + self.bias_fn(h, b))
            self.s_ref[h, t] = s
            cm = jnp.max(s, axis=0, keepdims=True)
            m0[h] = jnp.where(first, jnp.maximum(m0[h], cm), m0[h])
            m1[h] = jnp.where(first, m1[h], jnp.maximum(m1[h], cm))
        return tuple(m0), tuple(m1)

    def clear(self):
        for slot in range(2):
            for h in range(self.heads):
                self.acc_ref[slot, h] = jnp.zeros((VT_ROWS, TILE), F32)

    def weighted(self, t, ms):
        slot, j, _ = self.step_fn(t)
        first = slot == 0
        for h in range(self.heads):
            p = jnp.exp2((self.s_ref[h, t] - jnp.where(first, ms[0][h], ms[1][h])).astype(BF16))
            self.acc_ref[slot, h] += jnp.dot(self.vt_fn(h, j), p, preferred_element_type=F32)

    def run(self, steps):
        ms = lax.fori_loop(0, steps, self.scores, self.initial_max(), unroll=True)
        self.clear()

        def body(t, carry):
            self.weighted(t, ms)
            return carry

        lax.fori_loop(0, steps, body, 0, unroll=STEP_UNROLL)

    def run_unrolled(self, steps):
        ms = self.initial_max()
        for t in range(steps):
            ms = self.scores(t, ms)
        self.clear()
        for t in range(steps):
            self.weighted(t, ms)


def _causal_step(n):
    def step(t):
        first = t <= n
        return (jnp.where(first, 0, 1), jnp.where(first, t, t - n - 1), jnp.where(first, n - t, N_TILES - t))
    return step


def _window_step(n):
    def step(t):
        if isinstance(t, int):
            slot, d = divmod(t, WIN_TILES)
            idx = _pair_tiles(n)[slot]
        else:
            slot = jnp.where(t >= WIN_TILES, 1, 0)
            d = t - WIN_TILES * slot
            idx = jnp.where(slot == 0, n, N_TILES - 1 - n)
        j = idx - d
        return slot, jnp.maximum(j, 0), jnp.where(j >= 0, d, TAB_W_ROWS - 1)
    return step


def _attn_a_kernel(qa_ref, qb_ref, k_ref, v_ref, za_ref, zb_ref, r_ref, o_ref,
                   vt_ref, qt_ref, s_ref, acc_ref):
    n = pl.program_id(2)
    heads = HEADS_PER_STEP

    @pl.when(n == 0)
    def _():
        for h in range(heads):
            _value_tiles(v_ref.at[h], vt_ref.at[h])

    for slot, q_ref in enumerate((qa_ref, qb_ref)):
        for h in range(heads):
            qt_ref[slot, h] = q_ref[h]

    _PairSoftmax(_causal_step(n), heads, qt_ref, lambda h, j: k_ref[h, _tile_rows(j), :],
                 lambda h, j: vt_ref[h, j], lambda h, delta: r_ref[h, delta], s_ref, acc_ref).run(CAUSAL_STEPS)
    for slot, (z_ref, idx) in enumerate(zip((za_ref, zb_ref), _pair_tiles(n))):
        for h in range(heads):
            y = (_normalized(acc_ref[slot, h]) * z_ref[h].astype(F32)).T
            o_ref[_tile_rows(idx), h * HEAD_DIM:(h + 1) * HEAD_DIM] = y.astype(BF16)


def _resident(block_shape, index_map):
    return pl.BlockSpec(block_shape, index_map, pipeline_mode=pl.Buffered(1))


def _attn_a(k_nat, qvz_t, table):
    batch = k_nat.shape[0]
    heads = HEADS_PER_STEP
    hgroups = N_HEADS // heads

    def tile_slot(s, second):
        def index(hg, b, n):
            return (b, s * hgroups + hg, 0, N_TILES - 1 - n if second else n)
        return pl.BlockSpec((None, heads, HEAD_DIM, TILE), index)

    return pl.pallas_call(
        _attn_a_kernel,
        grid=(hgroups, batch, N_PAIRS),
        in_specs=[tile_slot(0, False), tile_slot(0, True),
                  pl.BlockSpec((None, heads, SEQ, HEAD_DIM), lambda hg, b, n: (b, hg, 0, 0)),
                  pl.BlockSpec((None, heads, HEAD_DIM, SEQ), lambda hg, b, n: (b, hgroups + hg, 0, 0)),
                  tile_slot(2, False), tile_slot(2, True),
                  pl.BlockSpec((heads, N_TILES, TILE, TILE), lambda hg, b, n: (hg, TAB_A // N_TILES, 0, 0))],
        out_specs=pl.BlockSpec((None, SEQ, heads * HEAD_DIM), lambda hg, b, n: (b, 0, hg)),
        out_shape=jax.ShapeDtypeStruct((batch, SEQ, ATT_WIDTH), BF16),
        scratch_shapes=[
            pltpu.VMEM((heads, N_TILES, VT_ROWS, TILE), BF16),
            pltpu.VMEM((2, heads, HEAD_DIM, TILE), BF16),
            pltpu.VMEM((heads, CAUSAL_STEPS, TILE, TILE), F32),
            pltpu.VMEM((2, heads, VT_ROWS, TILE), F32),
        ],
        compiler_params=pltpu.CompilerParams(dimension_semantics=("parallel", "parallel", "arbitrary"),
                                             vmem_limit_bytes=VMEM_LIMIT),
        name="dilated_attn",
    )(qvz_t, qvz_t, k_nat, qvz_t, qvz_t, qvz_t, table)


def _cmp_kernel(c_ref, pos_ref, w1_ref, w2_ref, o_ref):
    c = c_ref[...].astype(F32)
    x_lo = (c + pos_ref[0]).astype(BF16)
    x_hi = (c + pos_ref[1]).astype(BF16)
    a = jnp.dot(x_lo, w1_ref[0:CMP_CHUNK, :], preferred_element_type=F32)
    bm = jnp.dot(x_hi, w1_ref[CMP_CHUNK:2 * CMP_CHUNK, :], preferred_element_type=F32)
    hid = jax.nn.gelu(a + pltpu.roll(bm, N_CMP - 1, 0))
    res = jnp.dot(hid.astype(BF16), w2_ref[...], preferred_element_type=F32)
    o_ref[...] = jnp.where(pl.program_id(0) == 1, res.T, res).astype(BF16)


def _compress(chunks, pos, w1_bf, w2_bf):
    batch = chunks.shape[0]
    return pl.pallas_call(
        _cmp_kernel,
        grid=(2, batch, NSA_KV_GROUPS),
        in_specs=[
            pl.BlockSpec((None, None, N_CMP, CMP_CHUNK), lambda t, b, g: (b, t * NSA_KV_GROUPS + g, 0, 0)),
            pl.BlockSpec((None, 2, 1, CMP_CHUNK), lambda t, b, g: (t, 0, 0, 0)),
            pl.BlockSpec((None, 2 * CMP_CHUNK, CMP_HIDDEN), lambda t, b, g: (t, 0, 0)),
            pl.BlockSpec((None, CMP_HIDDEN, HEAD_DIM), lambda t, b, g: (t, 0, 0)),
        ],
        out_specs=pl.BlockSpec((None, None, None, N_CMP, HEAD_DIM), lambda t, b, g: (t, b, g, 0, 0)),
        out_shape=jax.ShapeDtypeStruct((2, batch, NSA_KV_GROUPS, N_CMP, HEAD_DIM), BF16),
        compiler_params=pltpu.CompilerParams(dimension_semantics=("arbitrary", "arbitrary", "arbitrary"),
                                             vmem_limit_bytes=VMEM_LIMIT),
        name="compress_kv",
    )(chunks, pos, w1_bf, w2_bf)


def _nsa_kernel(qa_ref, qb_ref, zca_ref, zcb_ref, zsa_ref, zsb_ref, zwa_ref, zwb_ref, gla_ref, glb_ref,
                kc_ref, vct_ref, ks_ref, vs_ref, kw_ref, vw_ref, rs_ref, rw_ref, rca_ref, rcb_ref,
                ovt_ref, onehot_ref, o_ref,
                kaug_ref, vst_ref, vwt_ref, qa_sc, qw_sc, gates_ref, ycmp_ref, s_ref, accs_ref, accw_ref):
    n = pl.program_id(2)
    heads = NSA_HEADS_PER_GROUP
    tiles = _pair_tiles(n)

    @pl.when(n == 0)
    def _():
        kaug_ref[:, 0:HEAD_DIM] = ks_ref[...]
        kaug_ref[:, HEAD_DIM:2 * HEAD_DIM] = onehot_ref[...]
        _value_tiles(vs_ref, vst_ref)
        _value_tiles(vw_ref, vwt_ref)

    kc = kc_ref[...]
    vct = vct_ref[...]
    for slot, (q_ref, rc_ref, zc_ref, gl_ref) in enumerate(
            ((qa_ref, rca_ref, zca_ref, gla_ref), (qb_ref, rcb_ref, zcb_ref, glb_ref))):
        psum = jnp.zeros((N_CMP, TILE), F32)
        gates_ref[slot] = jax.nn.sigmoid(gl_ref[...].T)
        for h in range(heads):
            qt = q_ref[h]
            qw_sc[slot, h] = qt
            qa_sc[slot, h, 0:HEAD_DIM, :] = qt
            bias = rc_ref[h]
            s = jnp.dot(kc, qt, preferred_element_type=F32) + bias
            m = jnp.max(s, axis=0, keepdims=True)
            p = jnp.where(bias > 0.5 * NEG, jnp.exp2(s - m), 0.0)
            p = p / jnp.maximum(jnp.sum(p, axis=0, keepdims=True), 1e-30)
            psum = psum + p
            o_cmp = jnp.dot(vct, p.astype(BF16), preferred_element_type=F32)
            ycmp_ref[slot, h] = gates_ref[slot, h:h + 1, :] * o_cmp * zc_ref[h].astype(F32)

        imp = jnp.dot(ovt_ref[...], psum, precision=lax.Precision.HIGHEST, preferred_element_type=F32)
        blk = lax.broadcasted_iota(jnp.int32, (N_SLC, TILE), 0)
        tq = tiles[slot] * TILE + lax.broadcasted_iota(jnp.int32, (N_SLC, TILE), 1)
        cur = lax.shift_right_logical(tq, int(math.log2(SLC_BLOCK)))
        forced = (blk == 0) | (blk == cur) | (blk == cur - 1)
        score = jnp.where(forced, FORCE_SCORE, jnp.where(blk > cur, -FORCE_SCORE, imp))
        group = 8
        rows = [score[r:r + group, :] for r in range(0, N_SLC, group)]
        row_idx = lax.broadcasted_iota(jnp.int32, (group, TILE), 0)
        cnts = [jnp.zeros((group, TILE), jnp.int32) for _ in rows]
        for jj in range(N_SLC):
            sj = score[jj:jj + 1, :]
            for r, sr in enumerate(rows):
                lo = r * group
                if lo > jj:
                    beats = jnp.where(sj >= sr, 1, 0)
                elif lo + group - 1 <= jj:
                    beats = jnp.where(sj > sr, 1, 0)
                else:
                    beats = jnp.where(row_idx + lo > jj, jnp.where(sj >= sr, 1, 0), jnp.where(sj > sr, 1, 0))
                cnts[r] = cnts[r] + beats
        cnt = jnp.concatenate(cnts, axis=0)
        mask = jnp.where(cnt < SLC_TOP_N, 0.0, NEG)
        mask = jnp.concatenate([mask, jnp.zeros((HEAD_DIM - N_SLC, TILE), F32)], axis=0).astype(BF16)
        for h in range(heads):
            qa_sc[slot, h, HEAD_DIM:2 * HEAD_DIM, :] = mask

    _PairSoftmax(_causal_step(n), heads, qa_sc, lambda h, j: kaug_ref[_tile_rows(j), :],
                 lambda h, j: vst_ref[j], lambda h, delta: rs_ref[h, delta], s_ref, accs_ref).run(CAUSAL_STEPS)
    _PairSoftmax(_window_step(n), heads, qw_sc, lambda h, j: kw_ref[_tile_rows(j), :],
                 lambda h, j: vwt_ref[j], lambda h, b: rw_ref[h, b], s_ref, accw_ref).run_unrolled(2 * WIN_TILES)

    for slot, (zs_ref, zw_ref) in enumerate(((zsa_ref, zwa_ref), (zsb_ref, zwb_ref))):
        for h in range(heads):
            y = (ycmp_ref[slot, h]
                 + gates_ref[slot, heads + h:heads + h + 1, :] * _normalized(accs_ref[slot, h])
                 * zs_ref[h].astype(F32)
                 + gates_ref[slot, 2 * heads + h:2 * heads + h + 1, :] * _normalized(accw_ref[slot, h])
                 * zw_ref[h].astype(F32))
            o_ref[_tile_rows(tiles[slot]), h * HEAD_DIM:(h + 1) * HEAD_DIM] = y.T.astype(BF16)


def _nsa_attn(qz_t, gate_logits, kv_nat, kv_t, cmp_kv, table, table_c):
    batch = qz_t.shape[0]
    heads = NSA_HEADS_PER_GROUP
    groups = NSA_KV_GROUPS
    _, _, _, _, ov_t, onehot = _static_maps()

    def tile_of(n, second):
        return N_TILES - 1 - n if second else n

    def qz_specs(slot_group):
        return [pl.BlockSpec((None, heads, HEAD_DIM, TILE),
                             lambda g, b, n, second=second: (b, slot_group * groups + g, 0, tile_of(n, second)))
                for second in (False, True)]

    def key_spec(branch):
        return pl.BlockSpec((None, None, SEQ, HEAD_DIM), lambda g, b, n: (b, branch * groups + g, 0, 0))

    def value_spec(branch):
        return pl.BlockSpec((None, None, HEAD_DIM, SEQ), lambda g, b, n: (b, branch * groups + g, 0, 0))

    def cmp_spec(t):
        return pl.BlockSpec((None, None, None, N_CMP, HEAD_DIM), lambda g, b, n: (t, b, g, 0, 0))

    gl_specs = [pl.BlockSpec((None, TILE, HEAD_DIM), lambda g, b, n, second=second: (b, tile_of(n, second), g))
                for second in (False, True)]
    rc_specs = [pl.BlockSpec((heads, None, N_CMP, TILE), lambda g, b, n, second=second: (g, tile_of(n, second), 0, 0))
                for second in (False, True)]

    return pl.pallas_call(
        _nsa_kernel,
        grid=(groups, batch, N_PAIRS),
        in_specs=[
            *qz_specs(0), *qz_specs(1), *qz_specs(2), *qz_specs(3), *gl_specs,
            cmp_spec(0), cmp_spec(1), key_spec(0), value_spec(0), key_spec(1), value_spec(1),
            _resident((heads, N_TILES, TILE, TILE), lambda g, b, n: (g, TAB_S // N_TILES, 0, 0)),
            _resident((heads, TAB_W_ROWS, TILE, TILE), lambda g, b, n: (g, TAB_W // TAB_W_ROWS, 0, 0)),
            *rc_specs,
            _resident((N_SLC, N_CMP), lambda g, b, n: (0, 0)),
            _resident((SEQ, HEAD_DIM), lambda g, b, n: (0, 0)),
        ],
        out_specs=pl.BlockSpec((None, SEQ, heads * HEAD_DIM), lambda g, b, n: (b, 0, g)),
        out_shape=jax.ShapeDtypeStruct((batch, SEQ, ATT_WIDTH), BF16),
        scratch_shapes=[
            pltpu.VMEM((SEQ, 2 * HEAD_DIM), BF16),
            pltpu.VMEM((N_TILES, VT_ROWS, TILE), BF16),
            pltpu.VMEM((N_TILES, VT_ROWS, TILE), BF16),
            pltpu.VMEM((2, heads, 2 * HEAD_DIM, TILE), BF16),
            pltpu.VMEM((2, heads, HEAD_DIM, TILE), BF16),
            pltpu.VMEM((2, HEAD_DIM, TILE), F32),
            pltpu.VMEM((2, heads, HEAD_DIM, TILE), F32),
            pltpu.VMEM((heads, CAUSAL_STEPS, TILE, TILE), F32),
            pltpu.VMEM((2, heads, VT_ROWS, TILE), F32),
            pltpu.VMEM((2, heads, VT_ROWS, TILE), F32),
        ],
        compiler_params=pltpu.CompilerParams(dimension_semantics=("parallel", "parallel", "arbitrary"),
                                             vmem_limit_bytes=VMEM_LIMIT),
        name="nsa_attn",
    )(qz_t, qz_t, qz_t, qz_t, qz_t, qz_t, qz_t, qz_t, gate_logits, gate_logits, cmp_kv, cmp_kv,
      kv_nat, kv_t, kv_nat, kv_t, table, table, table_c, table_c,
      jnp.asarray(ov_t), jnp.asarray(onehot, dtype=BF16))


def kernel(x, norm_pre, norm_post, rel_table, w_in_a, w_out_a, kv_norm, w_kv, cmp_pos_k, cmp_pos_v,
           cmp_w1_k, cmp_w2_k, cmp_w1_v, cmp_w2_v, w_in_b, w_out_b):
    batch = x.shape[0]
    m = batch * SEQ
    heads = NSA_HEADS_PER_GROUP
    groups = NSA_KV_GROUPS
    x2d = x.reshape(m, D_MODEL)
    table, table_c = _bias_tables(rel_table)

    tile_slots = 8
    tiles_per_part = ATT_WIDTH // (tile_slots * HEAD_DIM)

    kinds_a = [(kind,) * tile_slots for kind in ("tq", "n", "t", "ts") for _ in range(tiles_per_part)]
    k_nat, qvz_t = _norm_matmul(x2d, norm_pre[0], w_in_a[0], tile_kinds=kinds_a)
    att_a = _attn_a(k_nat, qvz_t, table)
    h1 = _out_proj(att_a.reshape(m, ATT_WIDTH), w_out_a[0], x2d, norm_post[0])

    kinds_kv = [("c",) * tile_slots] + [("n",) * groups + ("t",) * groups] * 2
    kv_nat, kv_t, cmp_chunks = _norm_matmul(h1, kv_norm, w_kv, tile_kinds=kinds_kv)
    pos = jnp.stack([cmp_pos_k, cmp_pos_v]).reshape(2, 2, 1, CMP_CHUNK)
    cmp_kv = _compress(cmp_chunks, pos, jnp.stack([cmp_w1_k, cmp_w1_v]).astype(BF16),
                       jnp.stack([cmp_w2_k, cmp_w2_v]).astype(BF16))

    w_bt = w_in_b[0].T
    n_main = (1 + NSA_BRANCHES) * ATT_WIDTH
    wg = w_bt[n_main:].reshape(NSA_BRANCHES, groups, heads, D_MODEL).transpose(1, 0, 2, 3)
    wg = jnp.pad(wg.reshape(groups, NSA_BRANCHES * heads, D_MODEL),
                 ((0, 0), (0, HEAD_DIM - NSA_BRANCHES * heads), (0, 0))).reshape(groups * HEAD_DIM, D_MODEL)
    kinds_b = [(kind,) * tile_slots for kind in ("tq",) + ("ts",) * NSA_BRANCHES for _ in range(tiles_per_part)]
    qz_t, gate_logits = _norm_matmul(h1, norm_pre[1], w_bt, wg.T.astype(BF16), tile_kinds=kinds_b,
                                     w_transposed=True)
    att_b = _nsa_attn(qz_t, gate_logits.reshape(batch, SEQ, groups * HEAD_DIM), kv_nat, kv_t, cmp_kv,
                      table, table_c)
    out = _out_proj(att_b.reshape(m, ATT_WIDTH), w_out_b[0], h1, norm_post[1])
    return out.reshape(batch, SEQ, D_MODEL)
```

```python
import functools
import math

import numpy as np
import jax
import jax.numpy as jnp
from jax import lax
from jax.experimental import pallas as pl
from jax.experimental.pallas import tpu as pltpu

F32 = jnp.float32
BF16 = jnp.bfloat16

D_MODEL = 2048
SEQ = 2048
N_HEADS = 16
HEAD_DIM = 128
ATT_WIDTH = N_HEADS * HEAD_DIM
ATT_SCALE = HEAD_DIM ** -0.5
DIL_PATTERNS = ((128, 1), (512, 4), (2048, 16))
REL_BUCKETS = 32
REL_MAX_EXACT = 16
REL_MAX_DISTANCE = 2048
NSA_KV_GROUPS = 4
NSA_HEADS_PER_GROUP = 4
NSA_BRANCHES = 3
CMP_BLOCK = 32
CMP_STRIDE = 16
CMP_HIDDEN = 256
SLC_BLOCK = 64
SLC_TOP_N = 16
WIN_SIZE = 512
RMS_EPS = 1e-6
NEG = -1e30
FORCE_SCORE = 1e9
LOG2E = math.log2(math.e)

TILE = 256
N_TILES = SEQ // TILE
N_PAIRS = N_TILES // 2
CAUSAL_STEPS = N_TILES + 1
STEP_UNROLL = 3
N_CMP = SEQ // CMP_STRIDE
CMP_CHUNK = CMP_STRIDE * HEAD_DIM
N_SLC = SEQ // SLC_BLOCK
WIN_TILES = -(-(WIN_SIZE - 1) // TILE) + 1
HEADS_PER_STEP = 4
VT_ROWS = HEAD_DIM + 16
MXU_COLS = 512

TAB_A, TAB_S, TAB_W = 0, N_TILES, 2 * N_TILES
TAB_W_ROWS = 4
TAB_ROWS = 2 * N_TILES + TAB_W_ROWS
GATHER_MIN_BUCKETS = 6
VMEM_LIMIT = 56 * 1024 * 1024


def _np_bucket(dist):
    n = np.maximum(dist, 0)
    nf = np.maximum(n, 1).astype(np.float32)
    log_b = REL_MAX_EXACT + (
        np.log(nf / np.float32(REL_MAX_EXACT)) / np.float32(math.log(REL_MAX_DISTANCE / REL_MAX_EXACT))
        * np.float32(REL_BUCKETS - REL_MAX_EXACT)).astype(np.int32)
    return np.where(n < REL_MAX_EXACT, n, np.minimum(log_b, REL_BUCKETS - 1)).astype(np.int32)


@functools.lru_cache(maxsize=None)
def _static_maps():
    ki = np.arange(TILE)[:, None]
    qi = np.arange(TILE)[None, :]
    dist = TILE * np.arange(N_TILES)[:, None, None] + qi[None] - ki[None]
    bk_t = _np_bucket(dist)
    mult = np.zeros(dist.shape, np.int64)
    for window, dil in DIL_PATTERNS:
        mult += ((dist % dil == 0) & (dist <= window)).astype(np.int64)
    base_a = np.where((dist >= 0) & (mult > 0), np.log2(np.maximum(mult, 1)), NEG)
    base_s = np.where(dist >= 0, 0.0, NEG)
    base_w = np.full((TAB_W_ROWS, TILE, TILE), NEG)
    base_w[:WIN_TILES] = np.where((dist >= 0) & (dist < WIN_SIZE), 0.0, NEG)[:WIN_TILES]
    base_t = np.concatenate([base_a, base_s, base_w]).astype(np.float32)
    ci = np.arange(N_CMP)[:, None]
    t = TILE * np.arange(N_TILES)[:, None, None] + qi[None]
    dist_c = t - (CMP_STRIDE * ci[None] + CMP_BLOCK - 1)
    bk_c = _np_bucket(dist_c)
    base_c = np.where((dist_c >= 0) & (ci[None] < N_CMP - 1), 0.0, NEG).astype(np.float32)
    cs = np.arange(N_CMP)[None, :] * CMP_STRIDE
    sj = np.arange(N_SLC)[:, None] * SLC_BLOCK
    ov_t = ((cs < sj + SLC_BLOCK) & (cs + CMP_BLOCK > sj) & (np.arange(N_CMP)[None, :] < N_CMP - 1))
    onehot = np.zeros((SEQ, HEAD_DIM), np.float32)
    onehot[np.arange(SEQ), np.arange(SEQ) // SLC_BLOCK] = 1.0
    return bk_t.astype(np.int32), base_t, bk_c.astype(np.int32), base_c, ov_t.astype(np.float32), onehot


def _bias_kernel(tab_ref, tabt_ref, bkt_ref, baset_ref, bkc_ref, basec_ref, out_ref, outc_ref, *,
                 present_t, present_c):
    h = pl.program_id(0)
    tv = [tab_ref[b, h] * LOG2E for b in range(REL_BUCKETS)]
    lanes = 128
    row = jnp.broadcast_to(tabt_ref[pl.ds(h, 1), :] * LOG2E, (TILE, lanes))

    def lookup(bk, present):
        if len(present) > GATHER_MIN_BUCKETS:
            parts = [jnp.take_along_axis(row[:bk.shape[0]], bk[:, c:c + lanes], axis=1)
                     for c in range(0, bk.shape[1], lanes)]
            return jnp.concatenate(parts, axis=1)
        val = jnp.full(bk.shape, tv[present[0]], F32)
        for b in present[1:]:
            val = jnp.where(bk == b, tv[b], val)
        return val

    for d in range(N_TILES):
        g = lookup(bkt_ref[d], present_t[d])
        out_ref[TAB_A + d] = g + baset_ref[TAB_A + d]
        out_ref[TAB_S + d] = g + baset_ref[TAB_S + d]
        if d < TAB_W_ROWS:
            out_ref[TAB_W + d] = g + baset_ref[TAB_W + d]
        outc_ref[d] = lookup(bkc_ref[d], present_c[d]) + basec_ref[d]


def _bias_tables(rel_table):
    bk_t, base_t, bk_c, base_c, _, _ = _static_maps()
    present_t = tuple(tuple(int(b) for b in np.unique(bk_t[d])) for d in range(N_TILES))
    present_c = tuple(tuple(int(b) for b in np.unique(bk_c[d])) for d in range(N_TILES))
    return pl.pallas_call(
        functools.partial(_bias_kernel, present_t=present_t, present_c=present_c),
        grid=(N_HEADS,),
        in_specs=[
            pl.BlockSpec(memory_space=pltpu.SMEM),
            pl.BlockSpec((N_HEADS, 128), lambda h: (0, 0)),
            pl.BlockSpec((N_TILES, TILE, TILE), lambda h: (0, 0, 0)),
            pl.BlockSpec((TAB_ROWS, TILE, TILE), lambda h: (0, 0, 0)),
            pl.BlockSpec((N_TILES, N_CMP, TILE), lambda h: (0, 0, 0)),
            pl.BlockSpec((N_TILES, N_CMP, TILE), lambda h: (0, 0, 0)),
        ],
        out_specs=[
            pl.BlockSpec((None, TAB_ROWS, TILE, TILE), lambda h: (h, 0, 0, 0)),
            pl.BlockSpec((None, N_TILES, N_CMP, TILE), lambda h: (h, 0, 0, 0)),
        ],
        out_shape=[
            jax.ShapeDtypeStruct((N_HEADS, TAB_ROWS, TILE, TILE), F32),
            jax.ShapeDtypeStruct((N_HEADS, N_TILES, N_CMP, TILE), F32),
        ],
        compiler_params=pltpu.CompilerParams(dimension_semantics=("arbitrary",),
                                             vmem_limit_bytes=VMEM_LIMIT),
        name="bias_tables",
    )(rel_table, jnp.pad(rel_table.T, ((0, 0), (0, 128 - REL_BUCKETS))),
      jnp.asarray(bk_t), jnp.asarray(base_t), jnp.asarray(bk_c), jnp.asarray(base_c))


def _silu(z):
    half = 0.5 * z
    return half + half * jnp.tanh(half)


TRANSPOSED_KINDS = ("t", "tq", "ts")


def _nm_kernel(*refs, tile_kinds, gate, w_transposed):
    refs = list(refs)
    x_ref, g_ref, w_ref = refs[:3]
    del refs[:3]
    wg_ref = refs.pop(0) if gate else None
    kinds_used = {k for kinds in tile_kinds for k in kinds}
    on_ref = refs.pop(0) if "n" in kinds_used else None
    ot_ref = refs.pop(0) if kinds_used & set(TRANSPOSED_KINDS) else None
    og_ref = refs.pop(0) if gate else None
    oc_ref = refs.pop(0) if "c" in kinds_used else None
    xn_ref = refs.pop(0)
    res_ref = refs.pop(0) if "c" in kinds_used else None
    j = pl.program_id(1)

    @pl.when(j == 0)
    def _():
        x = x_ref[...]
        ms = jnp.mean(x * x, axis=-1, keepdims=True)
        xn_ref[...] = (x * lax.rsqrt(ms + RMS_EPS) * g_ref[...]).astype(BF16)
        if gate:
            og_ref[...] = jnp.dot(xn_ref[...], wg_ref[...], preferred_element_type=F32)

    def product(kinds):
        pos = {"n": 0, "t": 0}
        per_group = MXU_COLS // HEAD_DIM
        for cc in range(len(kinds) // per_group):
            if w_transposed:
                w_cols = w_ref[cc * MXU_COLS:(cc + 1) * MXU_COLS, :].astype(BF16)
                res = lax.dot_general(xn_ref[...], w_cols, (((1,), (1,)), ((), ())), preferred_element_type=F32)
            else:
                w_cols = w_ref[:, cc * MXU_COLS:(cc + 1) * MXU_COLS].astype(BF16)
                res = jnp.dot(xn_ref[...], w_cols, preferred_element_type=F32)
            for u in range(per_group):
                c = per_group * cc + u
                piece = res[:, u * HEAD_DIM:(u + 1) * HEAD_DIM]
                kind = kinds[c]
                if kind == "n":
                    on_ref[pos["n"]] = piece.astype(BF16)
                    pos["n"] += 1
                elif kind == "c":
                    res_ref[c] = piece
                else:
                    if kind == "tq":
                        piece = piece * (ATT_SCALE * LOG2E)
                    elif kind == "ts":
                        piece = _silu(piece)
                    ot_ref[pos["t"]] = piece.T.astype(BF16)
                    pos["t"] += 1
        if "c" in kinds:
            rows = res_ref.shape[1] // CMP_STRIDE
            for c in range(len(kinds)):
                for i in range(CMP_STRIDE):
                    oc_ref[c, :, i * HEAD_DIM:(i + 1) * HEAD_DIM] = res_ref[
                        c, pl.ds(i, rows, stride=CMP_STRIDE), :].astype(BF16)

    for kinds in sorted(set(tile_kinds)):
        tiles = [t for t, k in enumerate(tile_kinds) if k == kinds]
        cond = functools.reduce(lambda a, b: a | b, [j == t for t in tiles])
        pl.when(cond)(functools.partial(product, kinds))


def _norm_matmul(x2d, gain, w, wg_bf=None, *, tile_kinds, w_transposed=False, tm=1024, tn=1024):
    m, d = x2d.shape
    slots = tn // HEAD_DIM
    batch = m // SEQ
    spb = SEQ // tm
    gate = wg_bf is not None
    assert all(len(k) == slots for k in tile_kinds)

    def family(match):
        counts = [sum(1 for k in kinds if match(k)) for kinds in tile_kinds]
        tiles = [t for t, c in enumerate(counts) if c]
        per_tile = counts[tiles[0]] if tiles else 0
        assert all(counts[t] == per_tile for t in tiles)

        def block(j):
            return jnp.minimum(sum(jnp.where(j > t, 1, 0) for t in tiles), len(tiles) - 1)
        return tiles, per_tile, block

    n_tiles, n_per, n_block = family(lambda k: k == "n")
    t_tiles, t_per, t_block = family(lambda k: k in TRANSPOSED_KINDS)
    c_tiles, c_per, _ = family(lambda k: k == "c")
    assert c_tiles in ([], [0]) and c_per in (0, slots)

    in_specs = [
        pl.BlockSpec((tm, d), lambda i, j: (i, 0)),
        pl.BlockSpec((1, d), lambda i, j: (0, 0)),
        pl.BlockSpec((tn, d), lambda i, j: (j, 0)) if w_transposed else pl.BlockSpec((d, tn), lambda i, j: (0, j)),
    ]
    args = [x2d, gain.reshape(1, d), w]
    out_shape, out_specs = [], []
    scratch = [pltpu.VMEM((tm, d), BF16)]
    if gate:
        in_specs.append(pl.BlockSpec((d, wg_bf.shape[1]), lambda i, j: (0, 0)))
        args.append(wg_bf)
    if n_tiles:
        out_shape.append(jax.ShapeDtypeStruct((batch, n_per * len(n_tiles), SEQ, HEAD_DIM), BF16))
        out_specs.append(pl.BlockSpec((None, n_per, tm, HEAD_DIM),
                                      lambda i, j: (i // spb, n_block(j), i % spb, 0)))
    if t_tiles:
        out_shape.append(jax.ShapeDtypeStruct((batch, t_per * len(t_tiles), HEAD_DIM, SEQ), BF16))
        out_specs.append(pl.BlockSpec((None, t_per, HEAD_DIM, tm),
                                      lambda i, j: (i // spb, t_block(j), 0, i % spb)))
    if gate:
        out_shape.append(jax.ShapeDtypeStruct((m, wg_bf.shape[1]), F32))
        out_specs.append(pl.BlockSpec((tm, wg_bf.shape[1]), lambda i, j: (i, 0)))
    if c_tiles:
        out_shape.append(jax.ShapeDtypeStruct((batch, slots, N_CMP, CMP_CHUNK), BF16))
        out_specs.append(pl.BlockSpec((None, slots, tm // CMP_STRIDE, CMP_CHUNK),
                                      lambda i, j: (i // spb, 0, i % spb, 0)))
        scratch.append(pltpu.VMEM((slots, tm, HEAD_DIM), F32))
    return pl.pallas_call(
        functools.partial(_nm_kernel, tile_kinds=tuple(tuple(k) for k in tile_kinds), gate=gate,
                          w_transposed=w_transposed),
        grid=(m // tm, len(tile_kinds)),
        in_specs=in_specs,
        out_specs=out_specs,
        out_shape=out_shape,
        scratch_shapes=scratch,
        compiler_params=pltpu.CompilerParams(dimension_semantics=("parallel", "arbitrary"),
                                             vmem_limit_bytes=VMEM_LIMIT),
        name="norm_matmul",
    )(*args)


def _op_kernel(a_ref, w_ref, r_ref, g_ref, o_ref, wbf_ref, *, parts):
    @pl.when(pl.program_id(0) == 0)
    def _():
        step = 256
        for r in range(w_ref.shape[0] // step):
            wbf_ref[r * step:(r + 1) * step, :] = w_ref[r * step:(r + 1) * step, :].astype(BF16)

    rows = a_ref.shape[0] // parts
    for r in range(parts):
        sl = slice(r * rows, (r + 1) * rows)
        y = jnp.dot(a_ref[sl, :], wbf_ref[...], preferred_element_type=F32)
        ms = jnp.mean(y * y, axis=-1, keepdims=True)
        o_ref[sl, :] = r_ref[sl, :] + y * lax.rsqrt(ms + RMS_EPS) * g_ref[...]


def _out_proj(a2d, w, resid2d, gain, *, tm=512, parts=2):
    m, k = a2d.shape
    n = w.shape[1]
    return pl.pallas_call(
        functools.partial(_op_kernel, parts=parts),
        grid=(m // tm,),
        in_specs=[
            pl.BlockSpec((tm, k), lambda i: (i, 0)),
            pl.BlockSpec((k, n), lambda i: (0, 0), pipeline_mode=pl.Buffered(1)),
            pl.BlockSpec((tm, n), lambda i: (i, 0)),
            pl.BlockSpec((1, n), lambda i: (0, 0)),
        ],
        out_specs=pl.BlockSpec((tm, n), lambda i: (i, 0)),
        out_shape=jax.ShapeDtypeStruct((m, n), F32),
        scratch_shapes=[pltpu.VMEM((k, n), BF16)],
        compiler_params=pltpu.CompilerParams(dimension_semantics=("arbitrary",),
                                             vmem_limit_bytes=VMEM_LIMIT),
        name="out_proj",
    )(a2d, w, resid2d, gain.reshape(1, n))


def _value_tiles(src_ref, dst_ref):
    for c in range(N_TILES):
        dst_ref[c, 0:HEAD_DIM, :] = src_ref[:, c * TILE:(c + 1) * TILE]
        dst_ref[c, HEAD_DIM:VT_ROWS, :] = jnp.ones((VT_ROWS - HEAD_DIM, TILE), BF16)


def _pair_tiles(n):
    return (n, N_TILES - 1 - n)


def _tile_rows(idx):
    return pl.ds(pl.multiple_of(idx * TILE, TILE), TILE)


def _normalized(acc):
    return acc[0:HEAD_DIM, :] / acc[HEAD_DIM:HEAD_DIM + 1, :]


class _PairSoftmax:
    def __init__(self, step_fn, heads, q_ref, k_fn, vt_fn, bias_fn, s_ref, acc_ref):
        self.step_fn, self.heads, self.q_ref, self.k_fn, self.vt_fn = step_fn, heads, q_ref, k_fn, vt_fn
        self.bias_fn, self.s_ref, self.acc_ref = bias_fn, s_ref, acc_ref

    def initial_max(self):
        neg = (jnp.full((1, TILE), NEG, F32),) * self.heads
        return neg, neg

    def scores(self, t, ms):
        slot, j, b = self.step_fn(t)
        first = slot == 0
        m0, m1 = list(ms[0]), list(ms[1])
        for h in range(self.heads):
            s = (jnp.dot(self.k_fn(h, j), self.q_ref[slot, h], preferred_element_type=F32)
                 + self.bias_fn(h, b))
            self.s_ref[h, t] = s
            cm = jnp.max(s, axis=0, keepdims=True)
            m0[h] = jnp.where(first, jnp.maximum(m0[h], cm), m0[h])
            m1[h] = jnp.where(first, m1[h], jnp.maximum(m1[h], cm))
        return tuple(m0), tuple(m1)

    def clear(self):
        for slot in range(2):
            for h in range(self.heads):
                self.acc_ref[slot, h] = jnp.zeros((VT_ROWS, TILE), F32)

    def weighted(self, t, ms):
        slot, j, _ = self.step_fn(t)
        first = slot == 0
        for h in range(self.heads):
            p = jnp.exp2((self.s_ref[h, t] - jnp.where(first, ms[0][h], ms[1][h])).astype(BF16))
            self.acc_ref[slot, h] += jnp.dot(self.vt_fn(h, j), p, preferred_element_type=F32)

    def run(self, steps):
        ms = lax.fori_loop(0, steps, self.scores, self.initial_max(), unroll=True)
        self.clear()

        def body(t, carry):
            self.weighted(t, ms)
            return carry

        lax.fori_loop(0, steps, body, 0, unroll=STEP_UNROLL)

    def run_unrolled(self, steps):
        ms = self.initial_max()
        for t in range(steps):
            ms = self.scores(t, ms)
        self.clear()
        for t in range(steps):
            self.weighted(t, ms)


def _causal_step(n):
    def step(t):
        first = t <= n
        return (jnp.where(first, 0, 1), jnp.where(first, t, t - n - 1), jnp.where(first, n - t, N_TILES - t))
    return step


def _window_step(n):
    def step(t):
        if isinstance(t, int):
            slot, d = divmod(t, WIN_TILES)
            idx = _pair_tiles(n)[slot]
        else:
            slot = jnp.where(t >= WIN_TILES, 1, 0)
            d = t - WIN_TILES * slot
            idx = jnp.where(slot == 0, n, N_TILES - 1 - n)
        j = idx - d
        return slot, jnp.maximum(j, 0), jnp.where(j >= 0, d, TAB_W_ROWS - 1)
    return step


def _attn_a_kernel(qa_ref, qb_ref, k_ref, v_ref, za_ref, zb_ref, r_ref, o_ref,
                   vt_ref, qt_ref, s_ref, acc_ref):
    n = pl.program_id(2)
    heads = HEADS_PER_STEP

    @pl.when(n == 0)
    def _():
        for h in range(heads):
            _value_tiles(v_ref.at[h], vt_ref.at[h])

    for slot, q_ref in enumerate((qa_ref, qb_ref)):
        for h in range(heads):
            qt_ref[slot, h] = q_ref[h]

    _PairSoftmax(_causal_step(n), heads, qt_ref, lambda h, j: k_ref[h, _tile_rows(j), :],
                 lambda h, j: vt_ref[h, j], lambda h, delta: r_ref[h, delta], s_ref, acc_ref).run(CAUSAL_STEPS)
    for slot, (z_ref, idx) in enumerate(zip((za_ref, zb_ref), _pair_tiles(n))):
        for h in range(heads):
            y = (_normalized(acc_ref[slot, h]) * z_ref[h].astype(F32)).T
            o_ref[_tile_rows(idx), h * HEAD_DIM:(h + 1) * HEAD_DIM] = y.astype(BF16)


def _resident(block_shape, index_map):
    return pl.BlockSpec(block_shape, index_map, pipeline_mode=pl.Buffered(1))


def _attn_a(k_nat, qvz_t, table):
    batch = k_nat.shape[0]
    heads = HEADS_PER_STEP
    hgroups = N_HEADS // heads

    def tile_slot(s, second):
        def index(hg, b, n):
            return (b, s * hgroups + hg, 0, N_TILES - 1 - n if second else n)
        return pl.BlockSpec((None, heads, HEAD_DIM, TILE), index)

    return pl.pallas_call(
        _attn_a_kernel,
        grid=(hgroups, batch, N_PAIRS),
        in_specs=[tile_slot(0, False), tile_slot(0, True),
                  pl.BlockSpec((None, heads, SEQ, HEAD_DIM), lambda hg, b, n: (b, hg, 0, 0)),
                  pl.BlockSpec((None, heads, HEAD_DIM, SEQ), lambda hg, b, n: (b, hgroups + hg, 0, 0)),
                  tile_slot(2, False), tile_slot(2, True),
                  pl.BlockSpec((heads, N_TILES, TILE, TILE), lambda hg, b, n: (hg, TAB_A // N_TILES, 0, 0))],
        out_specs=pl.BlockSpec((None, SEQ, heads * HEAD_DIM), lambda hg, b, n: (b, 0, hg)),
        out_shape=jax.ShapeDtypeStruct((batch, SEQ, ATT_WIDTH), BF16),
        scratch_shapes=[
            pltpu.VMEM((heads, N_TILES, VT_ROWS, TILE), BF16),
            pltpu.VMEM((2, heads, HEAD_DIM, TILE), BF16),
            pltpu.VMEM((heads, CAUSAL_STEPS, TILE, TILE), F32),
            pltpu.VMEM((2, heads, VT_ROWS, TILE), F32),
        ],
        compiler_params=pltpu.CompilerParams(dimension_semantics=("parallel", "parallel", "arbitrary"),
                                             vmem_limit_bytes=VMEM_LIMIT),
        name="dilated_attn",
    )(qvz_t, qvz_t, k_nat, qvz_t, qvz_t, qvz_t, table)


def _cmp_kernel(c_ref, pos_ref, w1_ref, w2_ref, o_ref):
    c = c_ref[...].astype(F32)
    x_lo = (c + pos_ref[0]).astype(BF16)
    x_hi = (c + pos_ref[1]).astype(BF16)
    a = jnp.dot(x_lo, w1_ref[0:CMP_CHUNK, :], preferred_element_type=F32)
    bm = jnp.dot(x_hi, w1_ref[CMP_CHUNK:2 * CMP_CHUNK, :], preferred_element_type=F32)
    hid = jax.nn.gelu(a + pltpu.roll(bm, N_CMP - 1, 0))
    res = jnp.dot(hid.astype(BF16), w2_ref[...], preferred_element_type=F32)
    o_ref[...] = jnp.where(pl.program_id(0) == 1, res.T, res).astype(BF16)


def _compress(chunks, pos, w1_bf, w2_bf):
    batch = chunks.shape[0]
    return pl.pallas_call(
        _cmp_kernel,
        grid=(2, batch, NSA_KV_GROUPS),
        in_specs=[
            pl.BlockSpec((None, None, N_CMP, CMP_CHUNK), lambda t, b, g: (b, t * NSA_KV_GROUPS + g, 0, 0)),
            pl.BlockSpec((None, 2, 1, CMP_CHUNK), lambda t, b, g: (t, 0, 0, 0)),
            pl.BlockSpec((None, 2 * CMP_CHUNK, CMP_HIDDEN), lambda t, b, g: (t, 0, 0)),
            pl.BlockSpec((None, CMP_HIDDEN, HEAD_DIM), lambda t, b, g: (t, 0, 0)),
        ],
        out_specs=pl.BlockSpec((None, None, None, N_CMP, HEAD_DIM), lambda t, b, g: (t, b, g, 0, 0)),
        out_shape=jax.ShapeDtypeStruct((2, batch, NSA_KV_GROUPS, N_CMP, HEAD_DIM), BF16),
        compiler_params=pltpu.CompilerParams(dimension_semantics=("arbitrary", "arbitrary", "arbitrary"),
                                             vmem_limit_bytes=VMEM_LIMIT),
        name="compress_kv",
    )(chunks, pos, w1_bf, w2_bf)


def _nsa_kernel(qa_ref, qb_ref, zca_ref, zcb_ref, zsa_ref, zsb_ref, zwa_ref, zwb_ref, gla_ref, glb_ref,
                kc_ref, vct_ref, ks_ref, vs_ref, kw_ref, vw_ref, rs_ref, rw_ref, rca_ref, rcb_ref,
                ovt_ref, onehot_ref, o_ref,
                kaug_ref, vst_ref, vwt_ref, qa_sc, qw_sc, gates_ref, ycmp_ref, s_ref, accs_ref, accw_ref):
    n = pl.program_id(2)
    heads = NSA_HEADS_PER_GROUP
    tiles = _pair_tiles(n)

    @pl.when(n == 0)
    def _():
        kaug_ref[:, 0:HEAD_DIM] = ks_ref[...]
        kaug_ref[:, HEAD_DIM:2 * HEAD_DIM] = onehot_ref[...]
        _value_tiles(vs_ref, vst_ref)
        _value_tiles(vw_ref, vwt_ref)

    kc = kc_ref[...]
    vct = vct_ref[...]
    for slot, (q_ref, rc_ref, zc_ref, gl_ref) in enumerate(
            ((qa_ref, rca_ref, zca_ref, gla_ref), (qb_ref, rcb_ref, zcb_ref, glb_ref))):
        psum = jnp.zeros((N_CMP, TILE), F32)
        gates_ref[slot] = jax.nn.sigmoid(gl_ref[...].T)
        for h in range(heads):
            qt = q_ref[h]
            qw_sc[slot, h] = qt
            qa_sc[slot, h, 0:HEAD_DIM, :] = qt
            bias = rc_ref[h]
            s = jnp.dot(kc, qt, preferred_element_type=F32) + bias
            m = jnp.max(s, axis=0, keepdims=True)
            p = jnp.where(bias > 0.5 * NEG, jnp.exp2(s - m), 0.0)
            p = p / jnp.maximum(jnp.sum(p, axis=0, keepdims=True), 1e-30)
            psum = psum + p
            o_cmp = jnp.dot(vct, p.astype(BF16), preferred_element_type=F32)
            ycmp_ref[slot, h] = gates_ref[slot, h:h + 1, :] * o_cmp * zc_ref[h].astype(F32)

        imp = jnp.dot(ovt_ref[...], psum, precision=lax.Precision.HIGHEST, preferred_element_type=F32)
        blk = lax.broadcasted_iota(jnp.int32, (N_SLC, TILE), 0)
        tq = tiles[slot] * TILE + lax.broadcasted_iota(jnp.int32, (N_SLC, TILE), 1)
        cur = lax.shift_right_logical(tq, int(math.log2(SLC_BLOCK)))
        forced = (blk == 0) | (blk == cur) | (blk == cur - 1)
        score = jnp.where(forced, FORCE_SCORE, jnp.where(blk > cur, -FORCE_SCORE, imp))
        group = 8
        rows = [score[r:r + group, :] for r in range(0, N_SLC, group)]
        row_idx = lax.broadcasted_iota(jnp.int32, (group, TILE), 0)
        cnts = [jnp.zeros((group, TILE), jnp.int32) for _ in rows]
        for jj in range(N_SLC):
            sj = score[jj:jj + 1, :]
            for r, sr in enumerate(rows):
                lo = r * group
                if lo > jj:
                    beats = jnp.where(sj >= sr, 1, 0)
                elif lo + group - 1 <= jj:
                    beats = jnp.where(sj > sr, 1, 0)
                else:
                    beats = jnp.where(row_idx + lo > jj, jnp.where(sj >= sr, 1, 0), jnp.where(sj > sr, 1, 0))
                cnts[r] = cnts[r] + beats
        cnt = jnp.concatenate(cnts, axis=0)
        mask = jnp.where(cnt < SLC_TOP_N, 0.0, NEG)
        mask = jnp.concatenate([mask, jnp.zeros((HEAD_DIM - N_SLC, TILE), F32)], axis=0).astype(BF16)
        for h in range(heads):
            qa_sc[slot, h, HEAD_DIM:2 * HEAD_DIM, :] = mask

    _PairSoftmax(_causal_step(n), heads, qa_sc, lambda h, j: kaug_ref[_tile_rows(j), :],
                 lambda h, j: vst_ref[j], lambda h, delta: rs_ref[h, delta], s_ref, accs_ref).run(CAUSAL_STEPS)
    _PairSoftmax(_window_step(n), heads, qw_sc, lambda h, j: kw_ref[_tile_rows(j), :],
                 lambda h, j: vwt_ref[j], lambda h, b: rw_ref[h, b], s_ref, accw_ref).run_unrolled(2 * WIN_TILES)

    for slot, (zs_ref, zw_ref) in enumerate(((zsa_ref, zwa_ref), (zsb_ref, zwb_ref))):
        for h in range(heads):
            y = (ycmp_ref[slot, h]
                 + gates_ref[slot, heads + h:heads + h + 1, :] * _normalized(accs_ref[slot, h])
                 * zs_ref[h].astype(F32)
                 + gates_ref[slot, 2 * heads + h:2 * heads + h + 1, :] * _normalized(accw_ref[slot, h])
                 * zw_ref[h].astype(F32))
            o_ref[_tile_rows(tiles[slot]), h * HEAD_DIM:(h + 1) * HEAD_DIM] = y.T.astype(BF16)


def _nsa_attn(qz_t, gate_logits, kv_nat, kv_t, cmp_kv, table, table_c):
    batch = qz_t.shape[0]
    heads = NSA_HEADS_PER_GROUP
    groups = NSA_KV_GROUPS
    _, _, _, _, ov_t, onehot = _static_maps()

    def tile_of(n, second):
        return N_TILES - 1 - n if second else n

    def qz_specs(slot_group):
        return [pl.BlockSpec((None, heads, HEAD_DIM, TILE),
                             lambda g, b, n, second=second: (b, slot_group * groups + g, 0, tile_of(n, second)))
                for second in (False, True)]

    def key_spec(branch):
        return pl.BlockSpec((None, None, SEQ, HEAD_DIM), lambda g, b, n: (b, branch * groups + g, 0, 0))

    def value_spec(branch):
        return pl.BlockSpec((None, None, HEAD_DIM, SEQ), lambda g, b, n: (b, branch * groups + g, 0, 0))

    def cmp_spec(t):
        return pl.BlockSpec((None, None, None, N_CMP, HEAD_DIM), lambda g, b, n: (t, b, g, 0, 0))

    gl_specs = [pl.BlockSpec((None, TILE, HEAD_DIM), lambda g, b, n, second=second: (b, tile_of(n, second), g))
                for second in (False, True)]
    rc_specs = [pl.BlockSpec((heads, None, N_CMP, TILE), lambda g, b, n, second=second: (g, tile_of(n, second), 0, 0))
                for second in (False, True)]

    return pl.pallas_call(
        _nsa_kernel,
        grid=(groups, batch, N_PAIRS),
        in_specs=[
            *qz_specs(0), *qz_specs(1), *qz_specs(2), *qz_specs(3), *gl_specs,
            cmp_spec(0), cmp_spec(1), key_spec(0), value_spec(0), key_spec(1), value_spec(1),
            pl.BlockSpec((heads, N_TILES, TILE, TILE), lambda g, b, n: (g, TAB_S // N_TILES, 0, 0)),
            pl.BlockSpec((heads, TAB_W_ROWS, TILE, TILE), lambda g, b, n: (g, TAB_W // TAB_W_ROWS, 0, 0)),
            *rc_specs,
            _resident((N_SLC, N_CMP), lambda g, b, n: (0, 0)),
            _resident((SEQ, HEAD_DIM), lambda g, b, n: (0, 0)),
        ],
        out_specs=pl.BlockSpec((None, SEQ, heads * HEAD_DIM), lambda g, b, n: (b, 0, g)),
        out_shape=jax.ShapeDtypeStruct((batch, SEQ, ATT_WIDTH), BF16),
        scratch_shapes=[
            pltpu.VMEM((SEQ, 2 * HEAD_DIM), BF16),
            pltpu.VMEM((N_TILES, VT_ROWS, TILE), BF16),
            pltpu.VMEM((N_TILES, VT_ROWS, TILE), BF16),
            pltpu.VMEM((2, heads, 2 * HEAD_DIM, TILE), BF16),
            pltpu.VMEM((2, heads, HEAD_DIM, TILE), BF16),
            pltpu.VMEM((2, HEAD_DIM, TILE), F32),
            pltpu.VMEM((2, heads, HEAD_DIM, TILE), F32),
            pltpu.VMEM((heads, CAUSAL_STEPS, TILE, TILE), F32),
            pltpu.VMEM((2, heads, VT_ROWS, TILE), F32),
            pltpu.VMEM((2, heads, VT_ROWS, TILE), F32),
        ],
        compiler_params=pltpu.CompilerParams(dimension_semantics=("parallel", "parallel", "arbitrary"),
                                             vmem_limit_bytes=VMEM_LIMIT),
        name="nsa_attn",
    )(qz_t, qz_t, qz_t, qz_t, qz_t, qz_t, qz_t, qz_t, gate_logits, gate_logits, cmp_kv, cmp_kv,
      kv_nat, kv_t, kv_nat, kv_t, table, table, table_c, table_c,
      jnp.asarray(ov_t), jnp.asarray(onehot, dtype=BF16))


def kernel(x, norm_pre, norm_post, rel_table, w_in_a, w_out_a, kv_norm, w_kv, cmp_pos_k, cmp_pos_v,
           cmp_w1_k, cmp_w2_k, cmp_w1_v, cmp_w2_v, w_in_b, w_out_b):
    batch = x.shape[0]
    m = batch * SEQ
    heads = NSA_HEADS_PER_GROUP
    groups = NSA_KV_GROUPS
    x2d = x.reshape(m, D_MODEL)
    table, table_c = _bias_tables(rel_table)

    tile_slots = 8
    tiles_per_part = ATT_WIDTH // (tile_slots * HEAD_DIM)

    kinds_a = [(kind,) * tile_slots for kind in ("tq", "n", "t", "ts") for _ in range(tiles_per_part)]
    k_nat, qvz_t = _norm_matmul(x2d, norm_pre[0], w_in_a[0], tile_kinds=kinds_a)
    att_a = _attn_a(k_nat, qvz_t, table)
    h1 = _out_proj(att_a.reshape(m, ATT_WIDTH), w_out_a[0], x2d, norm_post[0])

    kinds_kv = [("c",) * tile_slots] + [("n",) * groups + ("t",) * groups] * 2
    kv_nat, kv_t, cmp_chunks = _norm_matmul(h1, kv_norm, w_kv, tile_kinds=kinds_kv)
    pos = jnp.stack([cmp_pos_k, cmp_pos_v]).reshape(2, 2, 1, CMP_CHUNK)
    cmp_kv = _compress(cmp_chunks, pos, jnp.stack([cmp_w1_k, cmp_w1_v]).astype(BF16),
                       jnp.stack([cmp_w2_k, cmp_w2_v]).astype(BF16))

    w_bt = w_in_b[0].T
    n_main = (1 + NSA_BRANCHES) * ATT_WIDTH
    wg = w_bt[n_main:].reshape(NSA_BRANCHES, groups, heads, D_MODEL).transpose(1, 0, 2, 3)
    wg = jnp.pad(wg.reshape(groups, NSA_BRANCHES * heads, D_MODEL),
                 ((0, 0), (0, HEAD_DIM - NSA_BRANCHES * heads), (0, 0))).reshape(groups * HEAD_DIM, D_MODEL)
    kinds_b = [(kind,) * tile_slots for kind in ("tq",) + ("ts",) * NSA_BRANCHES for _ in range(tiles_per_part)]
    qz_t, gate_logits = _norm_matmul(h1, norm_pre[1], w_bt, wg.T.astype(BF16), tile_kinds=kinds_b,
                                     w_transposed=True)
    att_b = _nsa_attn(qz_t, gate_logits.reshape(batch, SEQ, groups * HEAD_DIM), kv_nat, kv_t, cmp_kv,
                      table, table_c)
    out = _out_proj(att_b.reshape(m, ATT_WIDTH), w_out_b[0], h1, norm_post[1])
    return out.reshape(batch, SEQ, D_MODEL)
```

```python
import functools
import math

import numpy as np
import jax
import jax.numpy as jnp
from jax import lax
from jax.experimental import pallas as pl
from jax.experimental.pallas import tpu as pltpu

F32 = jnp.float32
BF16 = jnp.bfloat16

D_MODEL = 2048
SEQ = 2048
N_HEADS = 16
HEAD_DIM = 128
ATT_WIDTH = N_HEADS * HEAD_DIM
ATT_SCALE = HEAD_DIM ** -0.5
DIL_PATTERNS = ((128, 1), (512, 4), (2048, 16))
REL_BUCKETS = 32
REL_MAX_EXACT = 16
REL_MAX_DISTANCE = 2048
NSA_KV_GROUPS = 4
NSA_HEADS_PER_GROUP = 4
NSA_BRANCHES = 3
CMP_BLOCK = 32
CMP_STRIDE = 16
CMP_HIDDEN = 256
SLC_BLOCK = 64
SLC_TOP_N = 16
WIN_SIZE = 512
RMS_EPS = 1e-6
NEG = -1e30
FORCE_SCORE = 1e9
LOG2E = math.log2(math.e)

TILE = 256
N_TILES = SEQ // TILE
N_PAIRS = N_TILES // 2
CAUSAL_STEPS = N_TILES + 1
STEP_UNROLL = 3
N_CMP = SEQ // CMP_STRIDE
CMP_CHUNK = CMP_STRIDE * HEAD_DIM
N_SLC = SEQ // SLC_BLOCK
WIN_TILES = -(-(WIN_SIZE - 1) // TILE) + 1
HEADS_PER_STEP = 4
VT_ROWS = HEAD_DIM + 16
MXU_COLS = 512

TAB_A, TAB_S, TAB_W = 0, N_TILES, 2 * N_TILES
TAB_W_ROWS = 4
TAB_ROWS = 2 * N_TILES + TAB_W_ROWS
GATHER_MIN_BUCKETS = 6
VMEM_LIMIT = 56 * 1024 * 1024


def _np_bucket(dist):
    n = np.maximum(dist, 0)
    nf = np.maximum(n, 1).astype(np.float32)
    log_b = REL_MAX_EXACT + (
        np.log(nf / np.float32(REL_MAX_EXACT)) / np.float32(math.log(REL_MAX_DISTANCE / REL_MAX_EXACT))
        * np.float32(REL_BUCKETS - REL_MAX_EXACT)).astype(np.int32)
    return np.where(n < REL_MAX_EXACT, n, np.minimum(log_b, REL_BUCKETS - 1)).astype(np.int32)


@functools.lru_cache(maxsize=None)
def _static_maps():
    ki = np.arange(TILE)[:, None]
    qi = np.arange(TILE)[None, :]
    dist = TILE * np.arange(N_TILES)[:, None, None] + qi[None] - ki[None]
    bk_t = _np_bucket(dist)
    mult = np.zeros(dist.shape, np.int64)
    for window, dil in DIL_PATTERNS:
        mult += ((dist % dil == 0) & (dist <= window)).astype(np.int64)
    base_a = np.where((dist >= 0) & (mult > 0), np.log2(np.maximum(mult, 1)), NEG)
    base_s = np.where(dist >= 0, 0.0, NEG)
    base_w = np.full((TAB_W_ROWS, TILE, TILE), NEG)
    base_w[:WIN_TILES] = np.where((dist >= 0) & (dist < WIN_SIZE), 0.0, NEG)[:WIN_TILES]
    base_t = np.concatenate([base_a, base_s, base_w]).astype(np.float32)
    ci = np.arange(N_CMP)[:, None]
    t = TILE * np.arange(N_TILES)[:, None, None] + qi[None]
    dist_c = t - (CMP_STRIDE * ci[None] + CMP_BLOCK - 1)
    bk_c = _np_bucket(dist_c)
    base_c = np.where((dist_c >= 0) & (ci[None] < N_CMP - 1), 0.0, NEG).astype(np.float32)
    cs = np.arange(N_CMP)[None, :] * CMP_STRIDE
    sj = np.arange(N_SLC)[:, None] * SLC_BLOCK
    ov_t = ((cs < sj + SLC_BLOCK) & (cs + CMP_BLOCK > sj) & (np.arange(N_CMP)[None, :] < N_CMP - 1))
    onehot = np.zeros((SEQ, HEAD_DIM), np.float32)
    onehot[np.arange(SEQ), np.arange(SEQ) // SLC_BLOCK] = 1.0
    return bk_t.astype(np.int32), base_t, bk_c.astype(np.int32), base_c, ov_t.astype(np.float32), onehot


def _bias_kernel(tab_ref, tabt_ref, bkt_ref, baset_ref, bkc_ref, basec_ref, out_ref, outc_ref, *,
                 present_t, present_c):
    h = pl.program_id(0)
    tv = [tab_ref[b, h] * LOG2E for b in range(REL_BUCKETS)]
    lanes = 128
    row = jnp.broadcast_to(tabt_ref[pl.ds(h, 1), :] * LOG2E, (TILE, lanes))

    def lookup(bk, present):
        if len(present) > GATHER_MIN_BUCKETS:
            parts = [jnp.take_along_axis(row[:bk.shape[0]], bk[:, c:c + lanes], axis=1)
                     for c in range(0, bk.shape[1], lanes)]
            return jnp.concatenate(parts, axis=1)
        val = jnp.full(bk.shape, tv[present[0]], F32)
        for b in present[1:]:
            val = jnp.where(bk == b, tv[b], val)
        return val

    for d in range(N_TILES):
        g = lookup(bkt_ref[d], present_t[d])
        out_ref[TAB_A + d] = g + baset_ref[TAB_A + d]
        out_ref[TAB_S + d] = g + baset_ref[TAB_S + d]
        if d < TAB_W_ROWS:
            out_ref[TAB_W + d] = g + baset_ref[TAB_W + d]
        outc_ref[d] = lookup(bkc_ref[d], present_c[d]) + basec_ref[d]


def _bias_tables(rel_table):
    bk_t, base_t, bk_c, base_c, _, _ = _static_maps()
    present_t = tuple(tuple(int(b) for b in np.unique(bk_t[d])) for d in range(N_TILES))
    present_c = tuple(tuple(int(b) for b in np.unique(bk_c[d])) for d in range(N_TILES))
    return pl.pallas_call(
        functools.partial(_bias_kernel, present_t=present_t, present_c=present_c),
        grid=(N_HEADS,),
        in_specs=[
            pl.BlockSpec(memory_space=pltpu.SMEM),
            pl.BlockSpec((N_HEADS, 128), lambda h: (0, 0)),
            pl.BlockSpec((N_TILES, TILE, TILE), lambda h: (0, 0, 0)),
            pl.BlockSpec((TAB_ROWS, TILE, TILE), lambda h: (0, 0, 0)),
            pl.BlockSpec((N_TILES, N_CMP, TILE), lambda h: (0, 0, 0)),
            pl.BlockSpec((N_TILES, N_CMP, TILE), lambda h: (0, 0, 0)),
        ],
        out_specs=[
            pl.BlockSpec((None, TAB_ROWS, TILE, TILE), lambda h: (h, 0, 0, 0)),
            pl.BlockSpec((None, N_TILES, N_CMP, TILE), lambda h: (h, 0, 0, 0)),
        ],
        out_shape=[
            jax.ShapeDtypeStruct((N_HEADS, TAB_ROWS, TILE, TILE), F32),
            jax.ShapeDtypeStruct((N_HEADS, N_TILES, N_CMP, TILE), F32),
        ],
        compiler_params=pltpu.CompilerParams(dimension_semantics=("arbitrary",),
                                             vmem_limit_bytes=VMEM_LIMIT),
        name="bias_tables",
    )(rel_table, jnp.pad(rel_table.T, ((0, 0), (0, 128 - REL_BUCKETS))),
      jnp.asarray(bk_t), jnp.asarray(base_t), jnp.asarray(bk_c), jnp.asarray(base_c))


def _silu(z):
    half = 0.5 * z
    return half + half * jnp.tanh(half)


TRANSPOSED_KINDS = ("t", "tq", "ts")


def _nm_kernel(*refs, tile_kinds, gate, w_transposed):
    refs = list(refs)
    x_ref, g_ref, w_ref = refs[:3]
    del refs[:3]
    wg_ref = refs.pop(0) if gate else None
    kinds_used = {k for kinds in tile_kinds for k in kinds}
    on_ref = refs.pop(0) if "n" in kinds_used else None
    ot_ref = refs.pop(0) if kinds_used & set(TRANSPOSED_KINDS) else None
    og_ref = refs.pop(0) if gate else None
    oc_ref = refs.pop(0) if "c" in kinds_used else None
    xn_ref = refs.pop(0)
    res_ref = refs.pop(0) if "c" in kinds_used else None
    j = pl.program_id(1)

    @pl.when(j == 0)
    def _():
        x = x_ref[...]
        ms = jnp.mean(x * x, axis=-1, keepdims=True)
        xn_ref[...] = (x * lax.rsqrt(ms + RMS_EPS) * g_ref[...]).astype(BF16)
        if gate:
            og_ref[...] = jnp.dot(xn_ref[...], wg_ref[...], preferred_element_type=F32)

    def product(kinds):
        pos = {"n": 0, "t": 0}
        per_group = MXU_COLS // HEAD_DIM
        for cc in range(len(kinds) // per_group):
            if w_transposed:
                w_cols = w_ref[cc * MXU_COLS:(cc + 1) * MXU_COLS, :].astype(BF16)
                res = lax.dot_general(xn_ref[...], w_cols, (((1,), (1,)), ((), ())), preferred_element_type=F32)
            else:
                w_cols = w_ref[:, cc * MXU_COLS:(cc + 1) * MXU_COLS].astype(BF16)
                res = jnp.dot(xn_ref[...], w_cols, preferred_element_type=F32)
            for u in range(per_group):
                c = per_group * cc + u
                piece = res[:, u * HEAD_DIM:(u + 1) * HEAD_DIM]
                kind = kinds[c]
                if kind == "n":
                    on_ref[pos["n"]] = piece.astype(BF16)
                    pos["n"] += 1
                elif kind == "c":
                    res_ref[c] = piece
                else:
                    if kind == "tq":
                        piece = piece * (ATT_SCALE * LOG2E)
                    elif kind == "ts":
                        piece = _silu(piece)
                    ot_ref[pos["t"]] = piece.T.astype(BF16)
                    pos["t"] += 1
        if "c" in kinds:
            rows = res_ref.shape[1] // CMP_STRIDE
            for c in range(len(kinds)):
                for i in range(CMP_STRIDE):
                    oc_ref[c, :, i * HEAD_DIM:(i + 1) * HEAD_DIM] = res_ref[
                        c, pl.ds(i, rows, stride=CMP_STRIDE), :].astype(BF16)

    for kinds in sorted(set(tile_kinds)):
        tiles = [t for t, k in enumerate(tile_kinds) if k == kinds]
        cond = functools.reduce(lambda a, b: a | b, [j == t for t in tiles])
        pl.when(cond)(functools.partial(product, kinds))


def _norm_matmul(x2d, gain, w, wg_bf=None, *, tile_kinds, w_transposed=False, tm=1024, tn=1024):
    m, d = x2d.shape
    slots = tn // HEAD_DIM
    batch = m // SEQ
    spb = SEQ // tm
    gate = wg_bf is not None
    assert all(len(k) == slots for k in tile_kinds)

    def family(match):
        counts = [sum(1 for k in kinds if match(k)) for kinds in tile_kinds]
        tiles = [t for t, c in enumerate(counts) if c]
        per_tile = counts[tiles[0]] if tiles else 0
        assert all(counts[t] == per_tile for t in tiles)

        def block(j):
            return jnp.minimum(sum(jnp.where(j > t, 1, 0) for t in tiles), len(tiles) - 1)
        return tiles, per_tile, block

    n_tiles, n_per, n_block = family(lambda k: k == "n")
    t_tiles, t_per, t_block = family(lambda k: k in TRANSPOSED_KINDS)
    c_tiles, c_per, _ = family(lambda k: k == "c")
    assert c_tiles in ([], [0]) and c_per in (0, slots)

    in_specs = [
        pl.BlockSpec((tm, d), lambda i, j: (i, 0)),
        pl.BlockSpec((1, d), lambda i, j: (0, 0)),
        pl.BlockSpec((tn, d), lambda i, j: (j, 0)) if w_transposed else pl.BlockSpec((d, tn), lambda i, j: (0, j)),
    ]
    args = [x2d, gain.reshape(1, d), w]
    out_shape, out_specs = [], []
    scratch = [pltpu.VMEM((tm, d), BF16)]
    if gate:
        in_specs.append(pl.BlockSpec((d, wg_bf.shape[1]), lambda i, j: (0, 0)))
        args.append(wg_bf)
    if n_tiles:
        out_shape.append(jax.ShapeDtypeStruct((batch, n_per * len(n_tiles), SEQ, HEAD_DIM), BF16))
        out_specs.append(pl.BlockSpec((None, n_per, tm, HEAD_DIM),
                                      lambda i, j: (i // spb, n_block(j), i % spb, 0)))
    if t_tiles:
        out_shape.append(jax.ShapeDtypeStruct((batch, t_per * len(t_tiles), HEAD_DIM, SEQ), BF16))
        out_specs.append(pl.BlockSpec((None, t_per, HEAD_DIM, tm),
                                      lambda i, j: (i // spb, t_block(j), 0, i % spb)))
    if gate:
        out_shape.append(jax.ShapeDtypeStruct((m, wg_bf.shape[1]), F32))
        out_specs.append(pl.BlockSpec((tm, wg_bf.shape[1]), lambda i, j: (i, 0)))
    if c_tiles:
        out_shape.append(jax.ShapeDtypeStruct((batch, slots, N_CMP, CMP_CHUNK), BF16))
        out_specs.append(pl.BlockSpec((None, slots, tm // CMP_STRIDE, CMP_CHUNK),
                                      lambda i, j: (i // spb, 0, i % spb, 0)))
        scratch.append(pltpu.VMEM((slots, tm, HEAD_DIM), F32))
    return pl.pallas_call(
        functools.partial(_nm_kernel, tile_kinds=tuple(tuple(k) for k in tile_kinds), gate=gate,
                          w_transposed=w_transposed),
        grid=(m // tm, len(tile_kinds)),
        in_specs=in_specs,
        out_specs=out_specs,
        out_shape=out_shape,
        scratch_shapes=scratch,
        compiler_params=pltpu.CompilerParams(dimension_semantics=("parallel", "arbitrary"),
                                             vmem_limit_bytes=VMEM_LIMIT),
        name="norm_matmul",
    )(*args)


def _op_kernel(a_ref, w_ref, r_ref, g_ref, o_ref, wbf_ref, *, parts):
    @pl.when(pl.program_id(0) == 0)
    def _():
        step = 256
        for r in range(w_ref.shape[0] // step):
            wbf_ref[r * step:(r + 1) * step, :] = w_ref[r * step:(r + 1) * step, :].astype(BF16)

    rows = a_ref.shape[0] // parts
    for r in range(parts):
        sl = slice(r * rows, (r + 1) * rows)
        y = jnp.dot(a_ref[sl, :], wbf_ref[...], preferred_element_type=F32)
        ms = jnp.mean(y * y, axis=-1, keepdims=True)
        o_ref[sl, :] = r_ref[sl, :] + y * lax.rsqrt(ms + RMS_EPS) * g_ref[...]


def _out_proj(a2d, w, resid2d, gain, *, tm=512, parts=2):
    m, k = a2d.shape
    n = w.shape[1]
    return pl.pallas_call(
        functools.partial(_op_kernel, parts=parts),
        grid=(m // tm,),
        in_specs=[
            pl.BlockSpec((tm, k), lambda i: (i, 0)),
            pl.BlockSpec((k, n), lambda i: (0, 0), pipeline_mode=pl.Buffered(1)),
            pl.BlockSpec((tm, n), lambda i: (i, 0)),
            pl.BlockSpec((1, n), lambda i: (0, 0)),
        ],
        out_specs=pl.BlockSpec((tm, n), lambda i: (i, 0)),
        out_shape=jax.ShapeDtypeStruct((m, n), F32),
        scratch_shapes=[pltpu.VMEM((k, n), BF16)],
        compiler_params=pltpu.CompilerParams(dimension_semantics=("arbitrary",),
                                             vmem_limit_bytes=VMEM_LIMIT),
        name="out_proj",
    )(a2d, w, resid2d, gain.reshape(1, n))


def _value_tiles(src_ref, dst_ref):
    for c in range(N_TILES):
        dst_ref[c, 0:HEAD_DIM, :] = src_ref[:, c * TILE:(c + 1) * TILE]
        dst_ref[c, HEAD_DIM:VT_ROWS, :] = jnp.ones((VT_ROWS - HEAD_DIM, TILE), BF16)


def _pair_tiles(n):
    return (n, N_TILES - 1 - n)


def _tile_rows(idx):
    return pl.ds(pl.multiple_of(idx * TILE, TILE), TILE)


def _normalized(acc):
    return acc[0:HEAD_DIM, :] / acc[HEAD_DIM:HEAD_DIM + 1, :]


class _PairSoftmax:
    def __init__(self, step_fn, heads, q_ref, k_fn, vt_fn, bias_fn, s_ref, acc_ref):
        self.step_fn, self.heads, self.q_ref, self.k_fn, self.vt_fn = step_fn, heads, q_ref, k_fn, vt_fn
        self.bias_fn, self.s_ref, self.acc_ref = bias_fn, s_ref, acc_ref

    def initial_max(self):
        neg = (jnp.full((1, TILE), NEG, F32),) * self.heads
        return neg, neg

    def scores(self, t, ms):
        slot, j, b = self.step_fn(t)
        first = slot == 0
        m0, m1 = list(ms[0]), list(ms[1])
        for h in range(self.heads):
            s = (jnp.dot(self.k_fn(h, j), self.q_ref[slot, h], preferred_element_type=F32)
                 + self.bias_fn(h, b))
            self.s_ref[h, t] = s
            cm = jnp.max(s, axis=0, keepdims=True)
            m0[h] = jnp.where(first, jnp.maximum(m0[h], cm), m0[h])
            m1[h] = jnp.where(first, m1[h], jnp.maximum(m1[h], cm))
        return tuple(m0), tuple(m1)

    def clear(self):
        for slot in range(2):
            for h in range(self.heads):
                self.acc_ref[slot, h] = jnp.zeros((VT_ROWS, TILE), F32)

    def weighted(self, t, ms):
        slot, j, _ = self.step_fn(t)
        first = slot == 0
        for h in range(self.heads):
            p = jnp.exp2((self.s_ref[h, t] - jnp.where(first, ms[0][h], ms[1][h])).astype(BF16))
            self.acc_ref[slot, h] += jnp.dot(self.vt_fn(h, j), p, preferred_element_type=F32)

    def run(self, steps):
        ms = lax.fori_loop(0, steps, self.scores, self.initial_max(), unroll=True)
        self.clear()

        def body(t, carry):
            self.weighted(t, ms)
            return carry

        lax.fori_loop(0, steps, body, 0, unroll=STEP_UNROLL)

    def run_unrolled(self, steps):
        ms = self.initial_max()
        for t in range(steps):
            ms = self.scores(t, ms)
        self.clear()
        for t in range(steps):
            self.weighted(t, ms)


def _causal_step(n):
    def step(t):
        first = t <= n
        return (jnp.where(first, 0, 1), jnp.where(first, t, t - n - 1), jnp.where(first, n - t, N_TILES - t))
    return step


def _window_step(n):
    def step(t):
        if isinstance(t, int):
            slot, d = divmod(t, WIN_TILES)
            idx = _pair_tiles(n)[slot]
        else:
            slot = jnp.where(t >= WIN_TILES, 1, 0)
            d = t - WIN_TILES * slot
            idx = jnp.where(slot == 0, n, N_TILES - 1 - n)
        j = idx - d
        return slot, jnp.maximum(j, 0), jnp.where(j >= 0, d, TAB_W_ROWS - 1)
    return step


def _attn_a_kernel(qa_ref, qb_ref, k_ref, v_ref, za_ref, zb_ref, r_ref, o_ref,
                   vt_ref, qt_ref, s_ref, acc_ref):
    n = pl.program_id(2)
    heads = HEADS_PER_STEP

    @pl.when(n == 0)
    def _():
        for h in range(heads):
            _value_tiles(v_ref.at[h], vt_ref.at[h])

    for slot, q_ref in enumerate((qa_ref, qb_ref)):
        for h in range(heads):
            qt_ref[slot, h] = q_ref[h]

    _PairSoftmax(_causal_step(n), heads, qt_ref, lambda h, j: k_ref[h, _tile_rows(j), :],
                 lambda h, j: vt_ref[h, j], lambda h, delta: r_ref[h, delta], s_ref, acc_ref).run(CAUSAL_STEPS)
    for slot, (z_ref, idx) in enumerate(zip((za_ref, zb_ref), _pair_tiles(n))):
        for h in range(heads):
            y = (_normalized(acc_ref[slot, h]) * z_ref[h].astype(F32)).T
            o_ref[_tile_rows(idx), h * HEAD_DIM:(h + 1) * HEAD_DIM] = y.astype(BF16)


def _resident(block_shape, index_map):
    return pl.BlockSpec(block_shape, index_map, pipeline_mode=pl.Buffered(1))


def _attn_a(k_nat, qvz_t, table):
    batch = k_nat.shape[0]
    heads = HEADS_PER_STEP
    hgroups = N_HEADS // heads

    def tile_slot(s, second):
        def index(hg, b, n):
            return (b, s * hgroups + hg, 0, N_TILES - 1 - n if second else n)
        return pl.BlockSpec((None, heads, HEAD_DIM, TILE), index)

    return pl.pallas_call(
        _attn_a_kernel,
        grid=(hgroups, batch, N_PAIRS),
        in_specs=[tile_slot(0, False), tile_slot(0, True),
                  pl.BlockSpec((None, heads, SEQ, HEAD_DIM), lambda hg, b, n: (b, hg, 0, 0)),
                  pl.BlockSpec((None, heads, HEAD_DIM, SEQ), lambda hg, b, n: (b, hgroups + hg, 0, 0)),
                  tile_slot(2, False), tile_slot(2, True),
                  pl.BlockSpec((heads, N_TILES, TILE, TILE), lambda hg, b, n: (hg, TAB_A // N_TILES, 0, 0))],
        out_specs=pl.BlockSpec((None, SEQ, heads * HEAD_DIM), lambda hg, b, n: (b, 0, hg)),
        out_shape=jax.ShapeDtypeStruct((batch, SEQ, ATT_WIDTH), BF16),
        scratch_shapes=[
            pltpu.VMEM((heads, N_TILES, VT_ROWS, TILE), BF16),
            pltpu.VMEM((2, heads, HEAD_DIM, TILE), BF16),
            pltpu.VMEM((heads, CAUSAL_STEPS, TILE, TILE), F32),
            pltpu.VMEM((2, heads, VT_ROWS, TILE), F32),
        ],
        compiler_params=pltpu.CompilerParams(dimension_semantics=("parallel", "parallel", "arbitrary"),
                                             vmem_limit_bytes=VMEM_LIMIT),
        name="dilated_attn",
    )(qvz_t, qvz_t, k_nat, qvz_t, qvz_t, qvz_t, table)


def _cmp_kernel(c_ref, pos_ref, w1_ref, w2_ref, o_ref):
    for g in range(NSA_KV_GROUPS):
        c = c_ref[g].astype(F32)
        x_lo = (c + pos_ref[0]).astype(BF16)
        x_hi = (c + pos_ref[1]).astype(BF16)
        a = jnp.dot(x_lo, w1_ref[0:CMP_CHUNK, :], preferred_element_type=F32)
        bm = jnp.dot(x_hi, w1_ref[CMP_CHUNK:2 * CMP_CHUNK, :], preferred_element_type=F32)
        hid = jax.nn.gelu(a + pltpu.roll(bm, N_CMP - 1, 0))
        res = jnp.dot(hid.astype(BF16), w2_ref[...], preferred_element_type=F32)
        o_ref[g] = jnp.where(pl.program_id(0) == 1, res.T, res).astype(BF16)


def _compress(chunks, pos, w1_bf, w2_bf):
    batch = chunks.shape[0]
    return pl.pallas_call(
        _cmp_kernel,
        grid=(2, batch),
        in_specs=[
            pl.BlockSpec((None, NSA_KV_GROUPS, N_CMP, CMP_CHUNK), lambda t, b: (b, t, 0, 0)),
            pl.BlockSpec((None, 2, 1, CMP_CHUNK), lambda t, b: (t, 0, 0, 0)),
            pl.BlockSpec((None, 2 * CMP_CHUNK, CMP_HIDDEN), lambda t, b: (t, 0, 0)),
            pl.BlockSpec((None, CMP_HIDDEN, HEAD_DIM), lambda t, b: (t, 0, 0)),
        ],
        out_specs=pl.BlockSpec((None, None, NSA_KV_GROUPS, N_CMP, HEAD_DIM), lambda t, b: (t, b, 0, 0, 0)),
        out_shape=jax.ShapeDtypeStruct((2, batch, NSA_KV_GROUPS, N_CMP, HEAD_DIM), BF16),
        compiler_params=pltpu.CompilerParams(dimension_semantics=("arbitrary", "arbitrary"),
                                             vmem_limit_bytes=VMEM_LIMIT),
        name="compress_kv",
    )(chunks, pos, w1_bf, w2_bf)


def _nsa_kernel(qa_ref, qb_ref, zca_ref, zcb_ref, zsa_ref, zsb_ref, zwa_ref, zwb_ref, gla_ref, glb_ref,
                kc_ref, vct_ref, ks_ref, vs_ref, kw_ref, vw_ref, rs_ref, rw_ref, rca_ref, rcb_ref,
                ovt_ref, onehot_ref, o_ref,
                kaug_ref, vst_ref, vwt_ref, qa_sc, qw_sc, gates_ref, ycmp_ref, s_ref, accs_ref, accw_ref):
    n = pl.program_id(2)
    heads = NSA_HEADS_PER_GROUP
    tiles = _pair_tiles(n)

    @pl.when(n == 0)
    def _():
        kaug_ref[:, 0:HEAD_DIM] = ks_ref[...]
        kaug_ref[:, HEAD_DIM:2 * HEAD_DIM] = onehot_ref[...]
        _value_tiles(vs_ref, vst_ref)
        _value_tiles(vw_ref, vwt_ref)

    kc = kc_ref[...]
    vct = vct_ref[...]
    for slot, (q_ref, rc_ref, zc_ref, gl_ref) in enumerate(
            ((qa_ref, rca_ref, zca_ref, gla_ref), (qb_ref, rcb_ref, zcb_ref, glb_ref))):
        psum = jnp.zeros((N_CMP, TILE), F32)
        gates_ref[slot] = jax.nn.sigmoid(gl_ref[...].T)
        for h in range(heads):
            qt = q_ref[h]
            qw_sc[slot, h] = qt
            qa_sc[slot, h, 0:HEAD_DIM, :] = qt
            bias = rc_ref[h]
            s = jnp.dot(kc, qt, preferred_element_type=F32) + bias
            m = jnp.max(s, axis=0, keepdims=True)
            p = jnp.where(bias > 0.5 * NEG, jnp.exp2(s - m), 0.0)
            p = p / jnp.maximum(jnp.sum(p, axis=0, keepdims=True), 1e-30)
            psum = psum + p
            o_cmp = jnp.dot(vct, p.astype(BF16), preferred_element_type=F32)
            ycmp_ref[slot, h] = gates_ref[slot, h:h + 1, :] * o_cmp * zc_ref[h].astype(F32)

        imp = jnp.dot(ovt_ref[...], psum, precision=lax.Precision.HIGHEST, preferred_element_type=F32)
        blk = lax.broadcasted_iota(jnp.int32, (N_SLC, TILE), 0)
        tq = tiles[slot] * TILE + lax.broadcasted_iota(jnp.int32, (N_SLC, TILE), 1)
        cur = lax.shift_right_logical(tq, int(math.log2(SLC_BLOCK)))
        forced = (blk == 0) | (blk == cur) | (blk == cur - 1)
        score = jnp.where(forced, FORCE_SCORE, jnp.where(blk > cur, -FORCE_SCORE, imp))
        group = 8
        rows = [score[r:r + group, :] for r in range(0, N_SLC, group)]
        row_idx = lax.broadcasted_iota(jnp.int32, (group, TILE), 0)
        cnts = [jnp.zeros((group, TILE), jnp.int32) for _ in rows]
        for jj in range(N_SLC):
            sj = score[jj:jj + 1, :]
            for r, sr in enumerate(rows):
                lo = r * group
                if lo > jj:
                    beats = jnp.where(sj >= sr, 1, 0)
                elif lo + group - 1 <= jj:
                    beats = jnp.where(sj > sr, 1, 0)
                else:
                    beats = jnp.where(row_idx + lo > jj, jnp.where(sj >= sr, 1, 0), jnp.where(sj > sr, 1, 0))
                cnts[r] = cnts[r] + beats
        cnt = jnp.concatenate(cnts, axis=0)
        mask = jnp.where(cnt < SLC_TOP_N, 0.0, NEG)
        mask = jnp.concatenate([mask, jnp.zeros((HEAD_DIM - N_SLC, TILE), F32)], axis=0).astype(BF16)
        for h in range(heads):
            qa_sc[slot, h, HEAD_DIM:2 * HEAD_DIM, :] = mask

    _PairSoftmax(_causal_step(n), heads, qa_sc, lambda h, j: kaug_ref[_tile_rows(j), :],
                 lambda h, j: vst_ref[j], lambda h, delta: rs_ref[h, delta], s_ref, accs_ref).run(CAUSAL_STEPS)
    _PairSoftmax(_window_step(n), heads, qw_sc, lambda h, j: kw_ref[_tile_rows(j), :],
                 lambda h, j: vwt_ref[j], lambda h, b: rw_ref[h, b], s_ref, accw_ref).run_unrolled(2 * WIN_TILES)

    for slot, (zs_ref, zw_ref) in enumerate(((zsa_ref, zwa_ref), (zsb_ref, zwb_ref))):
        for h in range(heads):
            y = (ycmp_ref[slot, h]
                 + gates_ref[slot, heads + h:heads + h + 1, :] * _normalized(accs_ref[slot, h])
                 * zs_ref[h].astype(F32)
                 + gates_ref[slot, 2 * heads + h:2 * heads + h + 1, :] * _normalized(accw_ref[slot, h])
                 * zw_ref[h].astype(F32))
            o_ref[_tile_rows(tiles[slot]), h * HEAD_DIM:(h + 1) * HEAD_DIM] = y.T.astype(BF16)


def _nsa_attn(qz_t, gate_logits, kv_nat, kv_t, cmp_kv, table, table_c):
    batch = qz_t.shape[0]
    heads = NSA_HEADS_PER_GROUP
    groups = NSA_KV_GROUPS
    _, _, _, _, ov_t, onehot = _static_maps()

    def tile_of(n, second):
        return N_TILES - 1 - n if second else n

    def qz_specs(slot_group):
        return [pl.BlockSpec((None, heads, HEAD_DIM, TILE),
                             lambda g, b, n, second=second: (b, slot_group * groups + g, 0, tile_of(n, second)))
                for second in (False, True)]

    def key_spec(branch):
        return pl.BlockSpec((None, None, SEQ, HEAD_DIM), lambda g, b, n: (b, branch * groups + g, 0, 0))

    def value_spec(branch):
        return pl.BlockSpec((None, None, HEAD_DIM, SEQ), lambda g, b, n: (b, branch * groups + g, 0, 0))

    def cmp_spec(t):
        return pl.BlockSpec((None, None, None, N_CMP, HEAD_DIM), lambda g, b, n: (t, b, g, 0, 0))

    gl_specs = [pl.BlockSpec((None, TILE, HEAD_DIM), lambda g, b, n, second=second: (b, tile_of(n, second), g))
                for second in (False, True)]
    rc_specs = [pl.BlockSpec((heads, None, N_CMP, TILE), lambda g, b, n, second=second: (g, tile_of(n, second), 0, 0))
                for second in (False, True)]

    return pl.pallas_call(
        _nsa_kernel,
        grid=(groups, batch, N_PAIRS),
        in_specs=[
            *qz_specs(0), *qz_specs(1), *qz_specs(2), *qz_specs(3), *gl_specs,
            cmp_spec(0), cmp_spec(1), key_spec(0), value_spec(0), key_spec(1), value_spec(1),
            pl.BlockSpec((heads, N_TILES, TILE, TILE), lambda g, b, n: (g, TAB_S // N_TILES, 0, 0)),
            pl.BlockSpec((heads, TAB_W_ROWS, TILE, TILE), lambda g, b, n: (g, TAB_W // TAB_W_ROWS, 0, 0)),
            *rc_specs,
            _resident((N_SLC, N_CMP), lambda g, b, n: (0, 0)),
            _resident((SEQ, HEAD_DIM), lambda g, b, n: (0, 0)),
        ],
        out_specs=pl.BlockSpec((None, SEQ, heads * HEAD_DIM), lambda g, b, n: (b, 0, g)),
        out_shape=jax.ShapeDtypeStruct((batch, SEQ, ATT_WIDTH), BF16),
        scratch_shapes=[
            pltpu.VMEM((SEQ, 2 * HEAD_DIM), BF16),
            pltpu.VMEM((N_TILES, VT_ROWS, TILE), BF16),
            pltpu.VMEM((N_TILES, VT_ROWS, TILE), BF16),
            pltpu.VMEM((2, heads, 2 * HEAD_DIM, TILE), BF16),
            pltpu.VMEM((2, heads, HEAD_DIM, TILE), BF16),
            pltpu.VMEM((2, HEAD_DIM, TILE), F32),
            pltpu.VMEM((2, heads, HEAD_DIM, TILE), F32),
            pltpu.VMEM((heads, CAUSAL_STEPS, TILE, TILE), F32),
            pltpu.VMEM((2, heads, VT_ROWS, TILE), F32),
            pltpu.VMEM((2, heads, VT_ROWS, TILE), F32),
        ],
        compiler_params=pltpu.CompilerParams(dimension_semantics=("parallel", "parallel", "arbitrary"),
                                             vmem_limit_bytes=VMEM_LIMIT),
        name="nsa_attn",
    )(qz_t, qz_t, qz_t, qz_t, qz_t, qz_t, qz_t, qz_t, gate_logits, gate_logits, cmp_kv, cmp_kv,
      kv_nat, kv_t, kv_nat, kv_t, table, table, table_c, table_c,
      jnp.asarray(ov_t), jnp.asarray(onehot, dtype=BF16))


def kernel(x, norm_pre, norm_post, rel_table, w_in_a, w_out_a, kv_norm, w_kv, cmp_pos_k, cmp_pos_v,
           cmp_w1_k, cmp_w2_k, cmp_w1_v, cmp_w2_v, w_in_b, w_out_b):
    batch = x.shape[0]
    m = batch * SEQ
    heads = NSA_HEADS_PER_GROUP
    groups = NSA_KV_GROUPS
    x2d = x.reshape(m, D_MODEL)
    table, table_c = _bias_tables(rel_table)

    tile_slots = 8
    tiles_per_part = ATT_WIDTH // (tile_slots * HEAD_DIM)

    kinds_a = [(kind,) * tile_slots for kind in ("tq", "n", "t", "ts") for _ in range(tiles_per_part)]
    k_nat, qvz_t = _norm_matmul(x2d, norm_pre[0], w_in_a[0], tile_kinds=kinds_a)
    att_a = _attn_a(k_nat, qvz_t, table)
    h1 = _out_proj(att_a.reshape(m, ATT_WIDTH), w_out_a[0], x2d, norm_post[0])

    kinds_kv = [("c",) * tile_slots] + [("n",) * groups + ("t",) * groups] * 2
    kv_nat, kv_t, cmp_chunks = _norm_matmul(h1, kv_norm, w_kv, tile_kinds=kinds_kv)
    pos = jnp.stack([cmp_pos_k, cmp_pos_v]).reshape(2, 2, 1, CMP_CHUNK)
    cmp_kv = _compress(cmp_chunks, pos, jnp.stack([cmp_w1_k, cmp_w1_v]).astype(BF16),
                       jnp.stack([cmp_w2_k, cmp_w2_v]).astype(BF16))

    w_bt = w_in_b[0].T
    n_main = (1 + NSA_BRANCHES) * ATT_WIDTH
    wg = w_bt[n_main:].reshape(NSA_BRANCHES, groups, heads, D_MODEL).transpose(1, 0, 2, 3)
    wg = jnp.pad(wg.reshape(groups, NSA_BRANCHES * heads, D_MODEL),
                 ((0, 0), (0, HEAD_DIM - NSA_BRANCHES * heads), (0, 0))).reshape(groups * HEAD_DIM, D_MODEL)
    kinds_b = [(kind,) * tile_slots for kind in ("tq",) + ("ts",) * NSA_BRANCHES for _ in range(tiles_per_part)]
    qz_t, gate_logits = _norm_matmul(h1, norm_pre[1], w_bt, wg.T.astype(BF16), tile_kinds=kinds_b,
                                     w_transposed=True)
    att_b = _nsa_attn(qz_t, gate_logits.reshape(batch, SEQ, groups * HEAD_DIM), kv_nat, kv_t, cmp_kv,
                      table, table_c)
    out = _out_proj(att_b.reshape(m, ATT_WIDTH), w_out_b[0], h1, norm_post[1])
    return out.reshape(batch, SEQ, D_MODEL)
```

```python
import functools
import math

import numpy as np
import jax
import jax.numpy as jnp
from jax import lax
from jax.experimental import pallas as pl
from jax.experimental.pallas import tpu as pltpu

F32 = jnp.float32
BF16 = jnp.bfloat16

D_MODEL = 2048
SEQ = 2048
N_HEADS = 16
HEAD_DIM = 128
ATT_WIDTH = N_HEADS * HEAD_DIM
ATT_SCALE = HEAD_DIM ** -0.5
DIL_PATTERNS = ((128, 1), (512, 4), (2048, 16))
REL_BUCKETS = 32
REL_MAX_EXACT = 16
REL_MAX_DISTANCE = 2048
NSA_KV_GROUPS = 4
NSA_HEADS_PER_GROUP = 4
NSA_BRANCHES = 3
CMP_BLOCK = 32
CMP_STRIDE = 16
CMP_HIDDEN = 256
SLC_BLOCK = 64
SLC_TOP_N = 16
WIN_SIZE = 512
RMS_EPS = 1e-6
NEG = -1e30
FORCE_SCORE = 1e9
LOG2E = math.log2(math.e)

TILE = 256
N_TILES = SEQ // TILE
N_PAIRS = N_TILES // 2
CAUSAL_STEPS = N_TILES + 1
STEP_UNROLL = 3
N_CMP = SEQ // CMP_STRIDE
CMP_CHUNK = CMP_STRIDE * HEAD_DIM
N_SLC = SEQ // SLC_BLOCK
WIN_TILES = -(-(WIN_SIZE - 1) // TILE) + 1
HEADS_PER_STEP = 4
VT_ROWS = HEAD_DIM + 16
MXU_COLS = 512

TAB_A, TAB_S, TAB_W = 0, N_TILES, 2 * N_TILES
TAB_W_ROWS = 4
TAB_ROWS = 2 * N_TILES + TAB_W_ROWS
GATHER_MIN_BUCKETS = 6
V7X_VMEM_BYTES = 64 * 1024 * 1024
VMEM_LIMIT = V7X_VMEM_BYTES // 8 * 7
LANES = 128


def _np_bucket(dist):
    n = np.maximum(dist, 0)
    nf = np.maximum(n, 1).astype(np.float32)
    log_b = REL_MAX_EXACT + (
        np.log(nf / np.float32(REL_MAX_EXACT)) / np.float32(math.log(REL_MAX_DISTANCE / REL_MAX_EXACT))
        * np.float32(REL_BUCKETS - REL_MAX_EXACT)).astype(np.int32)
    return np.where(n < REL_MAX_EXACT, n, np.minimum(log_b, REL_BUCKETS - 1)).astype(np.int32)


@functools.lru_cache(maxsize=None)
def _static_maps():
    ki = np.arange(TILE)[:, None]
    qi = np.arange(TILE)[None, :]
    dist = TILE * np.arange(N_TILES)[:, None, None] + qi[None] - ki[None]
    bk_t = _np_bucket(dist)
    mult = np.zeros(dist.shape, np.int64)
    for window, dil in DIL_PATTERNS:
        mult += ((dist % dil == 0) & (dist <= window)).astype(np.int64)
    base_a = np.where((dist >= 0) & (mult > 0), np.log2(np.maximum(mult, 1)), NEG)
    base_s = np.where(dist >= 0, 0.0, NEG)
    base_w = np.full((TAB_W_ROWS, TILE, TILE), NEG)
    base_w[:WIN_TILES] = np.where((dist >= 0) & (dist < WIN_SIZE), 0.0, NEG)[:WIN_TILES]
    base_t = np.concatenate([base_a, base_s, base_w]).astype(np.float32)
    ci = np.arange(N_CMP)[:, None]
    t = TILE * np.arange(N_TILES)[:, None, None] + qi[None]
    dist_c = t - (CMP_STRIDE * ci[None] + CMP_BLOCK - 1)
    bk_c = _np_bucket(dist_c)
    base_c = np.where((dist_c >= 0) & (ci[None] < N_CMP - 1), 0.0, NEG).astype(np.float32)
    cs = np.arange(N_CMP)[None, :] * CMP_STRIDE
    sj = np.arange(N_SLC)[:, None] * SLC_BLOCK
    ov_t = ((cs < sj + SLC_BLOCK) & (cs + CMP_BLOCK > sj) & (np.arange(N_CMP)[None, :] < N_CMP - 1))
    onehot = np.zeros((SEQ, HEAD_DIM), np.float32)
    onehot[np.arange(SEQ), np.arange(SEQ) // SLC_BLOCK] = 1.0
    return bk_t.astype(np.int32), base_t, bk_c.astype(np.int32), base_c, ov_t.astype(np.float32), onehot


def _bias_kernel(tab_ref, tabt_ref, bkt_ref, baset_ref, bkc_ref, basec_ref, out_ref, outc_ref, *,
                 present_t, present_c):
    h = pl.program_id(0)
    tv = [tab_ref[b, h] * LOG2E for b in range(REL_BUCKETS)]
    lanes = LANES
    row = jnp.broadcast_to(tabt_ref[pl.ds(h, 1), :] * LOG2E, (TILE, lanes))

    def lookup(bk, present):
        if len(present) > GATHER_MIN_BUCKETS:
            parts = [jnp.take_along_axis(row[:bk.shape[0]], bk[:, c:c + lanes], axis=1)
                     for c in range(0, bk.shape[1], lanes)]
            return jnp.concatenate(parts, axis=1)
        val = jnp.full(bk.shape, tv[present[0]], F32)
        for b in present[1:]:
            val = jnp.where(bk == b, tv[b], val)
        return val

    for d in range(N_TILES):
        g = lookup(bkt_ref[d], present_t[d])
        out_ref[TAB_A + d] = g + baset_ref[TAB_A + d]
        out_ref[TAB_S + d] = g + baset_ref[TAB_S + d]
        if d < TAB_W_ROWS:
            out_ref[TAB_W + d] = g + baset_ref[TAB_W + d]
        outc_ref[d] = lookup(bkc_ref[d], present_c[d]) + basec_ref[d]


def _bias_tables(rel_table):
    bk_t, base_t, bk_c, base_c, _, _ = _static_maps()
    present_t = tuple(tuple(int(b) for b in np.unique(bk_t[d])) for d in range(N_TILES))
    present_c = tuple(tuple(int(b) for b in np.unique(bk_c[d])) for d in range(N_TILES))
    return pl.pallas_call(
        functools.partial(_bias_kernel, present_t=present_t, present_c=present_c),
        grid=(N_HEADS,),
        in_specs=[
            pl.BlockSpec(memory_space=pltpu.SMEM),
            pl.BlockSpec((N_HEADS, LANES), lambda h: (0, 0)),
            pl.BlockSpec((N_TILES, TILE, TILE), lambda h: (0, 0, 0)),
            pl.BlockSpec((TAB_ROWS, TILE, TILE), lambda h: (0, 0, 0)),
            pl.BlockSpec((N_TILES, N_CMP, TILE), lambda h: (0, 0, 0)),
            pl.BlockSpec((N_TILES, N_CMP, TILE), lambda h: (0, 0, 0)),
        ],
        out_specs=[
            pl.BlockSpec((None, TAB_ROWS, TILE, TILE), lambda h: (h, 0, 0, 0)),
            pl.BlockSpec((None, N_TILES, N_CMP, TILE), lambda h: (h, 0, 0, 0)),
        ],
        out_shape=[
            jax.ShapeDtypeStruct((N_HEADS, TAB_ROWS, TILE, TILE), F32),
            jax.ShapeDtypeStruct((N_HEADS, N_TILES, N_CMP, TILE), F32),
        ],
        compiler_params=pltpu.CompilerParams(dimension_semantics=("arbitrary",),
                                             vmem_limit_bytes=VMEM_LIMIT),
        name="bias_tables",
    )(rel_table, jnp.pad(rel_table.T, ((0, 0), (0, LANES - REL_BUCKETS))),
      jnp.asarray(bk_t), jnp.asarray(base_t), jnp.asarray(bk_c), jnp.asarray(base_c))


def _silu(z):
    half = 0.5 * z
    return half + half * jnp.tanh(half)


TRANSPOSED_KINDS = ("t", "tq", "ts")


def _nm_kernel(*refs, tile_kinds, gate, w_transposed):
    refs = list(refs)
    x_ref, g_ref, w_ref = refs[:3]
    del refs[:3]
    wg_ref = refs.pop(0) if gate else None
    kinds_used = {k for kinds in tile_kinds for k in kinds}
    on_ref = refs.pop(0) if "n" in kinds_used else None
    ot_ref = refs.pop(0) if kinds_used & set(TRANSPOSED_KINDS) else None
    og_ref = refs.pop(0) if gate else None
    oc_ref = refs.pop(0) if "c" in kinds_used else None
    xn_ref = refs.pop(0)
    res_ref = refs.pop(0) if "c" in kinds_used else None
    j = pl.program_id(1)

    @pl.when(j == 0)
    def _():
        x = x_ref[...]
        ms = jnp.mean(x * x, axis=-1, keepdims=True)
        xn_ref[...] = (x * lax.rsqrt(ms + RMS_EPS) * g_ref[...]).astype(BF16)
        if gate:
            og_ref[...] = jnp.dot(xn_ref[...], wg_ref[...], preferred_element_type=F32)

    def product(kinds):
        pos = {"n": 0, "t": 0}
        per_group = MXU_COLS // HEAD_DIM
        for cc in range(len(kinds) // per_group):
            if w_transposed:
                w_cols = w_ref[cc * MXU_COLS:(cc + 1) * MXU_COLS, :].astype(BF16)
                res = lax.dot_general(xn_ref[...], w_cols, (((1,), (1,)), ((), ())), preferred_element_type=F32)
            else:
                w_cols = w_ref[:, cc * MXU_COLS:(cc + 1) * MXU_COLS].astype(BF16)
                res = jnp.dot(xn_ref[...], w_cols, preferred_element_type=F32)
            for u in range(per_group):
                c = per_group * cc + u
                piece = res[:, u * HEAD_DIM:(u + 1) * HEAD_DIM]
                kind = kinds[c]
                if kind == "n":
                    on_ref[pos["n"]] = piece.astype(BF16)
                    pos["n"] += 1
                elif kind == "c":
                    res_ref[c] = piece
                else:
                    if kind == "tq":
                        piece = piece * (ATT_SCALE * LOG2E)
                    elif kind == "ts":
                        piece = _silu(piece)
                    ot_ref[pos["t"]] = piece.T.astype(BF16)
                    pos["t"] += 1
        if "c" in kinds:
            rows = res_ref.shape[1] // CMP_STRIDE
            for c in range(len(kinds)):
                for i in range(CMP_STRIDE):
                    oc_ref[c, :, i * HEAD_DIM:(i + 1) * HEAD_DIM] = res_ref[
                        c, pl.ds(i, rows, stride=CMP_STRIDE), :].astype(BF16)

    for kinds in sorted(set(tile_kinds)):
        tiles = [t for t, k in enumerate(tile_kinds) if k == kinds]
        cond = functools.reduce(lambda a, b: a | b, [j == t for t in tiles])
        pl.when(cond)(functools.partial(product, kinds))


def _norm_matmul(x2d, gain, w, wg_bf=None, *, tile_kinds, w_transposed=False, tm=1024, tn=1024):
    m, d = x2d.shape
    slots = tn // HEAD_DIM
    batch = m // SEQ
    spb = SEQ // tm
    gate = wg_bf is not None
    assert all(len(k) == slots for k in tile_kinds)

    def family(match):
        counts = [sum(1 for k in kinds if match(k)) for kinds in tile_kinds]
        tiles = [t for t, c in enumerate(counts) if c]
        per_tile = counts[tiles[0]] if tiles else 0
        assert all(counts[t] == per_tile for t in tiles)

        def block(j):
            return jnp.minimum(sum(jnp.where(j > t, 1, 0) for t in tiles), len(tiles) - 1)
        return tiles, per_tile, block

    n_tiles, n_per, n_block = family(lambda k: k == "n")
    t_tiles, t_per, t_block = family(lambda k: k in TRANSPOSED_KINDS)
    c_tiles, c_per, _ = family(lambda k: k == "c")
    assert c_tiles in ([], [0]) and c_per in (0, slots)

    in_specs = [
        pl.BlockSpec((tm, d), lambda i, j: (i, 0)),
        pl.BlockSpec((1, d), lambda i, j: (0, 0)),
        pl.BlockSpec((tn, d), lambda i, j: (j, 0)) if w_transposed else pl.BlockSpec((d, tn), lambda i, j: (0, j)),
    ]
    args = [x2d, gain.reshape(1, d), w]
    out_shape, out_specs = [], []
    scratch = [pltpu.VMEM((tm, d), BF16)]
    if gate:
        in_specs.append(pl.BlockSpec((d, wg_bf.shape[1]), lambda i, j: (0, 0)))
        args.append(wg_bf)
    if n_tiles:
        out_shape.append(jax.ShapeDtypeStruct((batch, n_per * len(n_tiles), SEQ, HEAD_DIM), BF16))
        out_specs.append(pl.BlockSpec((None, n_per, tm, HEAD_DIM),
                                      lambda i, j: (i // spb, n_block(j), i % spb, 0)))
    if t_tiles:
        out_shape.append(jax.ShapeDtypeStruct((batch, t_per * len(t_tiles), HEAD_DIM, SEQ), BF16))
        out_specs.append(pl.BlockSpec((None, t_per, HEAD_DIM, tm),
                                      lambda i, j: (i // spb, t_block(j), 0, i % spb)))
    if gate:
        out_shape.append(jax.ShapeDtypeStruct((m, wg_bf.shape[1]), F32))
        out_specs.append(pl.BlockSpec((tm, wg_bf.shape[1]), lambda i, j: (i, 0)))
    if c_tiles:
        out_shape.append(jax.ShapeDtypeStruct((batch, slots, N_CMP, CMP_CHUNK), BF16))
        out_specs.append(pl.BlockSpec((None, slots, tm // CMP_STRIDE, CMP_CHUNK),
                                      lambda i, j: (i // spb, 0, i % spb, 0)))
        scratch.append(pltpu.VMEM((slots, tm, HEAD_DIM), F32))
    return pl.pallas_call(
        functools.partial(_nm_kernel, tile_kinds=tuple(tuple(k) for k in tile_kinds), gate=gate,
                          w_transposed=w_transposed),
        grid=(m // tm, len(tile_kinds)),
        in_specs=in_specs,
        out_specs=out_specs,
        out_shape=out_shape,
        scratch_shapes=scratch,
        compiler_params=pltpu.CompilerParams(dimension_semantics=("parallel", "arbitrary"),
                                             vmem_limit_bytes=VMEM_LIMIT),
        name="norm_matmul",
    )(*args)


def _op_kernel(a_ref, w_ref, r_ref, g_ref, o_ref, wbf_ref, *, parts):
    @pl.when(pl.program_id(0) == 0)
    def _():
        step = 256
        for r in range(w_ref.shape[0] // step):
            wbf_ref[r * step:(r + 1) * step, :] = w_ref[r * step:(r + 1) * step, :].astype(BF16)

    rows = a_ref.shape[0] // parts
    for r in range(parts):
        sl = slice(r * rows, (r + 1) * rows)
        y = jnp.dot(a_ref[sl, :], wbf_ref[...], preferred_element_type=F32)
        ms = jnp.mean(y * y, axis=-1, keepdims=True)
        o_ref[sl, :] = r_ref[sl, :] + y * lax.rsqrt(ms + RMS_EPS) * g_ref[...]


def _out_proj(a2d, w, resid2d, gain, *, tm=512, parts=2):
    m, k = a2d.shape
    n = w.shape[1]
    return pl.pallas_call(
        functools.partial(_op_kernel, parts=parts),
        grid=(m // tm,),
        in_specs=[
            pl.BlockSpec((tm, k), lambda i: (i, 0)),
            pl.BlockSpec((k, n), lambda i: (0, 0), pipeline_mode=pl.Buffered(1)),
            pl.BlockSpec((tm, n), lambda i: (i, 0)),
            pl.BlockSpec((1, n), lambda i: (0, 0)),
        ],
        out_specs=pl.BlockSpec((tm, n), lambda i: (i, 0)),
        out_shape=jax.ShapeDtypeStruct((m, n), F32),
        scratch_shapes=[pltpu.VMEM((k, n), BF16)],
        compiler_params=pltpu.CompilerParams(dimension_semantics=("arbitrary",),
                                             vmem_limit_bytes=VMEM_LIMIT),
        name="out_proj",
    )(a2d, w, resid2d, gain.reshape(1, n))


def _value_tiles(src_ref, dst_ref):
    for c in range(N_TILES):
        dst_ref[c, 0:HEAD_DIM, :] = src_ref[:, c * TILE:(c + 1) * TILE]
        dst_ref[c, HEAD_DIM:VT_ROWS, :] = jnp.ones((VT_ROWS - HEAD_DIM, TILE), BF16)


def _pair_tiles(n):
    return (n, N_TILES - 1 - n)


def _tile_rows(idx):
    return pl.ds(pl.multiple_of(idx * TILE, TILE), TILE)


def _normalized(acc):
    return acc[0:HEAD_DIM, :] / acc[HEAD_DIM:HEAD_DIM + 1, :]


class _PairSoftmax:
    def __init__(self, step_fn, heads, q_ref, k_fn, vt_fn, bias_fn, s_ref, acc_ref):
        self.step_fn, self.heads, self.q_ref, self.k_fn, self.vt_fn = step_fn, heads, q_ref, k_fn, vt_fn
        self.bias_fn, self.s_ref, self.acc_ref = bias_fn, s_ref, acc_ref

    def initial_max(self):
        neg = (jnp.full((1, TILE), NEG, F32),) * self.heads
        return neg, neg

    def scores(self, t, ms):
        slot, j, b = self.step_fn(t)
        first = slot == 0
        m0, m1 = list(ms[0]), list(ms[1])
        for h in range(self.heads):
            s = (jnp.dot(self.k_fn(h, j), self.q_ref[slot, h], preferred_element_type=F32)
                 + self.bias_fn(h, b))
            self.s_ref[h, t] = s
            cm = jnp.max(s, axis=0, keepdims=True)
            m0[h] = jnp.where(first, jnp.maximum(m0[h], cm), m0[h])
            m1[h] = jnp.where(first, m1[h], jnp.maximum(m1[h], cm))
        return tuple(m0), tuple(m1)

    def clear(self):
        for slot in range(2):
            for h in range(self.heads):
                self.acc_ref[slot, h] = jnp.zeros((VT_ROWS, TILE), F32)

    def weighted(self, t, ms):
        slot, j, _ = self.step_fn(t)
        first = slot == 0
        for h in range(self.heads):
            p = jnp.exp2((self.s_ref[h, t] - jnp.where(first, ms[0][h], ms[1][h])).astype(BF16))
            self.acc_ref[slot, h] += jnp.dot(self.vt_fn(h, j), p, preferred_element_type=F32)

    def run(self, steps):
        ms = lax.fori_loop(0, steps, self.scores, self.initial_max(), unroll=True)
        self.clear()

        def body(t, carry):
            self.weighted(t, ms)
            return carry

        lax.fori_loop(0, steps, body, 0, unroll=STEP_UNROLL)

    def run_unrolled(self, steps):
        ms = self.initial_max()
        for t in range(steps):
            ms = self.scores(t, ms)
        self.clear()
        for t in range(steps):
            self.weighted(t, ms)


def _causal_step(n):
    def step(t):
        first = t <= n
        return (jnp.where(first, 0, 1), jnp.where(first, t, t - n - 1), jnp.where(first, n - t, N_TILES - t))
    return step


def _window_step(n):
    def step(t):
        if isinstance(t, int):
            slot, d = divmod(t, WIN_TILES)
            idx = _pair_tiles(n)[slot]
        else:
            slot = jnp.where(t >= WIN_TILES, 1, 0)
            d = t - WIN_TILES * slot
            idx = jnp.where(slot == 0, n, N_TILES - 1 - n)
        j = idx - d
        return slot, jnp.maximum(j, 0), jnp.where(j >= 0, d, TAB_W_ROWS - 1)
    return step


def _attn_a_kernel(qa_ref, qb_ref, k_ref, v_ref, za_ref, zb_ref, r_ref, o_ref,
                   vt_ref, qt_ref, s_ref, acc_ref):
    n = pl.program_id(2)
    heads = HEADS_PER_STEP

    @pl.when(n == 0)
    def _():
        for h in range(heads):
            _value_tiles(v_ref.at[h], vt_ref.at[h])

    for slot, q_ref in enumerate((qa_ref, qb_ref)):
        for h in range(heads):
            qt_ref[slot, h] = q_ref[h]

    _PairSoftmax(_causal_step(n), heads, qt_ref, lambda h, j: k_ref[h, _tile_rows(j), :],
                 lambda h, j: vt_ref[h, j], lambda h, delta: r_ref[h, delta], s_ref, acc_ref).run(CAUSAL_STEPS)
    for slot, (z_ref, idx) in enumerate(zip((za_ref, zb_ref), _pair_tiles(n))):
        for h in range(heads):
            y = (_normalized(acc_ref[slot, h]) * z_ref[h].astype(F32)).T
            o_ref[_tile_rows(idx), h * HEAD_DIM:(h + 1) * HEAD_DIM] = y.astype(BF16)


def _resident(block_shape, index_map):
    return pl.BlockSpec(block_shape, index_map, pipeline_mode=pl.Buffered(1))


def _attn_a(k_nat, qvz_t, table):
    batch = k_nat.shape[0]
    heads = HEADS_PER_STEP
    hgroups = N_HEADS // heads

    def tile_slot(s, second):
        def index(hg, b, n):
            return (b, s * hgroups + hg, 0, N_TILES - 1 - n if second else n)
        return pl.BlockSpec((None, heads, HEAD_DIM, TILE), index)

    return pl.pallas_call(
        _attn_a_kernel,
        grid=(hgroups, batch, N_PAIRS),
        in_specs=[tile_slot(0, False), tile_slot(0, True),
                  pl.BlockSpec((None, heads, SEQ, HEAD_DIM), lambda hg, b, n: (b, hg, 0, 0)),
                  pl.BlockSpec((None, heads, HEAD_DIM, SEQ), lambda hg, b, n: (b, hgroups + hg, 0, 0)),
                  tile_slot(2, False), tile_slot(2, True),
                  pl.BlockSpec((heads, N_TILES, TILE, TILE), lambda hg, b, n: (hg, TAB_A // N_TILES, 0, 0))],
        out_specs=pl.BlockSpec((None, SEQ, heads * HEAD_DIM), lambda hg, b, n: (b, 0, hg)),
        out_shape=jax.ShapeDtypeStruct((batch, SEQ, ATT_WIDTH), BF16),
        scratch_shapes=[
            pltpu.VMEM((heads, N_TILES, VT_ROWS, TILE), BF16),
            pltpu.VMEM((2, heads, HEAD_DIM, TILE), BF16),
            pltpu.VMEM((heads, CAUSAL_STEPS, TILE, TILE), F32),
            pltpu.VMEM((2, heads, VT_ROWS, TILE), F32),
        ],
        compiler_params=pltpu.CompilerParams(dimension_semantics=("parallel", "parallel", "arbitrary"),
                                             vmem_limit_bytes=VMEM_LIMIT),
        name="dilated_attn",
    )(qvz_t, qvz_t, k_nat, qvz_t, qvz_t, qvz_t, table)


def _cmp_kernel(c_ref, pos_ref, w1_ref, w2_ref, o_ref):
    for g in range(NSA_KV_GROUPS):
        c = c_ref[g].astype(F32)
        x_lo = (c + pos_ref[0]).astype(BF16)
        x_hi = (c + pos_ref[1]).astype(BF16)
        a = jnp.dot(x_lo, w1_ref[0:CMP_CHUNK, :], preferred_element_type=F32)
        bm = jnp.dot(x_hi, w1_ref[CMP_CHUNK:2 * CMP_CHUNK, :], preferred_element_type=F32)
        hid = jax.nn.gelu(a + pltpu.roll(bm, N_CMP - 1, 0))
        res = jnp.dot(hid.astype(BF16), w2_ref[...], preferred_element_type=F32)
        o_ref[g] = jnp.where(pl.program_id(0) == 1, res.T, res).astype(BF16)


def _compress(chunks, pos, w1_bf, w2_bf):
    batch = chunks.shape[0]
    return pl.pallas_call(
        _cmp_kernel,
        grid=(2, batch),
        in_specs=[
            pl.BlockSpec((None, NSA_KV_GROUPS, N_CMP, CMP_CHUNK), lambda t, b: (b, t, 0, 0)),
            pl.BlockSpec((None, 2, 1, CMP_CHUNK), lambda t, b: (t, 0, 0, 0)),
            pl.BlockSpec((None, 2 * CMP_CHUNK, CMP_HIDDEN), lambda t, b: (t, 0, 0)),
            pl.BlockSpec((None, CMP_HIDDEN, HEAD_DIM), lambda t, b: (t, 0, 0)),
        ],
        out_specs=pl.BlockSpec((None, None, NSA_KV_GROUPS, N_CMP, HEAD_DIM), lambda t, b: (t, b, 0, 0, 0)),
        out_shape=jax.ShapeDtypeStruct((2, batch, NSA_KV_GROUPS, N_CMP, HEAD_DIM), BF16),
        compiler_params=pltpu.CompilerParams(dimension_semantics=("arbitrary", "arbitrary"),
                                             vmem_limit_bytes=VMEM_LIMIT),
        name="compress_kv",
    )(chunks, pos, w1_bf, w2_bf)


def _nsa_kernel(qa_ref, qb_ref, zca_ref, zcb_ref, zsa_ref, zsb_ref, zwa_ref, zwb_ref, gla_ref, glb_ref,
                kc_ref, vct_ref, ks_ref, vs_ref, kw_ref, vw_ref, rs_ref, rw_ref, rca_ref, rcb_ref,
                ovt_ref, onehot_ref, o_ref,
                kaug_ref, vst_ref, vwt_ref, qa_sc, qw_sc, gates_ref, ycmp_ref, s_ref, accs_ref, accw_ref):
    n = pl.program_id(2)
    heads = NSA_HEADS_PER_GROUP
    tiles = _pair_tiles(n)

    @pl.when(n == 0)
    def _():
        kaug_ref[:, 0:HEAD_DIM] = ks_ref[...]
        kaug_ref[:, HEAD_DIM:2 * HEAD_DIM] = onehot_ref[...]
        _value_tiles(vs_ref, vst_ref)
        _value_tiles(vw_ref, vwt_ref)

    kc = kc_ref[...]
    vct = vct_ref[...]
    for slot, (q_ref, rc_ref, zc_ref, gl_ref) in enumerate(
            ((qa_ref, rca_ref, zca_ref, gla_ref), (qb_ref, rcb_ref, zcb_ref, glb_ref))):
        psum = jnp.zeros((N_CMP, TILE), F32)
        gates_ref[slot] = jax.nn.sigmoid(gl_ref[...].T)
        for h in range(heads):
            qt = q_ref[h]
            qw_sc[slot, h] = qt
            qa_sc[slot, h, 0:HEAD_DIM, :] = qt
            bias = rc_ref[h]
            s = jnp.dot(kc, qt, preferred_element_type=F32) + bias
            m = jnp.max(s, axis=0, keepdims=True)
            p = jnp.where(bias > 0.5 * NEG, jnp.exp2(s - m), 0.0)
            p = p / jnp.maximum(jnp.sum(p, axis=0, keepdims=True), 1e-30)
            psum = psum + p
            o_cmp = jnp.dot(vct, p.astype(BF16), preferred_element_type=F32)
            ycmp_ref[slot, h] = gates_ref[slot, h:h + 1, :] * o_cmp * zc_ref[h].astype(F32)

        imp = jnp.dot(ovt_ref[...], psum, precision=lax.Precision.HIGHEST, preferred_element_type=F32)
        blk = lax.broadcasted_iota(jnp.int32, (N_SLC, TILE), 0)
        tq = tiles[slot] * TILE + lax.broadcasted_iota(jnp.int32, (N_SLC, TILE), 1)
        cur = lax.shift_right_logical(tq, int(math.log2(SLC_BLOCK)))
        forced = (blk == 0) | (blk == cur) | (blk == cur - 1)
        score = jnp.where(forced, FORCE_SCORE, jnp.where(blk > cur, -FORCE_SCORE, imp))
        group = 8
        rows = [score[r:r + group, :] for r in range(0, N_SLC, group)]
        row_idx = lax.broadcasted_iota(jnp.int32, (group, TILE), 0)
        cnts = [jnp.zeros((group, TILE), jnp.int32) for _ in rows]
        for jj in range(N_SLC):
            sj = score[jj:jj + 1, :]
            for r, sr in enumerate(rows):
                lo = r * group
                if lo > jj:
                    beats = jnp.where(sj >= sr, 1, 0)
                elif lo + group - 1 <= jj:
                    beats = jnp.where(sj > sr, 1, 0)
                else:
                    beats = jnp.where(row_idx + lo > jj, jnp.where(sj >= sr, 1, 0), jnp.where(sj > sr, 1, 0))
                cnts[r] = cnts[r] + beats
        cnt = jnp.concatenate(cnts, axis=0)
        mask = jnp.where(cnt < SLC_TOP_N, 0.0, NEG)
        mask = jnp.concatenate([mask, jnp.zeros((HEAD_DIM - N_SLC, TILE), F32)], axis=0).astype(BF16)
        for h in range(heads):
            qa_sc[slot, h, HEAD_DIM:2 * HEAD_DIM, :] = mask

    _PairSoftmax(_causal_step(n), heads, qa_sc, lambda h, j: kaug_ref[_tile_rows(j), :],
                 lambda h, j: vst_ref[j], lambda h, delta: rs_ref[h, delta], s_ref, accs_ref).run(CAUSAL_STEPS)
    _PairSoftmax(_window_step(n), heads, qw_sc, lambda h, j: kw_ref[_tile_rows(j), :],
                 lambda h, j: vwt_ref[j], lambda h, b: rw_ref[h, b], s_ref, accw_ref).run_unrolled(2 * WIN_TILES)

    for slot, (zs_ref, zw_ref) in enumerate(((zsa_ref, zwa_ref), (zsb_ref, zwb_ref))):
        for h in range(heads):
            y = (ycmp_ref[slot, h]
                 + gates_ref[slot, heads + h:heads + h + 1, :] * _normalized(accs_ref[slot, h])
                 * zs_ref[h].astype(F32)
                 + gates_ref[slot, 2 * heads + h:2 * heads + h + 1, :] * _normalized(accw_ref[slot, h])
                 * zw_ref[h].astype(F32))
            o_ref[_tile_rows(tiles[slot]), h * HEAD_DIM:(h + 1) * HEAD_DIM] = y.T.astype(BF16)


def _nsa_attn(qz_t, gate_logits, kv_nat, kv_t, cmp_kv, table, table_c):
    batch = qz_t.shape[0]
    heads = NSA_HEADS_PER_GROUP
    groups = NSA_KV_GROUPS
    _, _, _, _, ov_t, onehot = _static_maps()

    def tile_of(n, second):
        return N_TILES - 1 - n if second else n

    def qz_specs(slot_group):
        return [pl.BlockSpec((None, heads, HEAD_DIM, TILE),
                             lambda g, b, n, second=second: (b, slot_group * groups + g, 0, tile_of(n, second)))
                for second in (False, True)]

    def key_spec(branch):
        return pl.BlockSpec((None, None, SEQ, HEAD_DIM), lambda g, b, n: (b, branch * groups + g, 0, 0))

    def value_spec(branch):
        return pl.BlockSpec((None, None, HEAD_DIM, SEQ), lambda g, b, n: (b, branch * groups + g, 0, 0))

    def cmp_spec(t):
        return pl.BlockSpec((None, None, None, N_CMP, HEAD_DIM), lambda g, b, n: (t, b, g, 0, 0))

    gl_specs = [pl.BlockSpec((None, TILE, HEAD_DIM), lambda g, b, n, second=second: (b, tile_of(n, second), g))
                for second in (False, True)]
    rc_specs = [pl.BlockSpec((heads, None, N_CMP, TILE), lambda g, b, n, second=second: (g, tile_of(n, second), 0, 0))
                for second in (False, True)]

    return pl.pallas_call(
        _nsa_kernel,
        grid=(groups, batch, N_PAIRS),
        in_specs=[
            *qz_specs(0), *qz_specs(1), *qz_specs(2), *qz_specs(3), *gl_specs,
            cmp_spec(0), cmp_spec(1), key_spec(0), value_spec(0), key_spec(1), value_spec(1),
            pl.BlockSpec((heads, N_TILES, TILE, TILE), lambda g, b, n: (g, TAB_S // N_TILES, 0, 0)),
            pl.BlockSpec((heads, TAB_W_ROWS, TILE, TILE), lambda g, b, n: (g, TAB_W // TAB_W_ROWS, 0, 0)),
            *rc_specs,
            _resident((N_SLC, N_CMP), lambda g, b, n: (0, 0)),
            _resident((SEQ, HEAD_DIM), lambda g, b, n: (0, 0)),
        ],
        out_specs=pl.BlockSpec((None, SEQ, heads * HEAD_DIM), lambda g, b, n: (b, 0, g)),
        out_shape=jax.ShapeDtypeStruct((batch, SEQ, ATT_WIDTH), BF16),
        scratch_shapes=[
            pltpu.VMEM((SEQ, 2 * HEAD_DIM), BF16),
            pltpu.VMEM((N_TILES, VT_ROWS, TILE), BF16),
            pltpu.VMEM((N_TILES, VT_ROWS, TILE), BF16),
            pltpu.VMEM((2, heads, 2 * HEAD_DIM, TILE), BF16),
            pltpu.VMEM((2, heads, HEAD_DIM, TILE), BF16),
            pltpu.VMEM((2, HEAD_DIM, TILE), F32),
            pltpu.VMEM((2, heads, HEAD_DIM, TILE), F32),
            pltpu.VMEM((heads, CAUSAL_STEPS, TILE, TILE), F32),
            pltpu.VMEM((2, heads, VT_ROWS, TILE), F32),
            pltpu.VMEM((2, heads, VT_ROWS, TILE), F32),
        ],
        compiler_params=pltpu.CompilerParams(dimension_semantics=("parallel", "parallel", "arbitrary"),
                                             vmem_limit_bytes=VMEM_LIMIT),
        name="nsa_attn",
    )(qz_t, qz_t, qz_t, qz_t, qz_t, qz_t, qz_t, qz_t, gate_logits, gate_logits, cmp_kv, cmp_kv,
      kv_nat, kv_t, kv_nat, kv_t, table, table, table_c, table_c,
      jnp.asarray(ov_t), jnp.asarray(onehot, dtype=BF16))


def kernel(x, norm_pre, norm_post, rel_table, w_in_a, w_out_a, kv_norm, w_kv, cmp_pos_k, cmp_pos_v,
           cmp_w1_k, cmp_w2_k, cmp_w1_v, cmp_w2_v, w_in_b, w_out_b):
    batch = x.shape[0]
    m = batch * SEQ
    heads = NSA_HEADS_PER_GROUP
    groups = NSA_KV_GROUPS
    x2d = x.reshape(m, D_MODEL)
    table, table_c = _bias_tables(rel_table)

    tile_slots = 8
    tiles_per_part = ATT_WIDTH // (tile_slots * HEAD_DIM)

    kinds_a = [(kind,) * tile_slots for kind in ("tq", "n", "t", "ts") for _ in range(tiles_per_part)]
    k_nat, qvz_t = _norm_matmul(x2d, norm_pre[0], w_in_a[0], tile_kinds=kinds_a)
    att_a = _attn_a(k_nat, qvz_t, table)
    h1 = _out_proj(att_a.reshape(m, ATT_WIDTH), w_out_a[0], x2d, norm_post[0])

    kinds_kv = [("c",) * tile_slots] + [("n",) * groups + ("t",) * groups] * 2
    kv_nat, kv_t, cmp_chunks = _norm_matmul(h1, kv_norm, w_kv, tile_kinds=kinds_kv)
    pos = jnp.stack([cmp_pos_k, cmp_pos_v]).reshape(2, 2, 1, CMP_CHUNK)
    cmp_kv = _compress(cmp_chunks, pos, jnp.stack([cmp_w1_k, cmp_w1_v]).astype(BF16),
                       jnp.stack([cmp_w2_k, cmp_w2_v]).astype(BF16))

    w_bt = w_in_b[0].T
    n_main = (1 + NSA_BRANCHES) * ATT_WIDTH
    wg = w_bt[n_main:].reshape(NSA_BRANCHES, groups, heads, D_MODEL).transpose(1, 0, 2, 3)
    wg = jnp.pad(wg.reshape(groups, NSA_BRANCHES * heads, D_MODEL),
                 ((0, 0), (0, HEAD_DIM - NSA_BRANCHES * heads), (0, 0))).reshape(groups * HEAD_DIM, D_MODEL)
    kinds_b = [(kind,) * tile_slots for kind in ("tq",) + ("ts",) * NSA_BRANCHES for _ in range(tiles_per_part)]
    qz_t, gate_logits = _norm_matmul(h1, norm_pre[1], w_bt, wg.T.astype(BF16), tile_kinds=kinds_b,
                                     w_transposed=True)
    att_b = _nsa_attn(qz_t, gate_logits.reshape(batch, SEQ, groups * HEAD_DIM), kv_nat, kv_t, cmp_kv,
                      table, table_c)
    out = _out_proj(att_b.reshape(m, ATT_WIDTH), w_out_b[0], h1, norm_post[1])
    return out.reshape(batch, SEQ, D_MODEL)
```

```python
import functools
import math

import numpy as np
import jax
import jax.numpy as jnp
from jax import lax
from jax.experimental import pallas as pl
from jax.experimental.pallas import tpu as pltpu

F32 = jnp.float32
BF16 = jnp.bfloat16

D_MODEL = 2048
SEQ = 2048
N_HEADS = 16
HEAD_DIM = 128
ATT_WIDTH = N_HEADS * HEAD_DIM
ATT_SCALE = HEAD_DIM ** -0.5
DIL_PATTERNS = ((128, 1), (512, 4), (2048, 16))
REL_BUCKETS = 32
REL_MAX_EXACT = 16
REL_MAX_DISTANCE = 2048
NSA_KV_GROUPS = 4
NSA_HEADS_PER_GROUP = 4
NSA_BRANCHES = 3
CMP_BLOCK = 32
CMP_STRIDE = 16
CMP_HIDDEN = 256
SLC_BLOCK = 64
SLC_TOP_N = 16
WIN_SIZE = 512
RMS_EPS = 1e-6
NEG = -1e30
FORCE_SCORE = 1e9
LOG2E = math.log2(math.e)

TILE = 256
N_TILES = SEQ // TILE
N_PAIRS = N_TILES // 2
CAUSAL_STEPS = N_TILES + 1
STEP_UNROLL = 3
N_CMP = SEQ // CMP_STRIDE
CMP_CHUNK = CMP_STRIDE * HEAD_DIM
N_SLC = SEQ // SLC_BLOCK
WIN_TILES = -(-(WIN_SIZE - 1) // TILE) + 1
HEADS_PER_STEP = 4
VT_ROWS = HEAD_DIM + 16
MXU_COLS = 512

TAB_A, TAB_S, TAB_W = 0, N_TILES, 2 * N_TILES
TAB_W_ROWS = 4
TAB_ROWS = 2 * N_TILES + TAB_W_ROWS
GATHER_MIN_BUCKETS = 6
V7X_VMEM_BYTES = 64 * 1024 * 1024
VMEM_LIMIT = V7X_VMEM_BYTES // 8 * 7
LANES = 128
GATE_ROWS = 16


def _np_bucket(dist):
    n = np.maximum(dist, 0)
    nf = np.maximum(n, 1).astype(np.float32)
    log_b = REL_MAX_EXACT + (
        np.log(nf / np.float32(REL_MAX_EXACT)) / np.float32(math.log(REL_MAX_DISTANCE / REL_MAX_EXACT))
        * np.float32(REL_BUCKETS - REL_MAX_EXACT)).astype(np.int32)
    return np.where(n < REL_MAX_EXACT, n, np.minimum(log_b, REL_BUCKETS - 1)).astype(np.int32)


@functools.lru_cache(maxsize=None)
def _static_maps():
    ki = np.arange(TILE)[:, None]
    qi = np.arange(TILE)[None, :]
    dist = TILE * np.arange(N_TILES)[:, None, None] + qi[None] - ki[None]
    bk_t = _np_bucket(dist)
    mult = np.zeros(dist.shape, np.int64)
    for window, dil in DIL_PATTERNS:
        mult += ((dist % dil == 0) & (dist <= window)).astype(np.int64)
    base_a = np.where((dist >= 0) & (mult > 0), np.log2(np.maximum(mult, 1)), NEG)
    base_s = np.where(dist >= 0, 0.0, NEG)
    base_w = np.full((TAB_W_ROWS, TILE, TILE), NEG)
    base_w[:WIN_TILES] = np.where((dist >= 0) & (dist < WIN_SIZE), 0.0, NEG)[:WIN_TILES]
    base_t = np.concatenate([base_a, base_s, base_w]).astype(np.float32)
    ci = np.arange(N_CMP)[:, None]
    t = TILE * np.arange(N_TILES)[:, None, None] + qi[None]
    dist_c = t - (CMP_STRIDE * ci[None] + CMP_BLOCK - 1)
    bk_c = _np_bucket(dist_c)
    base_c = np.where((dist_c >= 0) & (ci[None] < N_CMP - 1), 0.0, NEG).astype(np.float32)
    cs = np.arange(N_CMP)[None, :] * CMP_STRIDE
    sj = np.arange(N_SLC)[:, None] * SLC_BLOCK
    ov_t = ((cs < sj + SLC_BLOCK) & (cs + CMP_BLOCK > sj) & (np.arange(N_CMP)[None, :] < N_CMP - 1))
    onehot = np.zeros((SEQ, HEAD_DIM), np.float32)
    onehot[np.arange(SEQ), np.arange(SEQ) // SLC_BLOCK] = 1.0
    return bk_t.astype(np.int32), base_t, bk_c.astype(np.int32), base_c, ov_t.astype(np.float32), onehot


def _bias_kernel(tab_ref, tabt_ref, bkt_ref, baset_ref, bkc_ref, basec_ref, out_ref, outc_ref, *,
                 present_t, present_c):
    h = pl.program_id(0)
    tv = [tab_ref[b, h] * LOG2E for b in range(REL_BUCKETS)]
    lanes = LANES
    row = jnp.broadcast_to(tabt_ref[pl.ds(h, 1), :] * LOG2E, (TILE, lanes))

    def lookup(bk, present):
        if len(present) > GATHER_MIN_BUCKETS:
            parts = [jnp.take_along_axis(row[:bk.shape[0]], bk[:, c:c + lanes], axis=1)
                     for c in range(0, bk.shape[1], lanes)]
            return jnp.concatenate(parts, axis=1)
        val = jnp.full(bk.shape, tv[present[0]], F32)
        for b in present[1:]:
            val = jnp.where(bk == b, tv[b], val)
        return val

    for d in range(N_TILES):
        g = lookup(bkt_ref[d], present_t[d])
        out_ref[TAB_A + d] = g + baset_ref[TAB_A + d]
        out_ref[TAB_S + d] = g + baset_ref[TAB_S + d]
        if d < TAB_W_ROWS:
            out_ref[TAB_W + d] = g + baset_ref[TAB_W + d]
        outc_ref[d] = lookup(bkc_ref[d], present_c[d]) + basec_ref[d]


def _bias_tables(rel_table):
    bk_t, base_t, bk_c, base_c, _, _ = _static_maps()
    present_t = tuple(tuple(int(b) for b in np.unique(bk_t[d])) for d in range(N_TILES))
    present_c = tuple(tuple(int(b) for b in np.unique(bk_c[d])) for d in range(N_TILES))
    return pl.pallas_call(
        functools.partial(_bias_kernel, present_t=present_t, present_c=present_c),
        grid=(N_HEADS,),
        in_specs=[
            pl.BlockSpec(memory_space=pltpu.SMEM),
            pl.BlockSpec((N_HEADS, LANES), lambda h: (0, 0)),
            pl.BlockSpec((N_TILES, TILE, TILE), lambda h: (0, 0, 0)),
            pl.BlockSpec((TAB_ROWS, TILE, TILE), lambda h: (0, 0, 0)),
            pl.BlockSpec((N_TILES, N_CMP, TILE), lambda h: (0, 0, 0)),
            pl.BlockSpec((N_TILES, N_CMP, TILE), lambda h: (0, 0, 0)),
        ],
        out_specs=[
            pl.BlockSpec((None, TAB_ROWS, TILE, TILE), lambda h: (h, 0, 0, 0)),
            pl.BlockSpec((None, N_TILES, N_CMP, TILE), lambda h: (h, 0, 0, 0)),
        ],
        out_shape=[
            jax.ShapeDtypeStruct((N_HEADS, TAB_ROWS, TILE, TILE), F32),
            jax.ShapeDtypeStruct((N_HEADS, N_TILES, N_CMP, TILE), F32),
        ],
        compiler_params=pltpu.CompilerParams(dimension_semantics=("arbitrary",),
                                             vmem_limit_bytes=VMEM_LIMIT),
        name="bias_tables",
    )(rel_table, jnp.pad(rel_table.T, ((0, 0), (0, LANES - REL_BUCKETS))),
      jnp.asarray(bk_t), jnp.asarray(base_t), jnp.asarray(bk_c), jnp.asarray(base_c))


def _silu(z):
    half = 0.5 * z
    return half + half * jnp.tanh(half)


TRANSPOSED_KINDS = ("t", "tq", "ts")


def _nm_kernel(*refs, tile_kinds, gate, w_transposed):
    refs = list(refs)
    x_ref, g_ref, w_ref = refs[:3]
    del refs[:3]
    wg_ref = refs.pop(0) if gate else None
    kinds_used = {k for kinds in tile_kinds for k in kinds}
    on_ref = refs.pop(0) if "n" in kinds_used else None
    ot_ref = refs.pop(0) if kinds_used & set(TRANSPOSED_KINDS) else None
    og_ref = refs.pop(0) if gate else None
    oc_ref = refs.pop(0) if "c" in kinds_used else None
    xn_ref = refs.pop(0)
    res_ref = refs.pop(0) if "c" in kinds_used else None
    j = pl.program_id(1)

    @pl.when(j == 0)
    def _():
        x = x_ref[...]
        ms = jnp.mean(x * x, axis=-1, keepdims=True)
        xn_ref[...] = (x * lax.rsqrt(ms + RMS_EPS) * g_ref[...]).astype(BF16)
        if gate:
            og_ref[...] = jnp.dot(xn_ref[...], wg_ref[...], preferred_element_type=F32)

    def product(kinds):
        pos = {"n": 0, "t": 0}
        per_group = MXU_COLS // HEAD_DIM
        for cc in range(len(kinds) // per_group):
            if w_transposed:
                w_cols = w_ref[cc * MXU_COLS:(cc + 1) * MXU_COLS, :].astype(BF16)
                res = lax.dot_general(xn_ref[...], w_cols, (((1,), (1,)), ((), ())), preferred_element_type=F32)
            else:
                w_cols = w_ref[:, cc * MXU_COLS:(cc + 1) * MXU_COLS].astype(BF16)
                res = jnp.dot(xn_ref[...], w_cols, preferred_element_type=F32)
            for u in range(per_group):
                c = per_group * cc + u
                piece = res[:, u * HEAD_DIM:(u + 1) * HEAD_DIM]
                kind = kinds[c]
                if kind == "n":
                    on_ref[pos["n"]] = piece.astype(BF16)
                    pos["n"] += 1
                elif kind == "c":
                    res_ref[c] = piece
                else:
                    if kind == "tq":
                        piece = piece * (ATT_SCALE * LOG2E)
                    elif kind == "ts":
                        piece = _silu(piece)
                    ot_ref[pos["t"]] = piece.T.astype(BF16)
                    pos["t"] += 1
        if "c" in kinds:
            rows = res_ref.shape[1] // CMP_STRIDE
            for c in range(len(kinds)):
                for i in range(CMP_STRIDE):
                    oc_ref[c, :, i * HEAD_DIM:(i + 1) * HEAD_DIM] = res_ref[
                        c, pl.ds(i, rows, stride=CMP_STRIDE), :].astype(BF16)

    for kinds in sorted(set(tile_kinds)):
        tiles = [t for t, k in enumerate(tile_kinds) if k == kinds]
        cond = functools.reduce(lambda a, b: a | b, [j == t for t in tiles])
        pl.when(cond)(functools.partial(product, kinds))


def _norm_matmul(x2d, gain, w, wg_bf=None, *, tile_kinds, w_transposed=False, tm=1024, tn=1024):
    m, d = x2d.shape
    slots = tn // HEAD_DIM
    batch = m // SEQ
    spb = SEQ // tm
    gate = wg_bf is not None
    assert all(len(k) == slots for k in tile_kinds)

    def family(match):
        counts = [sum(1 for k in kinds if match(k)) for kinds in tile_kinds]
        tiles = [t for t, c in enumerate(counts) if c]
        per_tile = counts[tiles[0]] if tiles else 0
        assert all(counts[t] == per_tile for t in tiles)

        def block(j):
            return jnp.minimum(sum(jnp.where(j > t, 1, 0) for t in tiles), len(tiles) - 1)
        return tiles, per_tile, block

    n_tiles, n_per, n_block = family(lambda k: k == "n")
    t_tiles, t_per, t_block = family(lambda k: k in TRANSPOSED_KINDS)
    c_tiles, c_per, _ = family(lambda k: k == "c")
    assert c_tiles in ([], [0]) and c_per in (0, slots)

    in_specs = [
        pl.BlockSpec((tm, d), lambda i, j: (i, 0)),
        pl.BlockSpec((1, d), lambda i, j: (0, 0)),
        pl.BlockSpec((tn, d), lambda i, j: (j, 0)) if w_transposed else pl.BlockSpec((d, tn), lambda i, j: (0, j)),
    ]
    args = [x2d, gain.reshape(1, d), w]
    out_shape, out_specs = [], []
    scratch = [pltpu.VMEM((tm, d), BF16)]
    if gate:
        in_specs.append(pl.BlockSpec((d, wg_bf.shape[1]), lambda i, j: (0, 0)))
        args.append(wg_bf)
    if n_tiles:
        out_shape.append(jax.ShapeDtypeStruct((batch, n_per * len(n_tiles), SEQ, HEAD_DIM), BF16))
        out_specs.append(pl.BlockSpec((None, n_per, tm, HEAD_DIM),
                                      lambda i, j: (i // spb, n_block(j), i % spb, 0)))
    if t_tiles:
        out_shape.append(jax.ShapeDtypeStruct((batch, t_per * len(t_tiles), HEAD_DIM, SEQ), BF16))
        out_specs.append(pl.BlockSpec((None, t_per, HEAD_DIM, tm),
                                      lambda i, j: (i // spb, t_block(j), 0, i % spb)))
    if gate:
        out_shape.append(jax.ShapeDtypeStruct((m, wg_bf.shape[1]), F32))
        out_specs.append(pl.BlockSpec((tm, wg_bf.shape[1]), lambda i, j: (i, 0)))
    if c_tiles:
        out_shape.append(jax.ShapeDtypeStruct((batch, slots, N_CMP, CMP_CHUNK), BF16))
        out_specs.append(pl.BlockSpec((None, slots, tm // CMP_STRIDE, CMP_CHUNK),
                                      lambda i, j: (i // spb, 0, i % spb, 0)))
        scratch.append(pltpu.VMEM((slots, tm, HEAD_DIM), F32))
    return pl.pallas_call(
        functools.partial(_nm_kernel, tile_kinds=tuple(tuple(k) for k in tile_kinds), gate=gate,
                          w_transposed=w_transposed),
        grid=(m // tm, len(tile_kinds)),
        in_specs=in_specs,
        out_specs=out_specs,
        out_shape=out_shape,
        scratch_shapes=scratch,
        compiler_params=pltpu.CompilerParams(dimension_semantics=("parallel", "arbitrary"),
                                             vmem_limit_bytes=VMEM_LIMIT),
        name="norm_matmul",
    )(*args)


def _op_kernel(a_ref, w_ref, r_ref, g_ref, o_ref, wbf_ref, *, parts):
    @pl.when(pl.program_id(0) == 0)
    def _():
        step = 256
        for r in range(w_ref.shape[0] // step):
            wbf_ref[r * step:(r + 1) * step, :] = w_ref[r * step:(r + 1) * step, :].astype(BF16)

    rows = a_ref.shape[0] // parts
    for r in range(parts):
        sl = slice(r * rows, (r + 1) * rows)
        y = jnp.dot(a_ref[sl, :], wbf_ref[...], preferred_element_type=F32)
        ms = jnp.mean(y * y, axis=-1, keepdims=True)
        o_ref[sl, :] = r_ref[sl, :] + y * lax.rsqrt(ms + RMS_EPS) * g_ref[...]


def _out_proj(a2d, w, resid2d, gain, *, tm=512, parts=2):
    m, k = a2d.shape
    n = w.shape[1]
    return pl.pallas_call(
        functools.partial(_op_kernel, parts=parts),
        grid=(m // tm,),
        in_specs=[
            pl.BlockSpec((tm, k), lambda i: (i, 0)),
            pl.BlockSpec((k, n), lambda i: (0, 0), pipeline_mode=pl.Buffered(1)),
            pl.BlockSpec((tm, n), lambda i: (i, 0)),
            pl.BlockSpec((1, n), lambda i: (0, 0)),
        ],
        out_specs=pl.BlockSpec((tm, n), lambda i: (i, 0)),
        out_shape=jax.ShapeDtypeStruct((m, n), F32),
        scratch_shapes=[pltpu.VMEM((k, n), BF16)],
        compiler_params=pltpu.CompilerParams(dimension_semantics=("arbitrary",),
                                             vmem_limit_bytes=VMEM_LIMIT),
        name="out_proj",
    )(a2d, w, resid2d, gain.reshape(1, n))


def _value_tiles(src_ref, dst_ref):
    for c in range(N_TILES):
        dst_ref[c, 0:HEAD_DIM, :] = src_ref[:, c * TILE:(c + 1) * TILE]
        dst_ref[c, HEAD_DIM:VT_ROWS, :] = jnp.ones((VT_ROWS - HEAD_DIM, TILE), BF16)


def _pair_tiles(n):
    return (n, N_TILES - 1 - n)


def _tile_rows(idx):
    return pl.ds(pl.multiple_of(idx * TILE, TILE), TILE)


def _normalized(acc):
    return acc[0:HEAD_DIM, :] / acc[HEAD_DIM:HEAD_DIM + 1, :]


class _PairSoftmax:
    def __init__(self, step_fn, heads, q_ref, k_fn, vt_fn, bias_fn, s_ref, acc_ref):
        self.step_fn, self.heads, self.q_ref, self.k_fn, self.vt_fn = step_fn, heads, q_ref, k_fn, vt_fn
        self.bias_fn, self.s_ref, self.acc_ref = bias_fn, s_ref, acc_ref

    def initial_max(self):
        neg = (jnp.full((1, TILE), NEG, F32),) * self.heads
        return neg, neg

    def scores(self, t, ms):
        slot, j, b = self.step_fn(t)
        first = slot == 0
        m0, m1 = list(ms[0]), list(ms[1])
        for h in range(self.heads):
            s = (jnp.dot(self.k_fn(h, j), self.q_ref[slot, h], preferred_element_type=F32)
                 + self.bias_fn(h, b))
            self.s_ref[h, t] = s
            cm = jnp.max(s, axis=0, keepdims=True)
            m0[h] = jnp.where(first, jnp.maximum(m0[h], cm), m0[h])
            m1[h] = jnp.where(first, m1[h], jnp.maximum(m1[h], cm))
        return tuple(m0), tuple(m1)

    def clear(self):
        for slot in range(2):
            for h in range(self.heads):
                self.acc_ref[slot, h] = jnp.zeros((VT_ROWS, TILE), F32)

    def weighted(self, t, ms):
        slot, j, _ = self.step_fn(t)
        first = slot == 0
        for h in range(self.heads):
            p = jnp.exp2((self.s_ref[h, t] - jnp.where(first, ms[0][h], ms[1][h])).astype(BF16))
            self.acc_ref[slot, h] += jnp.dot(self.vt_fn(h, j), p, preferred_element_type=F32)

    def run(self, steps):
        ms = lax.fori_loop(0, steps, self.scores, self.initial_max(), unroll=True)
        self.clear()

        def body(t, carry):
            self.weighted(t, ms)
            return carry

        lax.fori_loop(0, steps, body, 0, unroll=STEP_UNROLL)

    def run_unrolled(self, steps):
        ms = self.initial_max()
        for t in range(steps):
            ms = self.scores(t, ms)
        self.clear()
        for t in range(steps):
            self.weighted(t, ms)


def _causal_step(n):
    def step(t):
        first = t <= n
        return (jnp.where(first, 0, 1), jnp.where(first, t, t - n - 1), jnp.where(first, n - t, N_TILES - t))
    return step


def _window_step(n):
    def step(t):
        if isinstance(t, int):
            slot, d = divmod(t, WIN_TILES)
            idx = _pair_tiles(n)[slot]
        else:
            slot = jnp.where(t >= WIN_TILES, 1, 0)
            d = t - WIN_TILES * slot
            idx = jnp.where(slot == 0, n, N_TILES - 1 - n)
        j = idx - d
        return slot, jnp.maximum(j, 0), jnp.where(j >= 0, d, TAB_W_ROWS - 1)
    return step


def _attn_a_kernel(qa_ref, qb_ref, k_ref, v_ref, za_ref, zb_ref, r_ref, o_ref,
                   vt_ref, qt_ref, s_ref, acc_ref):
    n = pl.program_id(2)
    heads = HEADS_PER_STEP

    @pl.when(n == 0)
    def _():
        for h in range(heads):
            _value_tiles(v_ref.at[h], vt_ref.at[h])

    for slot, q_ref in enumerate((qa_ref, qb_ref)):
        for h in range(heads):
            qt_ref[slot, h] = q_ref[h]

    _PairSoftmax(_causal_step(n), heads, qt_ref, lambda h, j: k_ref[h, _tile_rows(j), :],
                 lambda h, j: vt_ref[h, j], lambda h, delta: r_ref[h, delta], s_ref, acc_ref).run(CAUSAL_STEPS)
    for slot, (z_ref, idx) in enumerate(zip((za_ref, zb_ref), _pair_tiles(n))):
        for h in range(heads):
            y = (_normalized(acc_ref[slot, h]) * z_ref[h].astype(F32)).T
            o_ref[_tile_rows(idx), h * HEAD_DIM:(h + 1) * HEAD_DIM] = y.astype(BF16)


def _resident(block_shape, index_map):
    return pl.BlockSpec(block_shape, index_map, pipeline_mode=pl.Buffered(1))


def _attn_a(k_nat, qvz_t, table):
    batch = k_nat.shape[0]
    heads = HEADS_PER_STEP
    hgroups = N_HEADS // heads

    def tile_slot(s, second):
        def index(hg, b, n):
            return (b, s * hgroups + hg, 0, N_TILES - 1 - n if second else n)
        return pl.BlockSpec((None, heads, HEAD_DIM, TILE), index)

    return pl.pallas_call(
        _attn_a_kernel,
        grid=(hgroups, batch, N_PAIRS),
        in_specs=[tile_slot(0, False), tile_slot(0, True),
                  pl.BlockSpec((None, heads, SEQ, HEAD_DIM), lambda hg, b, n: (b, hg, 0, 0)),
                  pl.BlockSpec((None, heads, HEAD_DIM, SEQ), lambda hg, b, n: (b, hgroups + hg, 0, 0)),
                  tile_slot(2, False), tile_slot(2, True),
                  pl.BlockSpec((heads, N_TILES, TILE, TILE), lambda hg, b, n: (hg, TAB_A // N_TILES, 0, 0))],
        out_specs=pl.BlockSpec((None, SEQ, heads * HEAD_DIM), lambda hg, b, n: (b, 0, hg)),
        out_shape=jax.ShapeDtypeStruct((batch, SEQ, ATT_WIDTH), BF16),
        scratch_shapes=[
            pltpu.VMEM((heads, N_TILES, VT_ROWS, TILE), BF16),
            pltpu.VMEM((2, heads, HEAD_DIM, TILE), BF16),
            pltpu.VMEM((heads, CAUSAL_STEPS, TILE, TILE), F32),
            pltpu.VMEM((2, heads, VT_ROWS, TILE), F32),
        ],
        compiler_params=pltpu.CompilerParams(dimension_semantics=("parallel", "parallel", "arbitrary"),
                                             vmem_limit_bytes=VMEM_LIMIT),
        name="dilated_attn",
    )(qvz_t, qvz_t, k_nat, qvz_t, qvz_t, qvz_t, table)


def _cmp_kernel(c_ref, pos_ref, w1_ref, w2_ref, o_ref):
    for g in range(NSA_KV_GROUPS):
        c = c_ref[g].astype(F32)
        x_lo = (c + pos_ref[0]).astype(BF16)
        x_hi = (c + pos_ref[1]).astype(BF16)
        a = jnp.dot(x_lo, w1_ref[0:CMP_CHUNK, :], preferred_element_type=F32)
        bm = jnp.dot(x_hi, w1_ref[CMP_CHUNK:2 * CMP_CHUNK, :], preferred_element_type=F32)
        hid = jax.nn.gelu(a + pltpu.roll(bm, N_CMP - 1, 0))
        res = jnp.dot(hid.astype(BF16), w2_ref[...], preferred_element_type=F32)
        o_ref[g] = jnp.where(pl.program_id(0) == 1, res.T, res).astype(BF16)


def _compress(chunks, pos, w1_bf, w2_bf):
    batch = chunks.shape[0]
    return pl.pallas_call(
        _cmp_kernel,
        grid=(2, batch),
        in_specs=[
            pl.BlockSpec((None, NSA_KV_GROUPS, N_CMP, CMP_CHUNK), lambda t, b: (b, t, 0, 0)),
            pl.BlockSpec((None, 2, 1, CMP_CHUNK), lambda t, b: (t, 0, 0, 0)),
            pl.BlockSpec((None, 2 * CMP_CHUNK, CMP_HIDDEN), lambda t, b: (t, 0, 0)),
            pl.BlockSpec((None, CMP_HIDDEN, HEAD_DIM), lambda t, b: (t, 0, 0)),
        ],
        out_specs=pl.BlockSpec((None, None, NSA_KV_GROUPS, N_CMP, HEAD_DIM), lambda t, b: (t, b, 0, 0, 0)),
        out_shape=jax.ShapeDtypeStruct((2, batch, NSA_KV_GROUPS, N_CMP, HEAD_DIM), BF16),
        compiler_params=pltpu.CompilerParams(dimension_semantics=("arbitrary", "arbitrary"),
                                             vmem_limit_bytes=VMEM_LIMIT),
        name="compress_kv",
    )(chunks, pos, w1_bf, w2_bf)


def _nsa_kernel(qa_ref, qb_ref, zca_ref, zcb_ref, zsa_ref, zsb_ref, zwa_ref, zwb_ref, gla_ref, glb_ref,
                kc_ref, vct_ref, ks_ref, vs_ref, kw_ref, vw_ref, rs_ref, rw_ref, rca_ref, rcb_ref,
                ovt_ref, onehot_ref, o_ref,
                kaug_ref, vst_ref, vwt_ref, qa_sc, qw_sc, gates_ref, ycmp_ref, s_ref, accs_ref, accw_ref):
    n = pl.program_id(2)
    heads = NSA_HEADS_PER_GROUP
    tiles = _pair_tiles(n)
    gate_base = pl.multiple_of(pl.program_id(0) * GATE_ROWS, GATE_ROWS)

    def gate(slot, branch, h):
        return gates_ref[slot, pl.ds(gate_base + branch * heads + h, 1), :]

    @pl.when(n == 0)
    def _():
        kaug_ref[:, 0:HEAD_DIM] = ks_ref[...]
        kaug_ref[:, HEAD_DIM:2 * HEAD_DIM] = onehot_ref[...]
        _value_tiles(vs_ref, vst_ref)
        _value_tiles(vw_ref, vwt_ref)

    kc = kc_ref[...]
    vct = vct_ref[...]
    for slot, (q_ref, rc_ref, zc_ref, gl_ref) in enumerate(
            ((qa_ref, rca_ref, zca_ref, gla_ref), (qb_ref, rcb_ref, zcb_ref, glb_ref))):
        psum = jnp.zeros((N_CMP, TILE), F32)
        gates_ref[slot] = jax.nn.sigmoid(gl_ref[...].T)
        for h in range(heads):
            qt = q_ref[h]
            qw_sc[slot, h] = qt
            qa_sc[slot, h, 0:HEAD_DIM, :] = qt
            bias = rc_ref[h]
            s = jnp.dot(kc, qt, preferred_element_type=F32) + bias
            m = jnp.max(s, axis=0, keepdims=True)
            p = jnp.where(bias > 0.5 * NEG, jnp.exp2(s - m), 0.0)
            p = p / jnp.maximum(jnp.sum(p, axis=0, keepdims=True), 1e-30)
            psum = psum + p
            o_cmp = jnp.dot(vct, p.astype(BF16), preferred_element_type=F32)
            ycmp_ref[slot, h] = gate(slot, 0, h) * o_cmp * zc_ref[h].astype(F32)

        imp = jnp.dot(ovt_ref[...], psum, precision=lax.Precision.HIGHEST, preferred_element_type=F32)
        blk = lax.broadcasted_iota(jnp.int32, (N_SLC, TILE), 0)
        tq = tiles[slot] * TILE + lax.broadcasted_iota(jnp.int32, (N_SLC, TILE), 1)
        cur = lax.shift_right_logical(tq, int(math.log2(SLC_BLOCK)))
        forced = (blk == 0) | (blk == cur) | (blk == cur - 1)
        score = jnp.where(forced, FORCE_SCORE, jnp.where(blk > cur, -FORCE_SCORE, imp))
        group = 8
        rows = [score[r:r + group, :] for r in range(0, N_SLC, group)]
        row_idx = lax.broadcasted_iota(jnp.int32, (group, TILE), 0)
        cnts = [jnp.zeros((group, TILE), jnp.int32) for _ in rows]
        for jj in range(N_SLC):
            sj = score[jj:jj + 1, :]
            for r, sr in enumerate(rows):
                lo = r * group
                if lo > jj:
                    beats = jnp.where(sj >= sr, 1, 0)
                elif lo + group - 1 <= jj:
                    beats = jnp.where(sj > sr, 1, 0)
                else:
                    beats = jnp.where(row_idx + lo > jj, jnp.where(sj >= sr, 1, 0), jnp.where(sj > sr, 1, 0))
                cnts[r] = cnts[r] + beats
        cnt = jnp.concatenate(cnts, axis=0)
        mask = jnp.where(cnt < SLC_TOP_N, 0.0, NEG)
        mask = jnp.concatenate([mask, jnp.zeros((HEAD_DIM - N_SLC, TILE), F32)], axis=0).astype(BF16)
        for h in range(heads):
            qa_sc[slot, h, HEAD_DIM:2 * HEAD_DIM, :] = mask

    _PairSoftmax(_causal_step(n), heads, qa_sc, lambda h, j: kaug_ref[_tile_rows(j), :],
                 lambda h, j: vst_ref[j], lambda h, delta: rs_ref[h, delta], s_ref, accs_ref).run(CAUSAL_STEPS)
    _PairSoftmax(_window_step(n), heads, qw_sc, lambda h, j: kw_ref[_tile_rows(j), :],
                 lambda h, j: vwt_ref[j], lambda h, b: rw_ref[h, b], s_ref, accw_ref).run_unrolled(2 * WIN_TILES)

    for slot, (zs_ref, zw_ref) in enumerate(((zsa_ref, zwa_ref), (zsb_ref, zwb_ref))):
        for h in range(heads):
            y = (ycmp_ref[slot, h]
                 + gate(slot, 1, h) * _normalized(accs_ref[slot, h]) * zs_ref[h].astype(F32)
                 + gate(slot, 2, h) * _normalized(accw_ref[slot, h]) * zw_ref[h].astype(F32))
            o_ref[_tile_rows(tiles[slot]), h * HEAD_DIM:(h + 1) * HEAD_DIM] = y.T.astype(BF16)


def _nsa_attn(qz_t, gate_logits, kv_nat, kv_t, cmp_kv, table, table_c):
    batch = qz_t.shape[0]
    heads = NSA_HEADS_PER_GROUP
    groups = NSA_KV_GROUPS
    _, _, _, _, ov_t, onehot = _static_maps()

    def tile_of(n, second):
        return N_TILES - 1 - n if second else n

    def qz_specs(slot_group):
        return [pl.BlockSpec((None, heads, HEAD_DIM, TILE),
                             lambda g, b, n, second=second: (b, slot_group * groups + g, 0, tile_of(n, second)))
                for second in (False, True)]

    def key_spec(branch):
        return pl.BlockSpec((None, None, SEQ, HEAD_DIM), lambda g, b, n: (b, branch * groups + g, 0, 0))

    def value_spec(branch):
        return pl.BlockSpec((None, None, HEAD_DIM, SEQ), lambda g, b, n: (b, branch * groups + g, 0, 0))

    def cmp_spec(t):
        return pl.BlockSpec((None, None, None, N_CMP, HEAD_DIM), lambda g, b, n: (t, b, g, 0, 0))

    gl_specs = [pl.BlockSpec((None, TILE, HEAD_DIM), lambda g, b, n, second=second: (b, tile_of(n, second), 0))
                for second in (False, True)]
    rc_specs = [pl.BlockSpec((heads, None, N_CMP, TILE), lambda g, b, n, second=second: (g, tile_of(n, second), 0, 0))
                for second in (False, True)]

    return pl.pallas_call(
        _nsa_kernel,
        grid=(groups, batch, N_PAIRS),
        in_specs=[
            *qz_specs(0), *qz_specs(1), *qz_specs(2), *qz_specs(3), *gl_specs,
            cmp_spec(0), cmp_spec(1), key_spec(0), value_spec(0), key_spec(1), value_spec(1),
            pl.BlockSpec((heads, N_TILES, TILE, TILE), lambda g, b, n: (g, TAB_S // N_TILES, 0, 0)),
            pl.BlockSpec((heads, TAB_W_ROWS, TILE, TILE), lambda g, b, n: (g, TAB_W // TAB_W_ROWS, 0, 0)),
            *rc_specs,
            _resident((N_SLC, N_CMP), lambda g, b, n: (0, 0)),
            _resident((SEQ, HEAD_DIM), lambda g, b, n: (0, 0)),
        ],
        out_specs=pl.BlockSpec((None, SEQ, heads * HEAD_DIM), lambda g, b, n: (b, 0, g)),
        out_shape=jax.ShapeDtypeStruct((batch, SEQ, ATT_WIDTH), BF16),
        scratch_shapes=[
            pltpu.VMEM((SEQ, 2 * HEAD_DIM), BF16),
            pltpu.VMEM((N_TILES, VT_ROWS, TILE), BF16),
            pltpu.VMEM((N_TILES, VT_ROWS, TILE), BF16),
            pltpu.VMEM((2, heads, 2 * HEAD_DIM, TILE), BF16),
            pltpu.VMEM((2, heads, HEAD_DIM, TILE), BF16),
            pltpu.VMEM((2, HEAD_DIM, TILE), F32),
            pltpu.VMEM((2, heads, HEAD_DIM, TILE), F32),
            pltpu.VMEM((heads, CAUSAL_STEPS, TILE, TILE), F32),
            pltpu.VMEM((2, heads, VT_ROWS, TILE), F32),
            pltpu.VMEM((2, heads, VT_ROWS, TILE), F32),
        ],
        compiler_params=pltpu.CompilerParams(dimension_semantics=("parallel", "parallel", "arbitrary"),
                                             vmem_limit_bytes=VMEM_LIMIT),
        name="nsa_attn",
    )(qz_t, qz_t, qz_t, qz_t, qz_t, qz_t, qz_t, qz_t, gate_logits, gate_logits, cmp_kv, cmp_kv,
      kv_nat, kv_t, kv_nat, kv_t, table, table, table_c, table_c,
      jnp.asarray(ov_t), jnp.asarray(onehot, dtype=BF16))


def kernel(x, norm_pre, norm_post, rel_table, w_in_a, w_out_a, kv_norm, w_kv, cmp_pos_k, cmp_pos_v,
           cmp_w1_k, cmp_w2_k, cmp_w1_v, cmp_w2_v, w_in_b, w_out_b):
    batch = x.shape[0]
    m = batch * SEQ
    heads = NSA_HEADS_PER_GROUP
    groups = NSA_KV_GROUPS
    x2d = x.reshape(m, D_MODEL)
    table, table_c = _bias_tables(rel_table)

    tile_slots = 8
    tiles_per_part = ATT_WIDTH // (tile_slots * HEAD_DIM)

    kinds_a = [(kind,) * tile_slots for kind in ("tq", "n", "t", "ts") for _ in range(tiles_per_part)]
    k_nat, qvz_t = _norm_matmul(x2d, norm_pre[0], w_in_a[0], tile_kinds=kinds_a)
    att_a = _attn_a(k_nat, qvz_t, table)
    h1 = _out_proj(att_a.reshape(m, ATT_WIDTH), w_out_a[0], x2d, norm_post[0])

    kinds_kv = [("c",) * tile_slots] + [("n",) * groups + ("t",) * groups] * 2
    kv_nat, kv_t, cmp_chunks = _norm_matmul(h1, kv_norm, w_kv, tile_kinds=kinds_kv)
    pos = jnp.stack([cmp_pos_k, cmp_pos_v]).reshape(2, 2, 1, CMP_CHUNK)
    cmp_kv = _compress(cmp_chunks, pos, jnp.stack([cmp_w1_k, cmp_w1_v]).astype(BF16),
                       jnp.stack([cmp_w2_k, cmp_w2_v]).astype(BF16))

    w_bt = w_in_b[0].T
    n_main = (1 + NSA_BRANCHES) * ATT_WIDTH
    wg = w_bt[n_main:].reshape(NSA_BRANCHES, groups, heads, D_MODEL).transpose(1, 0, 2, 3)
    wg = jnp.pad(wg.reshape(groups, NSA_BRANCHES * heads, D_MODEL),
                 ((0, 0), (0, GATE_ROWS - NSA_BRANCHES * heads), (0, 0))).reshape(groups * GATE_ROWS, D_MODEL)
    wg = jnp.pad(wg, ((0, HEAD_DIM - groups * GATE_ROWS), (0, 0)))
    kinds_b = [(kind,) * tile_slots for kind in ("tq",) + ("ts",) * NSA_BRANCHES for _ in range(tiles_per_part)]
    qz_t, gate_logits = _norm_matmul(h1, norm_pre[1], w_bt, wg.T.astype(BF16), tile_kinds=kinds_b,
                                     w_transposed=True)
    att_b = _nsa_attn(qz_t, gate_logits.reshape(batch, SEQ, HEAD_DIM), kv_nat, kv_t, cmp_kv, table, table_c)
    out = _out_proj(att_b.reshape(m, ATT_WIDTH), w_out_b[0], h1, norm_post[1])
    return out.reshape(batch, SEQ, D_MODEL)
```
